```python
import math
import jax, jax.numpy as jnp
from jax import lax
import numpy as np

D_MODEL = 2048
BATCH = 4
SEQ = 2048
DEPTH = 1

CHUNK = 64
Q_BLOCK = 128
ROPE_THETA = 10000.0
NORM_EPS = 1e-6

MLA_HEADS = 8
MLA_Q_RANK = 512
MLA_KV_RANK = 512
MLA_NOPE_DIM = 128
MLA_ROPE_DIM = 64
MLA_V_DIM = 128
MLA_QK_DIM = MLA_NOPE_DIM + MLA_ROPE_DIM

DIFF_HEADS = 4
DIFF_HEAD_DIM = 128
DIFF_V_DIM = 2 * DIFF_HEAD_DIM

MLA_OUT = MLA_HEADS * MLA_V_DIM
DIFF_OUT = DIFF_HEADS * DIFF_V_DIM
MIX_WIDTH = MLA_OUT + DIFF_OUT

DIFF_QK_W = DIFF_HEADS * 2 * DIFF_HEAD_DIM
IN_WIDTHS = (MLA_Q_RANK, MLA_KV_RANK, MLA_ROPE_DIM, DIFF_QK_W, DIFF_QK_W, DIFF_OUT)
IN_SPLITS = tuple(int(v) for v in np.cumsum(IN_WIDTHS)[:-1])
IN_WIDTH = int(sum(IN_WIDTHS))

D_FF = 5632

kernel_name = "hybrid_mla_diffattn_macaron_block"


def rms_norm(x, g):
    xf = x.astype(jnp.float32)
    y = xf * lax.rsqrt(jnp.mean(xf * xf, axis=-1, keepdims=True) + NORM_EPS)
    return (y * g.astype(jnp.float32)).astype(x.dtype)


def rotate_half(x):
    x1, x2 = jnp.split(x, 2, axis=-1)
    return jnp.concatenate([-x2, x1], axis=-1)


def apply_rope(x):
    s, d = x.shape[1], x.shape[-1]
    pos = jnp.arange(s, dtype=jnp.float32)
    inv_freq = ROPE_THETA ** (-jnp.arange(0, d, 2, dtype=jnp.float32) / d)
    ang = pos[:, None] * inv_freq[None, :]
    ang = jnp.concatenate([ang, ang], axis=-1)
    bshape = (1, s) + (1,) * (x.ndim - 3) + (d,)
    cos = jnp.cos(ang).reshape(bshape).astype(x.dtype)
    sin = jnp.sin(ang).reshape(bshape).astype(x.dtype)
    return x * cos + rotate_half(x) * sin


def swiglu(x, w_gate, w_up, w_down):
    return (jax.nn.silu(x @ w_gate) * (x @ w_up)) @ w_down


def query_blocks(t):
    b, s = t.shape[0], t.shape[1]
    return jnp.moveaxis(t.reshape((b, s // Q_BLOCK, Q_BLOCK) + t.shape[2:]), 1, 0)


def merge_blocks(t):
    t = jnp.moveaxis(t, 0, 1)
    return t.reshape((t.shape[0], t.shape[1] * t.shape[2]) + t.shape[3:])


def chunk_mask(start, seq):
    q_pos = start + jnp.arange(Q_BLOCK, dtype=jnp.int32)
    k_pos = jnp.arange(seq, dtype=jnp.int32)
    return (q_pos[:, None] // CHUNK) >= (k_pos[None, :] // CHUNK)


def block_starts(seq):
    return jnp.arange(seq // Q_BLOCK, dtype=jnp.int32) * Q_BLOCK


def mla_attention(q, k, v):
    seq = q.shape[1]
    scale = MLA_QK_DIM ** -0.5

    def step(args):
        qb, start = args
        s = jnp.einsum('bqhd,bkhd->bhqk', qb, k, preferred_element_type=jnp.float32) * scale
        s = jnp.where(chunk_mask(start, seq)[None, None], s, -jnp.inf)
        p = jax.nn.softmax(s, axis=-1).astype(v.dtype)
        return jnp.einsum('bhqk,bkhd->bqhd', p, v)

    return merge_blocks(lax.map(step, (query_blocks(q), block_starts(seq))))


def diff_attention(q, k, v, lam):
    seq = q.shape[1]
    scale = DIFF_HEAD_DIM ** -0.5

    def step(args):
        qb, start = args
        s = jnp.einsum('bqhcd,bkhcd->bhcqk', qb, k, preferred_element_type=jnp.float32) * scale
        s = jnp.where(chunk_mask(start, seq)[None, None, None], s, -jnp.inf)
        p = jax.nn.softmax(s, axis=-1)
        a = (p[:, :, 0] - lam * p[:, :, 1]).astype(v.dtype)
        return jnp.einsum('bhqk,bkhe->bqhe', a, v)

    return merge_blocks(lax.map(step, (query_blocks(q), block_starts(seq))))


def setup_inputs(seed: int = 0) -> dict:
    key = jax.random.key(seed)
    ks = iter(jax.random.split(key, 32))

    def w(shape, fan_in):
        return jax.random.normal(next(ks), (DEPTH,) + shape, jnp.float32) * (fan_in ** -0.5)

    def gain(dim):
        return 1.0 + 0.02 * jax.random.normal(next(ks), (DEPTH, dim), jnp.float32)

    def lam_vec():
        return 0.1 * jax.random.normal(next(ks), (DEPTH, DIFF_HEAD_DIM), jnp.float32)

    x = jax.random.normal(next(ks), (BATCH, SEQ, D_MODEL), jnp.float32)
    return {
        "x": x,
        "ffn1_pre_g": gain(D_MODEL),
        "ffn1_w_gate": w((D_MODEL, D_FF), D_MODEL),
        "ffn1_w_up": w((D_MODEL, D_FF), D_MODEL),
        "ffn1_w_down": w((D_FF, D_MODEL), D_FF),
        "ffn1_post_g": gain(D_MODEL),
        "mix_pre_g": gain(D_MODEL),
        "w_in": w((D_MODEL, IN_WIDTH), D_MODEL),
        "mla_q_norm_g": gain(MLA_Q_RANK),
        "mla_w_uq": w((MLA_Q_RANK, MLA_HEADS * MLA_QK_DIM), MLA_Q_RANK),
        "mla_kv_norm_g": gain(MLA_KV_RANK),
        "mla_w_ukv": w((MLA_KV_RANK, MLA_HEADS * (MLA_NOPE_DIM + MLA_V_DIM)), MLA_KV_RANK),
        "diff_lambda_q1": lam_vec(),
        "diff_lambda_k1": lam_vec(),
        "diff_lambda_q2": lam_vec(),
        "diff_lambda_k2": lam_vec(),
        "diff_subln_g": gain(DIFF_V_DIM),
        "w_out": w((MIX_WIDTH, D_MODEL), MIX_WIDTH),
        "mix_post_g": gain(D_MODEL),
        "ffn2_pre_g": gain(D_MODEL),
        "ffn2_w_gate": w((D_MODEL, D_FF), D_MODEL),
        "ffn2_w_up": w((D_MODEL, D_FF), D_MODEL),
        "ffn2_w_down": w((D_FF, D_MODEL), D_FF),
        "ffn2_post_g": gain(D_MODEL),
    }


def reference(x, ffn1_pre_g, ffn1_w_gate, ffn1_w_up, ffn1_w_down, ffn1_post_g,
              mix_pre_g, w_in, mla_q_norm_g, mla_w_uq, mla_kv_norm_g, mla_w_ukv,
              diff_lambda_q1, diff_lambda_k1, diff_lambda_q2, diff_lambda_k2, diff_subln_g,
              w_out, mix_post_g, ffn2_pre_g, ffn2_w_gate, ffn2_w_up, ffn2_w_down, ffn2_post_g):
    b, s, _ = x.shape
    for l in range(DEPTH):
        f = swiglu(rms_norm(x, ffn1_pre_g[l]), ffn1_w_gate[l], ffn1_w_up[l], ffn1_w_down[l])
        x = x + 0.5 * rms_norm(f, ffn1_post_g[l])

        h = rms_norm(x, mix_pre_g[l])
        c_q, c_kv, k_rope, dq, dk, dv = jnp.split(h @ w_in[l], IN_SPLITS, axis=-1)

        q = (rms_norm(c_q, mla_q_norm_g[l]) @ mla_w_uq[l]).reshape(b, s, MLA_HEADS, MLA_QK_DIM)
        q_nope, q_pe = jnp.split(q, [MLA_NOPE_DIM], axis=-1)
        q = jnp.concatenate([q_nope, apply_rope(q_pe)], axis=-1)
        kv = (rms_norm(c_kv, mla_kv_norm_g[l]) @ mla_w_ukv[l]).reshape(b, s, MLA_HEADS, MLA_NOPE_DIM + MLA_V_DIM)
        k_nope, v = jnp.split(kv, [MLA_NOPE_DIM], axis=-1)
        k_pe = jnp.broadcast_to(apply_rope(k_rope)[:, :, None, :], (b, s, MLA_HEADS, MLA_ROPE_DIM))
        k = jnp.concatenate([k_nope, k_pe], axis=-1)
        o_mla = mla_attention(q, k, v).reshape(b, s, MLA_OUT)

        lambda_init = 0.8 - 0.6 * math.exp(-0.3 * l)
        lam = (jnp.exp(jnp.sum(diff_lambda_q1[l].astype(jnp.float32) * diff_lambda_k1[l].astype(jnp.float32)))
               - jnp.exp(jnp.sum(diff_lambda_q2[l].astype(jnp.float32) * diff_lambda_k2[l].astype(jnp.float32)))
               + lambda_init)
        dq = apply_rope(dq.reshape(b, s, DIFF_HEADS, 2, DIFF_HEAD_DIM))
        dk = apply_rope(dk.reshape(b, s, DIFF_HEADS, 2, DIFF_HEAD_DIM))
        dv = dv.reshape(b, s, DIFF_HEADS, DIFF_V_DIM)
        o_diff = diff_attention(dq, dk, dv, lam)
        o_diff = (rms_norm(o_diff, diff_subln_g[l]) * (1.0 - lambda_init)).reshape(b, s, DIFF_OUT)

        o = jnp.concatenate([o_mla, o_diff], axis=-1) @ w_out[l]
        x = x + rms_norm(o, mix_post_g[l])

        f = swiglu(rms_norm(x, ffn2_pre_g[l]), ffn2_w_gate[l], ffn2_w_up[l], ffn2_w_down[l])
        x = x + 0.5 * rms_norm(f, ffn2_post_g[l])
    return x
```

```python
import functools
import math

import jax
import jax.numpy as jnp
import numpy as np
from jax import lax
from jax.experimental import pallas as pl
from jax.experimental.pallas import tpu as pltpu

D_MODEL = 2048
CHUNK = 64
ROPE_THETA = 10000.0
NORM_EPS = 1e-6
MLA_HEADS = 8
MLA_Q_RANK = 512
MLA_KV_RANK = 512
MLA_NOPE_DIM = 128
MLA_ROPE_DIM = 64
MLA_V_DIM = 128
MLA_QK_DIM = MLA_NOPE_DIM + MLA_ROPE_DIM
DIFF_HEADS = 4
DIFF_HEAD_DIM = 128
DIFF_V_DIM = 2 * DIFF_HEAD_DIM
MLA_OUT = MLA_HEADS * MLA_V_DIM
DIFF_OUT = DIFF_HEADS * DIFF_V_DIM
DIFF_QK_W = DIFF_HEADS * 2 * DIFF_HEAD_DIM
D_FF = 5632
LAMBDA_INIT = 0.8 - 0.6 * math.exp(-0.3 * 0)

V7X_LANES = 128
V7X_VMEM_BYTES = 64 * 1024 * 1024
MLA_HEAD_PAD = 2 * V7X_LANES

FFN_TM = 1024
FFN_TF = 512
FFN_ROWS = 128
PROJ_TM = 256
ATT_TQ = 256
ATT_TK = 256
OUT_TM = 512

BF16 = jnp.bfloat16
F32 = jnp.float32


def _vmem_limit(nbytes):
    return int(min(V7X_VMEM_BYTES - 8 * 1024 * 1024, max(32 * 1024 * 1024, nbytes * 3 // 2)))


def _rms(v, g):
    ms = jnp.mean(v * v, axis=-1, keepdims=True)
    return v * lax.rsqrt(ms + NORM_EPS) * g


def _ffn_kernel(x_ref, pre_g_ref, wg_ref, wu_ref, wd_ref, post_g_ref, o_ref, h_ref):
    f = pl.program_id(1)
    n_chunks = FFN_TM // FFN_ROWS

    @pl.when(f == 0)
    def _():
        def body(r, c):
            rows = pl.ds(pl.multiple_of(r * FFN_ROWS, FFN_ROWS), FFN_ROWS)
            h_ref[rows, :] = _rms(x_ref[rows, :], pre_g_ref[...]).astype(BF16)
            return c
        lax.fori_loop(0, n_chunks, body, 0)
        o_ref[...] = jnp.zeros_like(o_ref)

    h = h_ref[...]
    g = jnp.dot(h, wg_ref[...], preferred_element_type=F32)
    u = jnp.dot(h, wu_ref[...], preferred_element_type=F32)
    a = (g * (1.0 / (1.0 + jnp.exp(-g))) * u).astype(BF16)
    o_ref[...] += jnp.dot(a, wd_ref[...], preferred_element_type=F32)

    @pl.when(f == pl.num_programs(1) - 1)
    def _():
        def body(r, c):
            rows = pl.ds(pl.multiple_of(r * FFN_ROWS, FFN_ROWS), FFN_ROWS)
            o_ref[rows, :] = x_ref[rows, :] + 0.5 * _rms(o_ref[rows, :], post_g_ref[...])
            return c
        lax.fori_loop(0, n_chunks, body, 0)


def _ffn(x, pre_g, wg, wu, wd, post_g):
    t, d = x.shape
    nbytes = (2 * FFN_TM * d * 4 + FFN_TM * d * 2 + 2 * 3 * d * FFN_TF * 2
              + 3 * FFN_TM * FFN_TF * 4)
    return pl.pallas_call(
        _ffn_kernel,
        grid=(t // FFN_TM, D_FF // FFN_TF),
        in_specs=[
            pl.BlockSpec((FFN_TM, d), lambda i, f: (i, 0), pipeline_mode=pl.Buffered(1)),
            pl.BlockSpec((1, d), lambda i, f: (0, 0)),
            pl.BlockSpec((d, FFN_TF), lambda i, f: (0, f)),
            pl.BlockSpec((d, FFN_TF), lambda i, f: (0, f)),
            pl.BlockSpec((FFN_TF, d), lambda i, f: (f, 0)),
            pl.BlockSpec((1, d), lambda i, f: (0, 0)),
        ],
        out_specs=pl.BlockSpec((FFN_TM, d), lambda i, f: (i, 0), pipeline_mode=pl.Buffered(1)),
        out_shape=jax.ShapeDtypeStruct((t, d), F32),
        scratch_shapes=[pltpu.VMEM((FFN_TM, d), BF16)],
        compiler_params=pltpu.CompilerParams(
            dimension_semantics=("parallel", "arbitrary"),
            vmem_limit_bytes=_vmem_limit(nbytes)),
        name="ffn",
    )(x, pre_g, wg, wu, wd, post_g)


_PA = MLA_Q_RANK + MLA_KV_RANK + V7X_LANES
_PDQ = _PA
_PDK = _PDQ + DIFF_QK_W
_PDV = _PDK + DIFF_QK_W
_PW = _PDV + DIFF_OUT


def _rope_half(v, cos, sin_signed):
    return v * cos + pltpu.roll(v, DIFF_HEAD_DIM // 2, 1) * sin_signed


def _rope_mla(v, cos, sin_up, sin_dn):
    half = MLA_ROPE_DIM // 2
    return (v * cos + pltpu.roll(v, half, 1) * sin_up
            + pltpu.roll(v, V7X_LANES - half, 1) * sin_dn)


def _proj_kernel(x_ref, g_ref, win_ref, qg_ref, wuq_ref, kvg_ref, wukv_ref,
                 cm_ref, sup_ref, sdn_ref, cd_ref, sd_ref,
                 q_ref, k_ref, v_ref, dq_ref, dk_ref, dv_ref):
    h = _rms(x_ref[...], g_ref[...]).astype(BF16)

    pa = jnp.dot(h, win_ref[:, 0:_PA], preferred_element_type=F32)
    cq = _rms(pa[:, 0:MLA_Q_RANK], qg_ref[...]).astype(BF16)
    ckv = _rms(pa[:, MLA_Q_RANK:MLA_Q_RANK + MLA_KV_RANK], kvg_ref[...]).astype(BF16)
    cm, sup, sdn = cm_ref[...], sup_ref[...], sdn_ref[...]
    kpe = _rope_mla(pa[:, MLA_Q_RANK + MLA_KV_RANK:_PA], cm, sup, sdn).astype(BF16)

    q = jnp.dot(cq, wuq_ref[...], preferred_element_type=F32)
    kv = jnp.dot(ckv, wukv_ref[...], preferred_element_type=F32)
    for hh in range(MLA_HEADS):
        a = hh * MLA_HEAD_PAD
        b = a + V7X_LANES
        c = b + V7X_LANES
        q_ref[:, a:b] = q[:, a:b].astype(BF16)
        q_ref[:, b:c] = _rope_mla(q[:, b:c], cm, sup, sdn).astype(BF16)
        k_ref[:, a:b] = kv[:, a:b].astype(BF16)
        k_ref[:, b:c] = kpe
        v_ref[:, hh * MLA_V_DIM:(hh + 1) * MLA_V_DIM] = kv[:, b:c].astype(BF16)

    cd, sd = cd_ref[...], sd_ref[...]
    pdq = jnp.dot(h, win_ref[:, _PDQ:_PDK], preferred_element_type=F32)
    for c in range(DIFF_QK_W // DIFF_HEAD_DIM):
        cols = slice(c * DIFF_HEAD_DIM, (c + 1) * DIFF_HEAD_DIM)
        dq_ref[:, cols] = _rope_half(pdq[:, cols], cd, sd).astype(BF16)
    pdk = jnp.dot(h, win_ref[:, _PDK:_PDV], preferred_element_type=F32)
    for c in range(DIFF_QK_W // DIFF_HEAD_DIM):
        cols = slice(c * DIFF_HEAD_DIM, (c + 1) * DIFF_HEAD_DIM)
        dk_ref[:, cols] = _rope_half(pdk[:, cols], cd, sd).astype(BF16)
    dv_ref[...] = jnp.dot(h, win_ref[:, _PDV:_PW], preferred_element_type=F32).astype(BF16)


def _mix_proj(x, g, win_p, qg, wuq_p, kvg, wukv, tables, seq):
    t, d = x.shape
    tm = PROJ_TM
    pos_blocks = seq // tm
    const = lambda i: (0, 0)
    row = lambda i: (i, 0)
    pos = lambda i: (i % pos_blocks, 0)
    wq = MLA_HEADS * MLA_HEAD_PAD
    out_w = (wq, wq, MLA_OUT, DIFF_QK_W, DIFF_QK_W, DIFF_OUT)
    nbytes = (d * _PW * 2 + 2 * MLA_Q_RANK * wq * 2 + 2 * tm * d * 4
              + 2 * tm * sum(out_w) * 2 + tm * _PW * 4 + 2 * tm * wq * 4)
    return pl.pallas_call(
        _proj_kernel,
        grid=(t // tm,),
        in_specs=[
            pl.BlockSpec((tm, d), row),
            pl.BlockSpec((1, d), const),
            pl.BlockSpec((d, _PW), const, pipeline_mode=pl.Buffered(1)),
            pl.BlockSpec((1, MLA_Q_RANK), const),
            pl.BlockSpec((MLA_Q_RANK, wq), const, pipeline_mode=pl.Buffered(1)),
            pl.BlockSpec((1, MLA_KV_RANK), const),
            pl.BlockSpec((MLA_KV_RANK, wq), const, pipeline_mode=pl.Buffered(1)),
        ] + [pl.BlockSpec((tm, V7X_LANES), pos)] * 5,
        out_specs=[pl.BlockSpec((tm, w), row) for w in out_w],
        out_shape=[jax.ShapeDtypeStruct((t, w), BF16) for w in out_w],
        compiler_params=pltpu.CompilerParams(
            dimension_semantics=("parallel",),
            vmem_limit_bytes=_vmem_limit(nbytes)),
        name="mix_proj",
    )(x, g, win_p, qg, wuq_p, kvg, wukv, *tables)


def _diag_mask():
    r = lax.broadcasted_iota(jnp.int32, (ATT_TQ, ATT_TK), 0) // CHUNK
    c = lax.broadcasted_iota(jnp.int32, (ATT_TQ, ATT_TK), 1) // CHUNK
    return r >= c


def _scores(q, k, scale):
    return lax.dot_general(q, k, (((1,), (1,)), ((), ())), preferred_element_type=F32) * scale


def _flash_first(s, v):
    m = jnp.max(s, axis=-1, keepdims=True)
    p = jnp.exp(s - m)
    l = jnp.sum(p, axis=-1, keepdims=True)
    acc = jnp.dot(p.astype(BF16), v, preferred_element_type=F32)
    return m, l, acc


def _flash_step(s, v, m, l, acc):
    m_new = jnp.maximum(m, jnp.max(s, axis=-1, keepdims=True))
    alpha = jnp.exp(m - m_new)
    p = jnp.exp(s - m_new)
    l = alpha * l + jnp.sum(p, axis=-1, keepdims=True)
    acc = alpha * acc + jnp.dot(p.astype(BF16), v, preferred_element_type=F32)
    return m_new, l, acc


def _mla_attn_kernel(q_ref, k_ref, v_ref, o_ref):
    qi = pl.program_id(2)
    scale = MLA_QK_DIM ** -0.5
    q = q_ref[...]
    diag = pl.ds(pl.multiple_of(qi * ATT_TK, ATT_TK), ATT_TK)
    s = jnp.where(_diag_mask(), _scores(q, k_ref[diag, :], scale), -jnp.inf)
    state = _flash_first(s, v_ref[diag, :])

    def body(j, st):
        rows = pl.ds(pl.multiple_of(j * ATT_TK, ATT_TK), ATT_TK)
        return _flash_step(_scores(q, k_ref[rows, :], scale), v_ref[rows, :], *st)

    m, l, acc = lax.fori_loop(0, qi, body, state)
    o_ref[...] = (acc / l).astype(o_ref.dtype)


def _mla_attn(q, k, v, batch, seq):
    nq = seq // ATT_TQ
    return pl.pallas_call(
        _mla_attn_kernel,
        grid=(batch, MLA_HEADS, nq),
        in_specs=[
            pl.BlockSpec((ATT_TQ, MLA_HEAD_PAD), lambda b, h, i: (b * nq + i, h)),
            pl.BlockSpec((seq, MLA_HEAD_PAD), lambda b, h, i: (b, h)),
            pl.BlockSpec((seq, MLA_V_DIM), lambda b, h, i: (b, h)),
        ],
        out_specs=pl.BlockSpec((ATT_TQ, MLA_V_DIM), lambda b, h, i: (b * nq + i, h)),
        out_shape=jax.ShapeDtypeStruct((batch * seq, MLA_OUT), BF16),
        compiler_params=pltpu.CompilerParams(
            dimension_semantics=("parallel", "parallel", "parallel")),
        name="mla_attn",
    )(q, k, v)


def _diff_attn_kernel(q_ref, k_ref, v_ref, lq1_ref, lk1_ref, lq2_ref, lk2_ref, g_ref, o_ref):
    qi = pl.program_id(2)
    scale = DIFF_HEAD_DIM ** -0.5
    d = DIFF_HEAD_DIM
    q1 = q_ref[:, 0:d]
    q2 = q_ref[:, d:2 * d]
    diag = pl.ds(pl.multiple_of(qi * ATT_TK, ATT_TK), ATT_TK)
    mask = _diag_mask()
    vd = v_ref[diag, :]
    s1 = jnp.where(mask, _scores(q1, k_ref[diag, 0:d], scale), -jnp.inf)
    s2 = jnp.where(mask, _scores(q2, k_ref[diag, d:2 * d], scale), -jnp.inf)
    state = _flash_first(s1, vd) + _flash_first(s2, vd)

    def body(j, st):
        rows = pl.ds(pl.multiple_of(j * ATT_TK, ATT_TK), ATT_TK)
        vj = v_ref[rows, :]
        st1 = _flash_step(_scores(q1, k_ref[rows, 0:d], scale), vj, *st[0:3])
        st2 = _flash_step(_scores(q2, k_ref[rows, d:2 * d], scale), vj, *st[3:6])
        return st1 + st2

    m1, l1, a1, m2, l2, a2 = lax.fori_loop(0, qi, body, state)
    lam = (jnp.exp(jnp.sum(lq1_ref[...] * lk1_ref[...], axis=-1, keepdims=True))
           - jnp.exp(jnp.sum(lq2_ref[...] * lk2_ref[...], axis=-1, keepdims=True))
           + LAMBDA_INIT)
    o = a1 / l1 - lam * (a2 / l2)
    o_ref[...] = (_rms(o, g_ref[...]) * (1.0 - LAMBDA_INIT)).astype(o_ref.dtype)


def _diff_attn(q, k, v, lq1, lk1, lq2, lk2, g, batch, seq):
    nq = seq // ATT_TQ
    w = 2 * DIFF_HEAD_DIM
    vec = pl.BlockSpec((1, DIFF_HEAD_DIM), lambda b, h, i: (0, 0))
    return pl.pallas_call(
        _diff_attn_kernel,
        grid=(batch, DIFF_HEADS, nq),
        in_specs=[
            pl.BlockSpec((ATT_TQ, w), lambda b, h, i: (b * nq + i, h)),
            pl.BlockSpec((seq, w), lambda b, h, i: (b, h)),
            pl.BlockSpec((seq, DIFF_V_DIM), lambda b, h, i: (b, h)),
            vec, vec, vec, vec,
            pl.BlockSpec((1, DIFF_V_DIM), lambda b, h, i: (0, 0)),
        ],
        out_specs=pl.BlockSpec((ATT_TQ, DIFF_V_DIM), lambda b, h, i: (b * nq + i, h)),
        out_shape=jax.ShapeDtypeStruct((batch * seq, DIFF_OUT), BF16),
        compiler_params=pltpu.CompilerParams(
            dimension_semantics=("parallel", "parallel", "parallel")),
        name="diff_attn",
    )(q, k, v, lq1, lk1, lq2, lk2, g)


def _out_kernel(x_ref, oa_ref, ob_ref, wa_ref, wb_ref, g_ref, o_ref):
    o = (jnp.dot(oa_ref[...], wa_ref[...], preferred_element_type=F32)
         + jnp.dot(ob_ref[...], wb_ref[...], preferred_element_type=F32))
    o_ref[...] = x_ref[...] + _rms(o, g_ref[...])


def _out_proj(x, oa, ob, wa, wb, g):
    t, d = x.shape
    tm = OUT_TM
    const = lambda i: (0, 0)
    row = lambda i: (i, 0)
    nbytes = (2 * d * d * 2 + 4 * tm * d * 4 + 2 * tm * d * 2 + 2 * tm * d * 4)
    return pl.pallas_call(
        _out_kernel,
        grid=(t // tm,),
        in_specs=[
            pl.BlockSpec((tm, d), row),
            pl.BlockSpec((tm, MLA_OUT), row),
            pl.BlockSpec((tm, DIFF_OUT), row),
            pl.BlockSpec((MLA_OUT, d), const),
            pl.BlockSpec((DIFF_OUT, d), const),
            pl.BlockSpec((1, d), const),
        ],
        out_specs=pl.BlockSpec((tm, d), row),
        out_shape=jax.ShapeDtypeStruct((t, d), F32),
        compiler_params=pltpu.CompilerParams(
            dimension_semantics=("parallel",),
            vmem_limit_bytes=_vmem_limit(nbytes)),
        name="out_proj",
    )(x, oa, ob, wa, wb, g)


def _pack_w_in(w_in):
    a = MLA_Q_RANK + MLA_KV_RANK
    b = a + MLA_ROPE_DIM
    pad = jnp.zeros((w_in.shape[0], V7X_LANES - MLA_ROPE_DIM), w_in.dtype)
    return jnp.concatenate([w_in[:, :b], pad, w_in[:, b:]], axis=1).astype(BF16)


def _pack_w_uq(w_uq):
    r = w_uq.shape[0]
    w = w_uq.reshape(r, MLA_HEADS, MLA_QK_DIM)
    pad = jnp.zeros((r, MLA_HEADS, MLA_HEAD_PAD - MLA_QK_DIM), w_uq.dtype)
    return jnp.concatenate([w, pad], axis=-1).reshape(r, MLA_HEADS * MLA_HEAD_PAD).astype(BF16)


def _rope_tables(seq):
    pos = jnp.arange(seq, dtype=F32)[:, None]

    def angles(d):
        inv_freq = ROPE_THETA ** (-jnp.arange(0, d, 2, dtype=F32) / d)
        return pos * inv_freq[None, :]

    ang = angles(MLA_ROPE_DIM)
    zeros = jnp.zeros((seq, V7X_LANES - MLA_ROPE_DIM), F32)
    half0 = jnp.zeros_like(ang)
    cos_m = jnp.concatenate([jnp.cos(ang), jnp.cos(ang), zeros], axis=1)
    sin_up = jnp.concatenate([half0, jnp.sin(ang), zeros], axis=1)
    sin_dn = jnp.concatenate([-jnp.sin(ang), half0, zeros], axis=1)
    ang = angles(DIFF_HEAD_DIM)
    cos_d = jnp.concatenate([jnp.cos(ang), jnp.cos(ang)], axis=1)
    sin_d = jnp.concatenate([-jnp.sin(ang), jnp.sin(ang)], axis=1)
    return cos_m, sin_up, sin_dn, cos_d, sin_d


def kernel(x, ffn1_pre_g, ffn1_w_gate, ffn1_w_up, ffn1_w_down, ffn1_post_g, mix_pre_g, w_in, mla_q_norm_g, mla_w_uq, mla_kv_norm_g, mla_w_ukv, diff_lambda_q1, diff_lambda_k1, diff_lambda_q2, diff_lambda_k2, diff_subln_g, w_out, mix_post_g, ffn2_pre_g, ffn2_w_gate, ffn2_w_up, ffn2_w_down, ffn2_post_g):
    batch, seq, d = x.shape
    depth = ffn1_pre_g.shape[0]
    assert depth == 1 and d == D_MODEL
    assert seq % ATT_TQ == 0 and seq % PROJ_TM == 0 and (batch * seq) % FFN_TM == 0
    tables = _rope_tables(seq)
    xt = x.reshape(batch * seq, d)
    for l in range(depth):
        xt = _ffn(xt, ffn1_pre_g[l][None], ffn1_w_gate[l].astype(BF16), ffn1_w_up[l].astype(BF16),
                  ffn1_w_down[l].astype(BF16), ffn1_post_g[l][None])
        q, k, v, dq, dk, dv = _mix_proj(
            xt, mix_pre_g[l][None], _pack_w_in(w_in[l]), mla_q_norm_g[l][None],
            _pack_w_uq(mla_w_uq[l]), mla_kv_norm_g[l][None], mla_w_ukv[l].astype(BF16),
            tables, seq)
        o_mla = _mla_attn(q, k, v, batch, seq)
        o_diff = _diff_attn(dq, dk, dv, diff_lambda_q1[l][None], diff_lambda_k1[l][None],
                            diff_lambda_q2[l][None], diff_lambda_k2[l][None],
                            diff_subln_g[l][None], batch, seq)
        wo = w_out[l].astype(BF16)
        xt = _out_proj(xt, o_mla, o_diff, wo[:MLA_OUT], wo[MLA_OUT:], mix_post_g[l][None])
        xt = _ffn(xt, ffn2_pre_g[l][None], ffn2_w_gate[l].astype(BF16), ffn2_w_up[l].astype(BF16),
                  ffn2_w_down[l].astype(BF16), ffn2_post_g[l][None])
    return xt.reshape(batch, seq, d)
```

```python
import functools
import math

import jax
import jax.numpy as jnp
import numpy as np
from jax import lax
from jax.experimental import pallas as pl
from jax.experimental.pallas import tpu as pltpu

D_MODEL = 2048
CHUNK = 64
ROPE_THETA = 10000.0
NORM_EPS = 1e-6
MLA_HEADS = 8
MLA_Q_RANK = 512
MLA_KV_RANK = 512
MLA_NOPE_DIM = 128
MLA_ROPE_DIM = 64
MLA_V_DIM = 128
MLA_QK_DIM = MLA_NOPE_DIM + MLA_ROPE_DIM
DIFF_HEADS = 4
DIFF_HEAD_DIM = 128
DIFF_V_DIM = 2 * DIFF_HEAD_DIM
MLA_OUT = MLA_HEADS * MLA_V_DIM
DIFF_OUT = DIFF_HEADS * DIFF_V_DIM
DIFF_QK_W = DIFF_HEADS * 2 * DIFF_HEAD_DIM
D_FF = 5632
LAMBDA_INIT = 0.8 - 0.6 * math.exp(-0.3 * 0)
LOG2_E = math.log2(math.e)

V7X_LANES = 128
V7X_VMEM_BYTES = 64 * 1024 * 1024
MLA_HEAD_PAD = 2 * V7X_LANES

FFN_TM = 1024
FFN_TF = 512
FFN_ROWS = 128
PROJ_TM = 256
ATT_TQ = 512
ATT_TK = 512
MLA_GROUP = 4
DIFF_GROUP = 2
OUT_TM = 512

BF16 = jnp.bfloat16
F32 = jnp.float32


def _vmem_limit(nbytes):
    return int(min(V7X_VMEM_BYTES - 8 * 1024 * 1024, max(32 * 1024 * 1024, nbytes * 3 // 2)))


def _rms(v, g):
    ms = jnp.mean(v * v, axis=-1, keepdims=True)
    return v * lax.rsqrt(ms + NORM_EPS) * g


def _ffn_kernel(x_ref, pre_g_ref, wg_ref, wu_ref, wd_ref, post_g_ref, o_ref, h_ref):
    f = pl.program_id(1)
    n_chunks = FFN_TM // FFN_ROWS

    @pl.when(f == 0)
    def _():
        def body(r, c):
            rows = pl.ds(pl.multiple_of(r * FFN_ROWS, FFN_ROWS), FFN_ROWS)
            h_ref[rows, :] = _rms(x_ref[rows, :], pre_g_ref[...]).astype(BF16)
            return c
        lax.fori_loop(0, n_chunks, body, 0)
        o_ref[...] = jnp.zeros_like(o_ref)

    h = h_ref[...]
    g = jnp.dot(h, wg_ref[...], preferred_element_type=F32)
    u = jnp.dot(h, wu_ref[...], preferred_element_type=F32)
    a = (g * (1.0 / (1.0 + jnp.exp(-g))) * u).astype(BF16)
    o_ref[...] += jnp.dot(a, wd_ref[...], preferred_element_type=F32)

    @pl.when(f == pl.num_programs(1) - 1)
    def _():
        def body(r, c):
            rows = pl.ds(pl.multiple_of(r * FFN_ROWS, FFN_ROWS), FFN_ROWS)
            o_ref[rows, :] = x_ref[rows, :] + 0.5 * _rms(o_ref[rows, :], post_g_ref[...])
            return c
        lax.fori_loop(0, n_chunks, body, 0)


def _ffn(x, pre_g, wg, wu, wd, post_g):
    t, d = x.shape
    nbytes = (2 * FFN_TM * d * 4 + FFN_TM * d * 2 + 2 * 3 * d * FFN_TF * 2
              + 3 * FFN_TM * FFN_TF * 4)
    return pl.pallas_call(
        _ffn_kernel,
        grid=(t // FFN_TM, D_FF // FFN_TF),
        in_specs=[
            pl.BlockSpec((FFN_TM, d), lambda i, f: (i, 0), pipeline_mode=pl.Buffered(1)),
            pl.BlockSpec((1, d), lambda i, f: (0, 0)),
            pl.BlockSpec((d, FFN_TF), lambda i, f: (0, f)),
            pl.BlockSpec((d, FFN_TF), lambda i, f: (0, f)),
            pl.BlockSpec((FFN_TF, d), lambda i, f: (f, 0)),
            pl.BlockSpec((1, d), lambda i, f: (0, 0)),
        ],
        out_specs=pl.BlockSpec((FFN_TM, d), lambda i, f: (i, 0), pipeline_mode=pl.Buffered(1)),
        out_shape=jax.ShapeDtypeStruct((t, d), F32),
        scratch_shapes=[pltpu.VMEM((FFN_TM, d), BF16)],
        compiler_params=pltpu.CompilerParams(
            dimension_semantics=("parallel", "arbitrary"),
            vmem_limit_bytes=_vmem_limit(nbytes)),
        name="ffn",
    )(x, pre_g, wg, wu, wd, post_g)


_PA = MLA_Q_RANK + MLA_KV_RANK + V7X_LANES
_PDQ = _PA
_PDK = _PDQ + DIFF_QK_W
_PDV = _PDK + DIFF_QK_W
_PW = _PDV + DIFF_OUT


def _rope_half(v, cos, sin_signed):
    return v * cos + pltpu.roll(v, DIFF_HEAD_DIM // 2, 1) * sin_signed


def _rope_mla(v, cos, sin_up, sin_dn):
    half = MLA_ROPE_DIM // 2
    return (v * cos + pltpu.roll(v, half, 1) * sin_up
            + pltpu.roll(v, V7X_LANES - half, 1) * sin_dn)


def _proj_kernel(x_ref, g_ref, win_ref, qg_ref, wuq_ref, kvg_ref, wukv_ref,
                 cm_ref, sup_ref, sdn_ref, cd_ref, sd_ref,
                 q_ref, k_ref, v_ref, dq_ref, dk_ref, dv_ref):
    h = _rms(x_ref[...], g_ref[...]).astype(BF16)

    pa = jnp.dot(h, win_ref[:, 0:_PA], preferred_element_type=F32)
    cq = _rms(pa[:, 0:MLA_Q_RANK], qg_ref[...]).astype(BF16)
    ckv = _rms(pa[:, MLA_Q_RANK:MLA_Q_RANK + MLA_KV_RANK], kvg_ref[...]).astype(BF16)
    cm, sup, sdn = cm_ref[...], sup_ref[...], sdn_ref[...]
    kpe = _rope_mla(pa[:, MLA_Q_RANK + MLA_KV_RANK:_PA], cm, sup, sdn).astype(BF16)

    q = jnp.dot(cq, wuq_ref[...], preferred_element_type=F32)
    kv = jnp.dot(ckv, wukv_ref[...], preferred_element_type=F32)
    for hh in range(MLA_HEADS):
        a = hh * MLA_HEAD_PAD
        b = a + V7X_LANES
        c = b + V7X_LANES
        q_ref[:, a:b] = q[:, a:b].astype(BF16)
        q_ref[:, b:c] = _rope_mla(q[:, b:c], cm, sup, sdn).astype(BF16)
        k_ref[:, a:b] = kv[:, a:b].astype(BF16)
        k_ref[:, b:c] = kpe
        v_ref[:, hh * MLA_V_DIM:(hh + 1) * MLA_V_DIM] = kv[:, b:c].astype(BF16)

    cd, sd = cd_ref[...], sd_ref[...]
    pdq = jnp.dot(h, win_ref[:, _PDQ:_PDK], preferred_element_type=F32)
    for c in range(DIFF_QK_W // DIFF_HEAD_DIM):
        cols = slice(c * DIFF_HEAD_DIM, (c + 1) * DIFF_HEAD_DIM)
        dq_ref[:, cols] = _rope_half(pdq[:, cols], cd, sd).astype(BF16)
    pdk = jnp.dot(h, win_ref[:, _PDK:_PDV], preferred_element_type=F32)
    for c in range(DIFF_QK_W // DIFF_HEAD_DIM):
        cols = slice(c * DIFF_HEAD_DIM, (c + 1) * DIFF_HEAD_DIM)
        dk_ref[:, cols] = _rope_half(pdk[:, cols], cd, sd).astype(BF16)
    dv_ref[...] = jnp.dot(h, win_ref[:, _PDV:_PW], preferred_element_type=F32).astype(BF16)


def _mix_proj(x, g, win_p, qg, wuq_p, kvg, wukv, tables, seq):
    t, d = x.shape
    tm = PROJ_TM
    pos_blocks = seq // tm
    const = lambda i: (0, 0)
    row = lambda i: (i, 0)
    pos = lambda i: (i % pos_blocks, 0)
    wq = MLA_HEADS * MLA_HEAD_PAD
    out_w = (wq, wq, MLA_OUT, DIFF_QK_W, DIFF_QK_W, DIFF_OUT)
    nbytes = (d * _PW * 2 + 2 * MLA_Q_RANK * wq * 2 + 2 * tm * d * 4
              + 2 * tm * sum(out_w) * 2 + tm * _PW * 4 + 2 * tm * wq * 4)
    return pl.pallas_call(
        _proj_kernel,
        grid=(t // tm,),
        in_specs=[
            pl.BlockSpec((tm, d), row),
            pl.BlockSpec((1, d), const),
            pl.BlockSpec((d, _PW), const, pipeline_mode=pl.Buffered(1)),
            pl.BlockSpec((1, MLA_Q_RANK), const),
            pl.BlockSpec((MLA_Q_RANK, wq), const, pipeline_mode=pl.Buffered(1)),
            pl.BlockSpec((1, MLA_KV_RANK), const),
            pl.BlockSpec((MLA_KV_RANK, wq), const, pipeline_mode=pl.Buffered(1)),
        ] + [pl.BlockSpec((tm, V7X_LANES), pos)] * 5,
        out_specs=[pl.BlockSpec((tm, w), row) for w in out_w],
        out_shape=[jax.ShapeDtypeStruct((t, w), BF16) for w in out_w],
        compiler_params=pltpu.CompilerParams(
            dimension_semantics=("parallel",),
            vmem_limit_bytes=_vmem_limit(nbytes)),
        name="mix_proj",
    )(x, g, win_p, qg, wuq_p, kvg, wukv, *tables)


def _diag_mask():
    r = lax.broadcasted_iota(jnp.int32, (ATT_TQ, ATT_TK), 0) // CHUNK
    c = lax.broadcasted_iota(jnp.int32, (ATT_TQ, ATT_TK), 1) // CHUNK
    return r >= c


def _scores(q, k, scale_log2):
    t = lax.dot_general(q, k, (((1,), (1,)), ((), ())), preferred_element_type=F32)
    return t * scale_log2


def _flash_first(t, v):
    m = jnp.max(t, axis=-1, keepdims=True)
    p = jnp.exp2(t - m)
    l = jnp.sum(p, axis=-1, keepdims=True)
    acc = jnp.dot(p.astype(BF16), v, preferred_element_type=F32)
    return m, l, acc


def _flash_step(t, v, m, l, acc):
    m_new = jnp.maximum(m, jnp.max(t, axis=-1, keepdims=True))
    alpha = jnp.exp2(m - m_new)
    p = jnp.exp2(t - m_new)
    l = alpha * l + jnp.sum(p, axis=-1, keepdims=True)
    acc = alpha * acc + jnp.dot(p.astype(BF16), v, preferred_element_type=F32)
    return m_new, l, acc


def _mla_attn_kernel(q_ref, k_ref, v_ref, o_ref):
    qi = pl.program_id(2)
    c = MLA_QK_DIM ** -0.5 * LOG2_E
    wq, wv = MLA_HEAD_PAD, MLA_V_DIM
    heads = range(MLA_GROUP)
    diag = pl.ds(pl.multiple_of(qi * ATT_TK, ATT_TK), ATT_TK)
    mask = _diag_mask()
    state = ()
    for g in heads:
        t = _scores(q_ref[:, g * wq:(g + 1) * wq], k_ref[diag, g * wq:(g + 1) * wq], c)
        state += _flash_first(jnp.where(mask, t, -jnp.inf), v_ref[diag, g * wv:(g + 1) * wv])

    def body(j, st):
        rows = pl.ds(pl.multiple_of(j * ATT_TK, ATT_TK), ATT_TK)
        out = ()
        for g in heads:
            t = _scores(q_ref[:, g * wq:(g + 1) * wq], k_ref[rows, g * wq:(g + 1) * wq], c)
            out += _flash_step(t, v_ref[rows, g * wv:(g + 1) * wv], *st[3 * g:3 * g + 3])
        return out

    state = lax.fori_loop(0, qi, body, state)
    for g in heads:
        m, l, acc = state[3 * g:3 * g + 3]
        o_ref[:, g * wv:(g + 1) * wv] = (acc / l).astype(o_ref.dtype)


def _mla_attn(q, k, v, batch, seq):
    nq = seq // ATT_TQ
    wq, wv = MLA_GROUP * MLA_HEAD_PAD, MLA_GROUP * MLA_V_DIM
    return pl.pallas_call(
        _mla_attn_kernel,
        grid=(batch, MLA_HEADS // MLA_GROUP, nq),
        in_specs=[
            pl.BlockSpec((ATT_TQ, wq), lambda b, h, i: (b * nq + i, h)),
            pl.BlockSpec((seq, wq), lambda b, h, i: (b, h)),
            pl.BlockSpec((seq, wv), lambda b, h, i: (b, h)),
        ],
        out_specs=pl.BlockSpec((ATT_TQ, wv), lambda b, h, i: (b * nq + i, h)),
        out_shape=jax.ShapeDtypeStruct((batch * seq, MLA_OUT), BF16),
        compiler_params=pltpu.CompilerParams(
            dimension_semantics=("parallel", "parallel", "parallel"),
            vmem_limit_bytes=_vmem_limit(2 * (seq + ATT_TQ) * (wq + wv) * 2
                                         + 8 * MLA_GROUP * ATT_TQ * ATT_TK * 4)),
        name="mla_attn",
    )(q, k, v)


def _diff_attn_kernel(q_ref, k_ref, v_ref, lq1_ref, lk1_ref, lq2_ref, lk2_ref, g_ref, o_ref):
    qi = pl.program_id(2)
    c = DIFF_HEAD_DIM ** -0.5 * LOG2_E
    d, wv = DIFF_HEAD_DIM, DIFF_V_DIM
    maps = range(2 * DIFF_GROUP)
    diag = pl.ds(pl.multiple_of(qi * ATT_TK, ATT_TK), ATT_TK)
    mask = _diag_mask()
    state = ()
    for n in maps:
        t = _scores(q_ref[:, n * d:(n + 1) * d], k_ref[diag, n * d:(n + 1) * d], c)
        state += _flash_first(jnp.where(mask, t, -jnp.inf),
                              v_ref[diag, (n // 2) * wv:(n // 2 + 1) * wv])

    def body(j, st):
        rows = pl.ds(pl.multiple_of(j * ATT_TK, ATT_TK), ATT_TK)
        out = ()
        for n in maps:
            t = _scores(q_ref[:, n * d:(n + 1) * d], k_ref[rows, n * d:(n + 1) * d], c)
            out += _flash_step(t, v_ref[rows, (n // 2) * wv:(n // 2 + 1) * wv],
                               *st[3 * n:3 * n + 3])
        return out

    state = lax.fori_loop(0, qi, body, state)
    lam = (jnp.exp(jnp.sum(lq1_ref[...] * lk1_ref[...], axis=-1, keepdims=True))
           - jnp.exp(jnp.sum(lq2_ref[...] * lk2_ref[...], axis=-1, keepdims=True))
           + LAMBDA_INIT)
    for h in range(DIFF_GROUP):
        _, l1, a1, _, l2, a2 = state[6 * h:6 * h + 6]
        o = a1 / l1 - lam * (a2 / l2)
        o_ref[:, h * wv:(h + 1) * wv] = (
            _rms(o, g_ref[...]) * (1.0 - LAMBDA_INIT)).astype(o_ref.dtype)


def _diff_attn(q, k, v, lq1, lk1, lq2, lk2, g, batch, seq):
    nq = seq // ATT_TQ
    w = DIFF_GROUP * DIFF_V_DIM
    vec = pl.BlockSpec((1, DIFF_HEAD_DIM), lambda b, h, i: (0, 0))
    return pl.pallas_call(
        _diff_attn_kernel,
        grid=(batch, DIFF_HEADS // DIFF_GROUP, nq),
        in_specs=[
            pl.BlockSpec((ATT_TQ, w), lambda b, h, i: (b * nq + i, h)),
            pl.BlockSpec((seq, w), lambda b, h, i: (b, h)),
            pl.BlockSpec((seq, w), lambda b, h, i: (b, h)),
            vec, vec, vec, vec,
            pl.BlockSpec((1, DIFF_V_DIM), lambda b, h, i: (0, 0)),
        ],
        out_specs=pl.BlockSpec((ATT_TQ, w), lambda b, h, i: (b * nq + i, h)),
        out_shape=jax.ShapeDtypeStruct((batch * seq, DIFF_OUT), BF16),
        compiler_params=pltpu.CompilerParams(
            dimension_semantics=("parallel", "parallel", "parallel"),
            vmem_limit_bytes=_vmem_limit(2 * (2 * seq + 2 * ATT_TQ) * w * 2
                                         + 16 * DIFF_GROUP * ATT_TQ * ATT_TK * 4)),
        name="diff_attn",
    )(q, k, v, lq1, lk1, lq2, lk2, g)


def _out_kernel(x_ref, oa_ref, ob_ref, wa_ref, wb_ref, g_ref, o_ref):
    o = (jnp.dot(oa_ref[...], wa_ref[...], preferred_element_type=F32)
         + jnp.dot(ob_ref[...], wb_ref[...], preferred_element_type=F32))
    o_ref[...] = x_ref[...] + _rms(o, g_ref[...])


def _out_proj(x, oa, ob, wa, wb, g):
    t, d = x.shape
    tm = OUT_TM
    const = lambda i: (0, 0)
    row = lambda i: (i, 0)
    nbytes = (2 * d * d * 2 + 4 * tm * d * 4 + 2 * tm * d * 2 + 2 * tm * d * 4)
    return pl.pallas_call(
        _out_kernel,
        grid=(t // tm,),
        in_specs=[
            pl.BlockSpec((tm, d), row),
            pl.BlockSpec((tm, MLA_OUT), row),
            pl.BlockSpec((tm, DIFF_OUT), row),
            pl.BlockSpec((MLA_OUT, d), const),
            pl.BlockSpec((DIFF_OUT, d), const),
            pl.BlockSpec((1, d), const),
        ],
        out_specs=pl.BlockSpec((tm, d), row),
        out_shape=jax.ShapeDtypeStruct((t, d), F32),
        compiler_params=pltpu.CompilerParams(
            dimension_semantics=("parallel",),
            vmem_limit_bytes=_vmem_limit(nbytes)),
        name="out_proj",
    )(x, oa, ob, wa, wb, g)


def _pack_w_in(w_in):
    a = MLA_Q_RANK + MLA_KV_RANK
    b = a + MLA_ROPE_DIM
    pad = jnp.zeros((w_in.shape[0], V7X_LANES - MLA_ROPE_DIM), w_in.dtype)
    return jnp.concatenate([w_in[:, :b], pad, w_in[:, b:]], axis=1).astype(BF16)


def _pack_w_uq(w_uq):
    r = w_uq.shape[0]
    w = w_uq.reshape(r, MLA_HEADS, MLA_QK_DIM)
    pad = jnp.zeros((r, MLA_HEADS, MLA_HEAD_PAD - MLA_QK_DIM), w_uq.dtype)
    return jnp.concatenate([w, pad], axis=-1).reshape(r, MLA_HEADS * MLA_HEAD_PAD).astype(BF16)


def _rope_tables(seq):
    pos = jnp.arange(seq, dtype=F32)[:, None]

    def angles(d):
        inv_freq = ROPE_THETA ** (-jnp.arange(0, d, 2, dtype=F32) / d)
        return pos * inv_freq[None, :]

    ang = angles(MLA_ROPE_DIM)
    zeros = jnp.zeros((seq, V7X_LANES - MLA_ROPE_DIM), F32)
    half0 = jnp.zeros_like(ang)
    cos_m = jnp.concatenate([jnp.cos(ang), jnp.cos(ang), zeros], axis=1)
    sin_up = jnp.concatenate([half0, jnp.sin(ang), zeros], axis=1)
    sin_dn = jnp.concatenate([-jnp.sin(ang), half0, zeros], axis=1)
    ang = angles(DIFF_HEAD_DIM)
    cos_d = jnp.concatenate([jnp.cos(ang), jnp.cos(ang)], axis=1)
    sin_d = jnp.concatenate([-jnp.sin(ang), jnp.sin(ang)], axis=1)
    return cos_m, sin_up, sin_dn, cos_d, sin_d


def kernel(x, ffn1_pre_g, ffn1_w_gate, ffn1_w_up, ffn1_w_down, ffn1_post_g, mix_pre_g, w_in, mla_q_norm_g, mla_w_uq, mla_kv_norm_g, mla_w_ukv, diff_lambda_q1, diff_lambda_k1, diff_lambda_q2, diff_lambda_k2, diff_subln_g, w_out, mix_post_g, ffn2_pre_g, ffn2_w_gate, ffn2_w_up, ffn2_w_down, ffn2_post_g):
    batch, seq, d = x.shape
    depth = ffn1_pre_g.shape[0]
    assert depth == 1 and d == D_MODEL
    assert seq % ATT_TQ == 0 and seq % PROJ_TM == 0 and (batch * seq) % FFN_TM == 0
    tables = _rope_tables(seq)
    xt = x.reshape(batch * seq, d)
    for l in range(depth):
        xt = _ffn(xt, ffn1_pre_g[l][None], ffn1_w_gate[l].astype(BF16), ffn1_w_up[l].astype(BF16),
                  ffn1_w_down[l].astype(BF16), ffn1_post_g[l][None])
        q, k, v, dq, dk, dv = _mix_proj(
            xt, mix_pre_g[l][None], _pack_w_in(w_in[l]), mla_q_norm_g[l][None],
            _pack_w_uq(mla_w_uq[l]), mla_kv_norm_g[l][None], mla_w_ukv[l].astype(BF16),
            tables, seq)
        o_mla = _mla_attn(q, k, v, batch, seq)
        o_diff = _diff_attn(dq, dk, dv, diff_lambda_q1[l][None], diff_lambda_k1[l][None],
                            diff_lambda_q2[l][None], diff_lambda_k2[l][None],
                            diff_subln_g[l][None], batch, seq)
        wo = w_out[l].astype(BF16)
        xt = _out_proj(xt, o_mla, o_diff, wo[:MLA_OUT], wo[MLA_OUT:], mix_post_g[l][None])
        xt = _ffn(xt, ffn2_pre_g[l][None], ffn2_w_gate[l].astype(BF16), ffn2_w_up[l].astype(BF16),
                  ffn2_w_down[l].astype(BF16), ffn2_post_g[l][None])
    return xt.reshape(batch, seq, d)
```

```python
import functools
import math

import jax
import jax.numpy as jnp
import numpy as np
from jax import lax
from jax.experimental import pallas as pl
from jax.experimental.pallas import tpu as pltpu

D_MODEL = 2048
CHUNK = 64
ROPE_THETA = 10000.0
NORM_EPS = 1e-6
MLA_HEADS = 8
MLA_Q_RANK = 512
MLA_KV_RANK = 512
MLA_NOPE_DIM = 128
MLA_ROPE_DIM = 64
MLA_V_DIM = 128
MLA_QK_DIM = MLA_NOPE_DIM + MLA_ROPE_DIM
DIFF_HEADS = 4
DIFF_HEAD_DIM = 128
DIFF_V_DIM = 2 * DIFF_HEAD_DIM
MLA_OUT = MLA_HEADS * MLA_V_DIM
DIFF_OUT = DIFF_HEADS * DIFF_V_DIM
DIFF_QK_W = DIFF_HEADS * 2 * DIFF_HEAD_DIM
D_FF = 5632
LAMBDA_INIT = 0.8 - 0.6 * math.exp(-0.3 * 0)
LOG2_E = math.log2(math.e)

V7X_LANES = 128
V7X_VMEM_BYTES = 64 * 1024 * 1024
MLA_HEAD_PAD = 2 * V7X_LANES

FFN_TM = 1024
FFN_TF = 512
FFN_ROWS = 128
PROJ_TM = 256
ATT_TQ = 512
ATT_TK = 512
MLA_GROUP = 4
DIFF_GROUP = 2
OUT_TM = 512

BF16 = jnp.bfloat16
F32 = jnp.float32


def _vmem_limit(nbytes):
    return int(min(V7X_VMEM_BYTES - 8 * 1024 * 1024, max(32 * 1024 * 1024, nbytes * 3 // 2)))


def _dot_bf16(a, w):
    return lax.dot_general(a, w, (((1,), (0,)), ((), ())), preferred_element_type=F32)


def _rms(v, g):
    ms = jnp.mean(v * v, axis=-1, keepdims=True)
    return v * lax.rsqrt(ms + NORM_EPS) * g


def _ffn_kernel(x_ref, pre_g_ref, wg_ref, wu_ref, wd_ref, post_g_ref, o_ref, h_ref):
    f = pl.program_id(1)
    n_chunks = FFN_TM // FFN_ROWS

    @pl.when(f == 0)
    def _():
        def body(r, c):
            rows = pl.ds(pl.multiple_of(r * FFN_ROWS, FFN_ROWS), FFN_ROWS)
            h_ref[rows, :] = _rms(x_ref[rows, :], pre_g_ref[...]).astype(BF16)
            return c
        lax.fori_loop(0, n_chunks, body, 0)
        o_ref[...] = jnp.zeros_like(o_ref)

    h = h_ref[...]
    g = _dot_bf16(h, wg_ref[...])
    u = _dot_bf16(h, wu_ref[...])
    a = (g * (1.0 / (1.0 + jnp.exp(-g))) * u).astype(BF16)
    o_ref[...] += _dot_bf16(a, wd_ref[...])

    @pl.when(f == pl.num_programs(1) - 1)
    def _():
        def body(r, c):
            rows = pl.ds(pl.multiple_of(r * FFN_ROWS, FFN_ROWS), FFN_ROWS)
            o = o_ref[rows, :]
            inv = lax.rsqrt(jnp.mean(o * o, axis=-1, keepdims=True) + NORM_EPS)
            o_ref[rows, :] = x_ref[rows, :] + 0.5 * (o_ref[rows, :] * inv * post_g_ref[...])
            return c
        lax.fori_loop(0, n_chunks, body, 0)


def _ffn(x, pre_g, wg, wu, wd, post_g):
    t, d = x.shape
    nbytes = (2 * FFN_TM * d * 4 + FFN_TM * d * 2 + 2 * 3 * d * FFN_TF * wg.dtype.itemsize
              + 3 * FFN_TM * FFN_TF * 4)
    return pl.pallas_call(
        _ffn_kernel,
        grid=(t // FFN_TM, D_FF // FFN_TF),
        in_specs=[
            pl.BlockSpec((FFN_TM, d), lambda i, f: (i, 0), pipeline_mode=pl.Buffered(1)),
            pl.BlockSpec((1, d), lambda i, f: (0, 0)),
            pl.BlockSpec((d, FFN_TF), lambda i, f: (0, f)),
            pl.BlockSpec((d, FFN_TF), lambda i, f: (0, f)),
            pl.BlockSpec((FFN_TF, d), lambda i, f: (f, 0)),
            pl.BlockSpec((1, d), lambda i, f: (0, 0)),
        ],
        out_specs=pl.BlockSpec((FFN_TM, d), lambda i, f: (i, 0), pipeline_mode=pl.Buffered(1)),
        out_shape=jax.ShapeDtypeStruct((t, d), F32),
        scratch_shapes=[pltpu.VMEM((FFN_TM, d), BF16)],
        compiler_params=pltpu.CompilerParams(
            dimension_semantics=("parallel", "arbitrary"),
            vmem_limit_bytes=_vmem_limit(nbytes)),
        name="ffn",
    )(x, pre_g, wg, wu, wd, post_g)


_IN_A = MLA_Q_RANK + MLA_KV_RANK + V7X_LANES
_IN_DQ = MLA_Q_RANK + MLA_KV_RANK + MLA_ROPE_DIM
_IN_DK = _IN_DQ + DIFF_QK_W
_IN_DV = _IN_DK + DIFF_QK_W
_IN_W = _IN_DV + DIFF_OUT


def _rope_half(v, cos, sin_signed):
    return v * cos + pltpu.roll(v, DIFF_HEAD_DIM // 2, 1) * sin_signed


def _rope_mla(v, cos, sin_up, sin_dn):
    half = MLA_ROPE_DIM // 2
    return (v * cos + pltpu.roll(v, half, 1) * sin_up
            + pltpu.roll(v, V7X_LANES - half, 1) * sin_dn)


def _proj_kernel(x_ref, g_ref, wa_ref, wdq_ref, wdk_ref, wdv_ref, qg_ref, wuq_ref, kvg_ref, wukv_ref,
                 cm_ref, sup_ref, sdn_ref, cd_ref, sd_ref,
                 q_ref, k_ref, v_ref, dq_ref, dk_ref, dv_ref):
    h = _rms(x_ref[...], g_ref[...]).astype(BF16)

    pa = _dot_bf16(h, wa_ref[...])
    cq = _rms(pa[:, 0:MLA_Q_RANK], qg_ref[...]).astype(BF16)
    ckv = _rms(pa[:, MLA_Q_RANK:MLA_Q_RANK + MLA_KV_RANK], kvg_ref[...]).astype(BF16)
    cm, sup, sdn = cm_ref[...], sup_ref[...], sdn_ref[...]
    kpe = _rope_mla(pa[:, MLA_Q_RANK + MLA_KV_RANK:_IN_A], cm, sup, sdn).astype(BF16)

    q = _dot_bf16(cq, wuq_ref[...])
    kv = _dot_bf16(ckv, wukv_ref[...])
    for hh in range(MLA_HEADS):
        a = hh * MLA_HEAD_PAD
        b = a + V7X_LANES
        c = b + V7X_LANES
        q_ref[:, a:b] = q[:, a:b].astype(BF16)
        q_ref[:, b:c] = _rope_mla(q[:, b:c], cm, sup, sdn).astype(BF16)
        k_ref[:, a:b] = kv[:, a:b].astype(BF16)
        k_ref[:, b:c] = kpe
        v_ref[:, hh * MLA_V_DIM:(hh + 1) * MLA_V_DIM] = kv[:, b:c].astype(BF16)

    cd, sd = cd_ref[...], sd_ref[...]
    pdq = _dot_bf16(h, wdq_ref[...])
    for c in range(DIFF_QK_W // DIFF_HEAD_DIM):
        cols = slice(c * DIFF_HEAD_DIM, (c + 1) * DIFF_HEAD_DIM)
        dq_ref[:, cols] = _rope_half(pdq[:, cols], cd, sd).astype(BF16)
    pdk = _dot_bf16(h, wdk_ref[...])
    for c in range(DIFF_QK_W // DIFF_HEAD_DIM):
        cols = slice(c * DIFF_HEAD_DIM, (c + 1) * DIFF_HEAD_DIM)
        dk_ref[:, cols] = _rope_half(pdk[:, cols], cd, sd).astype(BF16)
    dv_ref[...] = _dot_bf16(h, wdv_ref[...]).astype(BF16)


def _mix_proj(x, g, w_in, qg, wuq_p, kvg, wukv, tables, seq):
    t, d = x.shape
    tm = PROJ_TM
    pos_blocks = seq // tm
    const = lambda i: (0, 0)
    row = lambda i: (i, 0)
    pos = lambda i: (i % pos_blocks, 0)
    wq = MLA_HEADS * MLA_HEAD_PAD
    out_w = (wq, wq, MLA_OUT, DIFF_QK_W, DIFF_QK_W, DIFF_OUT)
    w_a = w_in[:, :_IN_A].astype(BF16)
    w_dq = w_in[:, _IN_DQ:_IN_DK].astype(BF16)
    w_dk = w_in[:, _IN_DK:_IN_DV].astype(BF16)
    w_dv = w_in[:, _IN_DV:_IN_W].astype(BF16)
    once = dict(pipeline_mode=pl.Buffered(1))
    nbytes = (d * (_IN_A + 3 * DIFF_QK_W) * 2 + MLA_Q_RANK * wq * (2 + 4) + 2 * tm * d * 4
              + 2 * tm * sum(out_w) * 2 + tm * _IN_W * 4 + 2 * tm * wq * 4)
    return pl.pallas_call(
        _proj_kernel,
        grid=(t // tm,),
        in_specs=[
            pl.BlockSpec((tm, d), row),
            pl.BlockSpec((1, d), const),
            pl.BlockSpec((d, _IN_A), const, **once),
            pl.BlockSpec((d, DIFF_QK_W), const, **once),
            pl.BlockSpec((d, DIFF_QK_W), const, **once),
            pl.BlockSpec((d, DIFF_OUT), const, **once),
            pl.BlockSpec((1, MLA_Q_RANK), const),
            pl.BlockSpec((MLA_Q_RANK, wq), const, **once),
            pl.BlockSpec((1, MLA_KV_RANK), const),
            pl.BlockSpec((MLA_KV_RANK, wq), const, **once),
        ] + [pl.BlockSpec((tm, V7X_LANES), pos)] * 5,
        out_specs=[pl.BlockSpec((tm, w), row) for w in out_w],
        out_shape=[jax.ShapeDtypeStruct((t, w), BF16) for w in out_w],
        compiler_params=pltpu.CompilerParams(
            dimension_semantics=("parallel",),
            vmem_limit_bytes=_vmem_limit(nbytes)),
        name="mix_proj",
    )(x, g, w_a, w_dq, w_dk, w_dv, qg, wuq_p, kvg, wukv, *tables)


def _diag_mask():
    r = lax.broadcasted_iota(jnp.int32, (ATT_TQ, ATT_TK), 0) // CHUNK
    c = lax.broadcasted_iota(jnp.int32, (ATT_TQ, ATT_TK), 1) // CHUNK
    return r >= c


def _scores(q, k, scale_log2):
    t = lax.dot_general(q, k, (((1,), (1,)), ((), ())), preferred_element_type=F32)
    return t * scale_log2


def _flash_first(t, v):
    m = jnp.max(t, axis=-1, keepdims=True)
    p = jnp.exp2(t - m)
    l = jnp.sum(p, axis=-1, keepdims=True)
    acc = jnp.dot(p.astype(BF16), v, preferred_element_type=F32)
    return m, l, acc


def _flash_step(t, v, m, l, acc):
    m_new = jnp.maximum(m, jnp.max(t, axis=-1, keepdims=True))
    alpha = jnp.exp2(m - m_new)
    p = jnp.exp2(t - m_new)
    l = alpha * l + jnp.sum(p, axis=-1, keepdims=True)
    acc = alpha * acc + jnp.dot(p.astype(BF16), v, preferred_element_type=F32)
    return m_new, l, acc


def _mla_attn_kernel(q_ref, k_ref, v_ref, o_ref):
    qi = pl.program_id(2)
    c = MLA_QK_DIM ** -0.5 * LOG2_E
    wq, wv = MLA_HEAD_PAD, MLA_V_DIM
    heads = range(MLA_GROUP)
    diag = pl.ds(pl.multiple_of(qi * ATT_TK, ATT_TK), ATT_TK)
    mask = _diag_mask()
    state = ()
    for g in heads:
        t = _scores(q_ref[:, g * wq:(g + 1) * wq], k_ref[diag, g * wq:(g + 1) * wq], c)
        state += _flash_first(jnp.where(mask, t, -jnp.inf), v_ref[diag, g * wv:(g + 1) * wv])

    def body(j, st):
        rows = pl.ds(pl.multiple_of(j * ATT_TK, ATT_TK), ATT_TK)
        out = ()
        for g in heads:
            t = _scores(q_ref[:, g * wq:(g + 1) * wq], k_ref[rows, g * wq:(g + 1) * wq], c)
            out += _flash_step(t, v_ref[rows, g * wv:(g + 1) * wv], *st[3 * g:3 * g + 3])
        return out

    state = lax.fori_loop(0, qi, body, state)
    for g in heads:
        m, l, acc = state[3 * g:3 * g + 3]
        o_ref[:, g * wv:(g + 1) * wv] = (acc / l).astype(o_ref.dtype)


def _mla_attn(q, k, v, batch, seq):
    nq = seq // ATT_TQ
    wq, wv = MLA_GROUP * MLA_HEAD_PAD, MLA_GROUP * MLA_V_DIM
    return pl.pallas_call(
        _mla_attn_kernel,
        grid=(batch, MLA_HEADS // MLA_GROUP, nq),
        in_specs=[
            pl.BlockSpec((ATT_TQ, wq), lambda b, h, i: (b * nq + i, h)),
            pl.BlockSpec((seq, wq), lambda b, h, i: (b, h)),
            pl.BlockSpec((seq, wv), lambda b, h, i: (b, h)),
        ],
        out_specs=pl.BlockSpec((ATT_TQ, wv), lambda b, h, i: (b * nq + i, h)),
        out_shape=jax.ShapeDtypeStruct((batch * seq, MLA_OUT), BF16),
        compiler_params=pltpu.CompilerParams(
            dimension_semantics=("parallel", "parallel", "parallel"),
            vmem_limit_bytes=_vmem_limit(2 * (seq + ATT_TQ) * (wq + wv) * 2
                                         + 8 * MLA_GROUP * ATT_TQ * ATT_TK * 4)),
        name="mla_attn",
    )(q, k, v)


def _diff_attn_kernel(q_ref, k_ref, v_ref, lq1_ref, lk1_ref, lq2_ref, lk2_ref, g_ref, o_ref):
    qi = pl.program_id(2)
    c = DIFF_HEAD_DIM ** -0.5 * LOG2_E
    d, wv = DIFF_HEAD_DIM, DIFF_V_DIM
    maps = range(2 * DIFF_GROUP)
    diag = pl.ds(pl.multiple_of(qi * ATT_TK, ATT_TK), ATT_TK)
    mask = _diag_mask()
    state = ()
    for n in maps:
        t = _scores(q_ref[:, n * d:(n + 1) * d], k_ref[diag, n * d:(n + 1) * d], c)
        state += _flash_first(jnp.where(mask, t, -jnp.inf),
                              v_ref[diag, (n // 2) * wv:(n // 2 + 1) * wv])

    def body(j, st):
        rows = pl.ds(pl.multiple_of(j * ATT_TK, ATT_TK), ATT_TK)
        out = ()
        for n in maps:
            t = _scores(q_ref[:, n * d:(n + 1) * d], k_ref[rows, n * d:(n + 1) * d], c)
            out += _flash_step(t, v_ref[rows, (n // 2) * wv:(n // 2 + 1) * wv],
                               *st[3 * n:3 * n + 3])
        return out

    state = lax.fori_loop(0, qi, body, state)
    lam = (jnp.exp(jnp.sum(lq1_ref[...] * lk1_ref[...], axis=-1, keepdims=True))
           - jnp.exp(jnp.sum(lq2_ref[...] * lk2_ref[...], axis=-1, keepdims=True))
           + LAMBDA_INIT)
    for h in range(DIFF_GROUP):
        _, l1, a1, _, l2, a2 = state[6 * h:6 * h + 6]
        o = a1 / l1 - lam * (a2 / l2)
        o_ref[:, h * wv:(h + 1) * wv] = (
            _rms(o, g_ref[...]) * (1.0 - LAMBDA_INIT)).astype(o_ref.dtype)


def _diff_attn(q, k, v, lq1, lk1, lq2, lk2, g, batch, seq):
    nq = seq // ATT_TQ
    w = DIFF_GROUP * DIFF_V_DIM
    vec = pl.BlockSpec((1, DIFF_HEAD_DIM), lambda b, h, i: (0, 0))
    return pl.pallas_call(
        _diff_attn_kernel,
        grid=(batch, DIFF_HEADS // DIFF_GROUP, nq),
        in_specs=[
            pl.BlockSpec((ATT_TQ, w), lambda b, h, i: (b * nq + i, h)),
            pl.BlockSpec((seq, w), lambda b, h, i: (b, h)),
            pl.BlockSpec((seq, w), lambda b, h, i: (b, h)),
            vec, vec, vec, vec,
            pl.BlockSpec((1, DIFF_V_DIM), lambda b, h, i: (0, 0)),
        ],
        out_specs=pl.BlockSpec((ATT_TQ, w), lambda b, h, i: (b * nq + i, h)),
        out_shape=jax.ShapeDtypeStruct((batch * seq, DIFF_OUT), BF16),
        compiler_params=pltpu.CompilerParams(
            dimension_semantics=("parallel", "parallel", "parallel"),
            vmem_limit_bytes=_vmem_limit(2 * (2 * seq + 2 * ATT_TQ) * w * 2
                                         + 16 * DIFF_GROUP * ATT_TQ * ATT_TK * 4)),
        name="diff_attn",
    )(q, k, v, lq1, lk1, lq2, lk2, g)


def _out_kernel(x_ref, oa_ref, ob_ref, wa_ref, wb_ref, g_ref, o_ref):
    o = _dot_bf16(oa_ref[...], wa_ref[...]) + _dot_bf16(ob_ref[...], wb_ref[...])
    o_ref[...] = x_ref[...] + _rms(o, g_ref[...])


def _out_proj(x, oa, ob, wa, wb, g):
    t, d = x.shape
    tm = OUT_TM
    const = lambda i: (0, 0)
    row = lambda i: (i, 0)
    once = dict(pipeline_mode=pl.Buffered(1))
    nbytes = (d * d * (wa.dtype.itemsize + 2) + 4 * tm * d * 4 + 2 * tm * d * 2 + 2 * tm * d * 4)
    return pl.pallas_call(
        _out_kernel,
        grid=(t // tm,),
        in_specs=[
            pl.BlockSpec((tm, d), row),
            pl.BlockSpec((tm, MLA_OUT), row),
            pl.BlockSpec((tm, DIFF_OUT), row),
            pl.BlockSpec((MLA_OUT, d), const, **once),
            pl.BlockSpec((DIFF_OUT, d), const, **once),
            pl.BlockSpec((1, d), const),
        ],
        out_specs=pl.BlockSpec((tm, d), row),
        out_shape=jax.ShapeDtypeStruct((t, d), F32),
        compiler_params=pltpu.CompilerParams(
            dimension_semantics=("parallel",),
            vmem_limit_bytes=_vmem_limit(nbytes)),
        name="out_proj",
    )(x, oa, ob, wa, wb, g)


def _pack_w_uq(w_uq):
    r = w_uq.shape[0]
    w = w_uq.reshape(r, MLA_HEADS, MLA_QK_DIM)
    pad = jnp.zeros((r, MLA_HEADS, MLA_HEAD_PAD - MLA_QK_DIM), w_uq.dtype)
    return jnp.concatenate([w, pad], axis=-1).reshape(r, MLA_HEADS * MLA_HEAD_PAD).astype(BF16)


def _rope_tables(seq):
    pos = jnp.arange(seq, dtype=F32)[:, None]

    def angles(d):
        inv_freq = ROPE_THETA ** (-jnp.arange(0, d, 2, dtype=F32) / d)
        return pos * inv_freq[None, :]

    ang = angles(MLA_ROPE_DIM)
    zeros = jnp.zeros((seq, V7X_LANES - MLA_ROPE_DIM), F32)
    half0 = jnp.zeros_like(ang)
    cos_m = jnp.concatenate([jnp.cos(ang), jnp.cos(ang), zeros], axis=1)
    sin_up = jnp.concatenate([half0, jnp.sin(ang), zeros], axis=1)
    sin_dn = jnp.concatenate([-jnp.sin(ang), half0, zeros], axis=1)
    ang = angles(DIFF_HEAD_DIM)
    cos_d = jnp.concatenate([jnp.cos(ang), jnp.cos(ang)], axis=1)
    sin_d = jnp.concatenate([-jnp.sin(ang), jnp.sin(ang)], axis=1)
    return cos_m, sin_up, sin_dn, cos_d, sin_d


def kernel(x, ffn1_pre_g, ffn1_w_gate, ffn1_w_up, ffn1_w_down, ffn1_post_g, mix_pre_g, w_in, mla_q_norm_g, mla_w_uq, mla_kv_norm_g, mla_w_ukv, diff_lambda_q1, diff_lambda_k1, diff_lambda_q2, diff_lambda_k2, diff_subln_g, w_out, mix_post_g, ffn2_pre_g, ffn2_w_gate, ffn2_w_up, ffn2_w_down, ffn2_post_g):
    batch, seq, d = x.shape
    depth = ffn1_pre_g.shape[0]
    assert depth == 1 and d == D_MODEL
    assert seq % ATT_TQ == 0 and seq % PROJ_TM == 0 and (batch * seq) % FFN_TM == 0
    tables = _rope_tables(seq)
    xt = x.reshape(batch * seq, d)
    for l in range(depth):
        xt = _ffn(xt, ffn1_pre_g[l][None], ffn1_w_gate[l], ffn1_w_up[l], ffn1_w_down[l],
                  ffn1_post_g[l][None])
        q, k, v, dq, dk, dv = _mix_proj(
            xt, mix_pre_g[l][None], w_in[l], mla_q_norm_g[l][None],
            _pack_w_uq(mla_w_uq[l]), mla_kv_norm_g[l][None], mla_w_ukv[l],
            tables, seq)
        o_mla = _mla_attn(q, k, v, batch, seq)
        o_diff = _diff_attn(dq, dk, dv, diff_lambda_q1[l][None], diff_lambda_k1[l][None],
                            diff_lambda_q2[l][None], diff_lambda_k2[l][None],
                            diff_subln_g[l][None], batch, seq)
        wo = w_out[l]
        xt = _out_proj(xt, o_mla, o_diff, wo[:MLA_OUT], wo[MLA_OUT:], mix_post_g[l][None])
        xt = _ffn(xt, ffn2_pre_g[l][None], ffn2_w_gate[l], ffn2_w_up[l], ffn2_w_down[l],
                  ffn2_post_g[l][None])
    return xt.reshape(batch, seq, d)
```

```python
import functools
import math

import jax
import jax.numpy as jnp
import numpy as np
from jax import lax
from jax.experimental import pallas as pl
from jax.experimental.pallas import tpu as pltpu

D_MODEL = 2048
CHUNK = 64
ROPE_THETA = 10000.0
NORM_EPS = 1e-6
MLA_HEADS = 8
MLA_Q_RANK = 512
MLA_KV_RANK = 512
MLA_NOPE_DIM = 128
MLA_ROPE_DIM = 64
MLA_V_DIM = 128
MLA_QK_DIM = MLA_NOPE_DIM + MLA_ROPE_DIM
DIFF_HEADS = 4
DIFF_HEAD_DIM = 128
DIFF_V_DIM = 2 * DIFF_HEAD_DIM
MLA_OUT = MLA_HEADS * MLA_V_DIM
DIFF_OUT = DIFF_HEADS * DIFF_V_DIM
DIFF_QK_W = DIFF_HEADS * 2 * DIFF_HEAD_DIM
D_FF = 5632
LAMBDA_INIT = 0.8 - 0.6 * math.exp(-0.3 * 0)
LOG2_E = math.log2(math.e)

V7X_LANES = 128
V7X_VMEM_BYTES = 64 * 1024 * 1024
MLA_HEAD_PAD = 2 * V7X_LANES

FFN_TM = 1024
FFN_TF = 512
FFN_ROWS = 256
PROJ_TM = 256
ATT_TQ = 512
ATT_TK = 512
MLA_GROUP = 4
DIFF_GROUP = 2
OUT_TM = 512

BF16 = jnp.bfloat16
F32 = jnp.float32


def _vmem_limit(nbytes):
    return int(min(V7X_VMEM_BYTES - 8 * 1024 * 1024, max(32 * 1024 * 1024, nbytes * 3 // 2)))


def _dot_bf16(a, w):
    return lax.dot_general(a, w, (((1,), (0,)), ((), ())), preferred_element_type=F32)


def _rms(v, g):
    ms = jnp.mean(v * v, axis=-1, keepdims=True)
    return v * lax.rsqrt(ms + NORM_EPS) * g


def _swiglu_hidden(h, wg, wu):
    g = _dot_bf16(h, wg)
    u = _dot_bf16(h, wu)
    return (g * (1.0 / (1.0 + jnp.exp(-g))) * u).astype(BF16)


def _ffn_kernel(x_ref, pre_g_ref, wg_ref, wu_ref, wd_ref, post_g_ref, o_ref, h_ref):
    f = pl.program_id(1)
    last = pl.num_programs(1) - 1
    chunks = [pl.ds(c * FFN_ROWS, FFN_ROWS) for c in range(FFN_TM // FFN_ROWS)]

    @pl.when(f == 0)
    def _():
        wg, wu, wd = wg_ref[...].astype(BF16), wu_ref[...].astype(BF16), wd_ref[...].astype(BF16)
        for rows in chunks:
            h = _rms(x_ref[rows, :], pre_g_ref[...]).astype(BF16)
            h_ref[rows, :] = h
            o_ref[rows, :] = _dot_bf16(_swiglu_hidden(h, wg, wu), wd)

    @pl.when(jnp.logical_and(f > 0, f < last))
    def _():
        a = _swiglu_hidden(h_ref[...], wg_ref[...], wu_ref[...])
        o_ref[...] += _dot_bf16(a, wd_ref[...])

    @pl.when(f == last)
    def _():
        wg, wu, wd = wg_ref[...].astype(BF16), wu_ref[...].astype(BF16), wd_ref[...].astype(BF16)
        for rows in chunks:
            acc = o_ref[rows, :] + _dot_bf16(_swiglu_hidden(h_ref[rows, :], wg, wu), wd)
            o_ref[rows, :] = x_ref[rows, :] + 0.5 * _rms(acc, post_g_ref[...])


def _ffn(x, pre_g, wg, wu, wd, post_g):
    t, d = x.shape
    assert D_FF // FFN_TF >= 2, "first and last hidden steps must be distinct"
    nbytes = (2 * FFN_TM * d * 4 + FFN_TM * d * 2 + 2 * 3 * d * FFN_TF * wg.dtype.itemsize
              + 3 * FFN_TM * FFN_TF * 4)
    return pl.pallas_call(
        _ffn_kernel,
        grid=(t // FFN_TM, D_FF // FFN_TF),
        in_specs=[
            pl.BlockSpec((FFN_TM, d), lambda i, f: (i, 0), pipeline_mode=pl.Buffered(1)),
            pl.BlockSpec((1, d), lambda i, f: (0, 0)),
            pl.BlockSpec((d, FFN_TF), lambda i, f: (0, f)),
            pl.BlockSpec((d, FFN_TF), lambda i, f: (0, f)),
            pl.BlockSpec((FFN_TF, d), lambda i, f: (f, 0)),
            pl.BlockSpec((1, d), lambda i, f: (0, 0)),
        ],
        out_specs=pl.BlockSpec((FFN_TM, d), lambda i, f: (i, 0), pipeline_mode=pl.Buffered(1)),
        out_shape=jax.ShapeDtypeStruct((t, d), F32),
        scratch_shapes=[pltpu.VMEM((FFN_TM, d), BF16)],
        compiler_params=pltpu.CompilerParams(
            dimension_semantics=("parallel", "arbitrary"),
            vmem_limit_bytes=_vmem_limit(nbytes)),
        name="ffn",
    )(x, pre_g, wg, wu, wd, post_g)


_IN_A = MLA_Q_RANK + MLA_KV_RANK + V7X_LANES
_IN_DQ = MLA_Q_RANK + MLA_KV_RANK + MLA_ROPE_DIM
_IN_DK = _IN_DQ + DIFF_QK_W
_IN_DV = _IN_DK + DIFF_QK_W
_IN_W = _IN_DV + DIFF_OUT


def _rope_half(v, cos, sin_signed):
    return v * cos + pltpu.roll(v, DIFF_HEAD_DIM // 2, 1) * sin_signed


def _rope_mla(v, cos, sin_up, sin_dn):
    half = MLA_ROPE_DIM // 2
    return (v * cos + pltpu.roll(v, half, 1) * sin_up
            + pltpu.roll(v, V7X_LANES - half, 1) * sin_dn)


def _proj_kernel(x_ref, g_ref, wa_ref, wdq_ref, wdk_ref, wdv_ref, qg_ref, wuq_ref, kvg_ref, wukv_ref,
                 cm_ref, sup_ref, sdn_ref, cd_ref, sd_ref,
                 q_ref, k_ref, v_ref, dq_ref, dk_ref, dv_ref):
    h = _rms(x_ref[...], g_ref[...]).astype(BF16)

    pa = _dot_bf16(h, wa_ref[...])
    cq = _rms(pa[:, 0:MLA_Q_RANK], qg_ref[...]).astype(BF16)
    ckv = _rms(pa[:, MLA_Q_RANK:MLA_Q_RANK + MLA_KV_RANK], kvg_ref[...]).astype(BF16)
    cm, sup, sdn = cm_ref[...], sup_ref[...], sdn_ref[...]
    kpe = _rope_mla(pa[:, MLA_Q_RANK + MLA_KV_RANK:_IN_A], cm, sup, sdn).astype(BF16)

    q = _dot_bf16(cq, wuq_ref[...])
    kv = _dot_bf16(ckv, wukv_ref[...])
    for hh in range(MLA_HEADS):
        a = hh * MLA_HEAD_PAD
        b = a + V7X_LANES
        c = b + V7X_LANES
        q_ref[:, a:b] = q[:, a:b].astype(BF16)
        q_ref[:, b:c] = _rope_mla(q[:, b:c], cm, sup, sdn).astype(BF16)
        k_ref[:, a:b] = kv[:, a:b].astype(BF16)
        k_ref[:, b:c] = kpe
        v_ref[:, hh * MLA_V_DIM:(hh + 1) * MLA_V_DIM] = kv[:, b:c].astype(BF16)

    cd, sd = cd_ref[...], sd_ref[...]
    pdq = _dot_bf16(h, wdq_ref[...])
    for c in range(DIFF_QK_W // DIFF_HEAD_DIM):
        cols = slice(c * DIFF_HEAD_DIM, (c + 1) * DIFF_HEAD_DIM)
        dq_ref[:, cols] = _rope_half(pdq[:, cols], cd, sd).astype(BF16)
    pdk = _dot_bf16(h, wdk_ref[...])
    for c in range(DIFF_QK_W // DIFF_HEAD_DIM):
        cols = slice(c * DIFF_HEAD_DIM, (c + 1) * DIFF_HEAD_DIM)
        dk_ref[:, cols] = _rope_half(pdk[:, cols], cd, sd).astype(BF16)
    dv_ref[...] = _dot_bf16(h, wdv_ref[...]).astype(BF16)


def _mix_proj(x, g, w_in, qg, wuq_p, kvg, wukv, tables, seq):
    t, d = x.shape
    tm = PROJ_TM
    pos_blocks = seq // tm
    const = lambda i: (0, 0)
    row = lambda i: (i, 0)
    pos = lambda i: (i % pos_blocks, 0)
    wq = MLA_HEADS * MLA_HEAD_PAD
    out_w = (wq, wq, MLA_OUT, DIFF_QK_W, DIFF_QK_W, DIFF_OUT)
    w_a = w_in[:, :_IN_A].astype(BF16)
    w_dq = w_in[:, _IN_DQ:_IN_DK].astype(BF16)
    w_dk = w_in[:, _IN_DK:_IN_DV].astype(BF16)
    w_dv = w_in[:, _IN_DV:_IN_W].astype(BF16)
    once = dict(pipeline_mode=pl.Buffered(1))
    nbytes = (d * (_IN_A + 3 * DIFF_QK_W) * 2 + MLA_Q_RANK * wq * (2 + 4) + 2 * tm * d * 4
              + 2 * tm * sum(out_w) * 2 + tm * _IN_W * 4 + 2 * tm * wq * 4)
    return pl.pallas_call(
        _proj_kernel,
        grid=(t // tm,),
        in_specs=[
            pl.BlockSpec((tm, d), row),
            pl.BlockSpec((1, d), const),
            pl.BlockSpec((d, _IN_A), const, **once),
            pl.BlockSpec((d, DIFF_QK_W), const, **once),
            pl.BlockSpec((d, DIFF_QK_W), const, **once),
            pl.BlockSpec((d, DIFF_OUT), const, **once),
            pl.BlockSpec((1, MLA_Q_RANK), const),
            pl.BlockSpec((MLA_Q_RANK, wq), const, **once),
            pl.BlockSpec((1, MLA_KV_RANK), const),
            pl.BlockSpec((MLA_KV_RANK, wq), const, **once),
        ] + [pl.BlockSpec((tm, V7X_LANES), pos)] * 5,
        out_specs=[pl.BlockSpec((tm, w), row) for w in out_w],
        out_shape=[jax.ShapeDtypeStruct((t, w), BF16) for w in out_w],
        compiler_params=pltpu.CompilerParams(
            dimension_semantics=("parallel",),
            vmem_limit_bytes=_vmem_limit(nbytes)),
        name="mix_proj",
    )(x, g, w_a, w_dq, w_dk, w_dv, qg, wuq_p, kvg, wukv, *tables)


def _diag_mask():
    r = lax.broadcasted_iota(jnp.int32, (ATT_TQ, ATT_TK), 0) // CHUNK
    c = lax.broadcasted_iota(jnp.int32, (ATT_TQ, ATT_TK), 1) // CHUNK
    return r >= c


def _scores(q, k, scale_log2):
    t = lax.dot_general(q, k, (((1,), (1,)), ((), ())), preferred_element_type=F32)
    return t * scale_log2


def _flash_first(t, v):
    m = jnp.max(t, axis=-1, keepdims=True)
    p = jnp.exp2(t - m)
    l = jnp.sum(p, axis=-1, keepdims=True)
    acc = jnp.dot(p.astype(BF16), v, preferred_element_type=F32)
    return m, l, acc


def _flash_step(t, v, m, l, acc):
    m_new = jnp.maximum(m, jnp.max(t, axis=-1, keepdims=True))
    alpha = jnp.exp2(m - m_new)
    p = jnp.exp2(t - m_new)
    l = alpha * l + jnp.sum(p, axis=-1, keepdims=True)
    acc = alpha * acc + jnp.dot(p.astype(BF16), v, preferred_element_type=F32)
    return m_new, l, acc


def _mla_attn_kernel(q_ref, k_ref, v_ref, o_ref):
    qi = pl.program_id(2)
    c = MLA_QK_DIM ** -0.5 * LOG2_E
    wq, wv = MLA_HEAD_PAD, MLA_V_DIM
    heads = range(MLA_GROUP)
    diag = pl.ds(pl.multiple_of(qi * ATT_TK, ATT_TK), ATT_TK)
    mask = _diag_mask()
    state = ()
    for g in heads:
        t = _scores(q_ref[:, g * wq:(g + 1) * wq], k_ref[diag, g * wq:(g + 1) * wq], c)
        state += _flash_first(jnp.where(mask, t, -jnp.inf), v_ref[diag, g * wv:(g + 1) * wv])

    def body(j, st):
        rows = pl.ds(pl.multiple_of(j * ATT_TK, ATT_TK), ATT_TK)
        out = ()
        for g in heads:
            t = _scores(q_ref[:, g * wq:(g + 1) * wq], k_ref[rows, g * wq:(g + 1) * wq], c)
            out += _flash_step(t, v_ref[rows, g * wv:(g + 1) * wv], *st[3 * g:3 * g + 3])
        return out

    state = lax.fori_loop(0, qi, body, state)
    for g in heads:
        m, l, acc = state[3 * g:3 * g + 3]
        o_ref[:, g * wv:(g + 1) * wv] = (acc / l).astype(o_ref.dtype)


def _mla_attn(q, k, v, batch, seq):
    nq = seq // ATT_TQ
    wq, wv = MLA_GROUP * MLA_HEAD_PAD, MLA_GROUP * MLA_V_DIM
    return pl.pallas_call(
        _mla_attn_kernel,
        grid=(batch, MLA_HEADS // MLA_GROUP, nq),
        in_specs=[
            pl.BlockSpec((ATT_TQ, wq), lambda b, h, i: (b * nq + i, h)),
            pl.BlockSpec((seq, wq), lambda b, h, i: (b, h)),
            pl.BlockSpec((seq, wv), lambda b, h, i: (b, h)),
        ],
        out_specs=pl.BlockSpec((ATT_TQ, wv), lambda b, h, i: (b * nq + i, h)),
        out_shape=jax.ShapeDtypeStruct((batch * seq, MLA_OUT), BF16),
        compiler_params=pltpu.CompilerParams(
            dimension_semantics=("parallel", "parallel", "parallel"),
            vmem_limit_bytes=_vmem_limit(2 * (seq + ATT_TQ) * (wq + wv) * 2
                                         + 8 * MLA_GROUP * ATT_TQ * ATT_TK * 4)),
        name="mla_attn",
    )(q, k, v)


def _diff_attn_kernel(q_ref, k_ref, v_ref, lq1_ref, lk1_ref, lq2_ref, lk2_ref, g_ref, o_ref):
    qi = pl.program_id(2)
    c = DIFF_HEAD_DIM ** -0.5 * LOG2_E
    d, wv = DIFF_HEAD_DIM, DIFF_V_DIM
    maps = range(2 * DIFF_GROUP)
    diag = pl.ds(pl.multiple_of(qi * ATT_TK, ATT_TK), ATT_TK)
    mask = _diag_mask()
    state = ()
    for n in maps:
        t = _scores(q_ref[:, n * d:(n + 1) * d], k_ref[diag, n * d:(n + 1) * d], c)
        state += _flash_first(jnp.where(mask, t, -jnp.inf),
                              v_ref[diag, (n // 2) * wv:(n // 2 + 1) * wv])

    def body(j, st):
        rows = pl.ds(pl.multiple_of(j * ATT_TK, ATT_TK), ATT_TK)
        out = ()
        for n in maps:
            t = _scores(q_ref[:, n * d:(n + 1) * d], k_ref[rows, n * d:(n + 1) * d], c)
            out += _flash_step(t, v_ref[rows, (n // 2) * wv:(n // 2 + 1) * wv],
                               *st[3 * n:3 * n + 3])
        return out

    state = lax.fori_loop(0, qi, body, state)
    lam = (jnp.exp(jnp.sum(lq1_ref[...] * lk1_ref[...], axis=-1, keepdims=True))
           - jnp.exp(jnp.sum(lq2_ref[...] * lk2_ref[...], axis=-1, keepdims=True))
           + LAMBDA_INIT)
    for h in range(DIFF_GROUP):
        _, l1, a1, _, l2, a2 = state[6 * h:6 * h + 6]
        o = a1 / l1 - lam * (a2 / l2)
        o_ref[:, h * wv:(h + 1) * wv] = (
            _rms(o, g_ref[...]) * (1.0 - LAMBDA_INIT)).astype(o_ref.dtype)


def _diff_attn(q, k, v, lq1, lk1, lq2, lk2, g, batch, seq):
    nq = seq // ATT_TQ
    w = DIFF_GROUP * DIFF_V_DIM
    vec = pl.BlockSpec((1, DIFF_HEAD_DIM), lambda b, h, i: (0, 0))
    return pl.pallas_call(
        _diff_attn_kernel,
        grid=(batch, DIFF_HEADS // DIFF_GROUP, nq),
        in_specs=[
            pl.BlockSpec((ATT_TQ, w), lambda b, h, i: (b * nq + i, h)),
            pl.BlockSpec((seq, w), lambda b, h, i: (b, h)),
            pl.BlockSpec((seq, w), lambda b, h, i: (b, h)),
            vec, vec, vec, vec,
            pl.BlockSpec((1, DIFF_V_DIM), lambda b, h, i: (0, 0)),
        ],
        out_specs=pl.BlockSpec((ATT_TQ, w), lambda b, h, i: (b * nq + i, h)),
        out_shape=jax.ShapeDtypeStruct((batch * seq, DIFF_OUT), BF16),
        compiler_params=pltpu.CompilerParams(
            dimension_semantics=("parallel", "parallel", "parallel"),
            vmem_limit_bytes=_vmem_limit(2 * (2 * seq + 2 * ATT_TQ) * w * 2
                                         + 16 * DIFF_GROUP * ATT_TQ * ATT_TK * 4)),
        name="diff_attn",
    )(q, k, v, lq1, lk1, lq2, lk2, g)


def _out_kernel(x_ref, oa_ref, ob_ref, wa_ref, wb_ref, g_ref, o_ref):
    o = _dot_bf16(oa_ref[...], wa_ref[...]) + _dot_bf16(ob_ref[...], wb_ref[...])
    o_ref[...] = x_ref[...] + _rms(o, g_ref[...])


def _out_proj(x, oa, ob, wa, wb, g):
    t, d = x.shape
    tm = OUT_TM
    const = lambda i: (0, 0)
    row = lambda i: (i, 0)
    once = dict(pipeline_mode=pl.Buffered(1))
    nbytes = (d * d * (wa.dtype.itemsize + 2) + 4 * tm * d * 4 + 2 * tm * d * 2 + 2 * tm * d * 4)
    return pl.pallas_call(
        _out_kernel,
        grid=(t // tm,),
        in_specs=[
            pl.BlockSpec((tm, d), row),
            pl.BlockSpec((tm, MLA_OUT), row),
            pl.BlockSpec((tm, DIFF_OUT), row),
            pl.BlockSpec((MLA_OUT, d), const, **once),
            pl.BlockSpec((DIFF_OUT, d), const, **once),
            pl.BlockSpec((1, d), const),
        ],
        out_specs=pl.BlockSpec((tm, d), row),
        out_shape=jax.ShapeDtypeStruct((t, d), F32),
        compiler_params=pltpu.CompilerParams(
            dimension_semantics=("parallel",),
            vmem_limit_bytes=_vmem_limit(nbytes)),
        name="out_proj",
    )(x, oa, ob, wa, wb, g)


def _pack_w_uq(w_uq):
    r = w_uq.shape[0]
    w = w_uq.reshape(r, MLA_HEADS, MLA_QK_DIM)
    pad = jnp.zeros((r, MLA_HEADS, MLA_HEAD_PAD - MLA_QK_DIM), w_uq.dtype)
    return jnp.concatenate([w, pad], axis=-1).reshape(r, MLA_HEADS * MLA_HEAD_PAD).astype(BF16)


def _rope_tables(seq):
    pos = jnp.arange(seq, dtype=F32)[:, None]

    def angles(d):
        inv_freq = ROPE_THETA ** (-jnp.arange(0, d, 2, dtype=F32) / d)
        return pos * inv_freq[None, :]

    ang = angles(MLA_ROPE_DIM)
    zeros = jnp.zeros((seq, V7X_LANES - MLA_ROPE_DIM), F32)
    half0 = jnp.zeros_like(ang)
    cos_m = jnp.concatenate([jnp.cos(ang), jnp.cos(ang), zeros], axis=1)
    sin_up = jnp.concatenate([half0, jnp.sin(ang), zeros], axis=1)
    sin_dn = jnp.concatenate([-jnp.sin(ang), half0, zeros], axis=1)
    ang = angles(DIFF_HEAD_DIM)
    cos_d = jnp.concatenate([jnp.cos(ang), jnp.cos(ang)], axis=1)
    sin_d = jnp.concatenate([-jnp.sin(ang), jnp.sin(ang)], axis=1)
    return cos_m, sin_up, sin_dn, cos_d, sin_d


def kernel(x, ffn1_pre_g, ffn1_w_gate, ffn1_w_up, ffn1_w_down, ffn1_post_g, mix_pre_g, w_in, mla_q_norm_g, mla_w_uq, mla_kv_norm_g, mla_w_ukv, diff_lambda_q1, diff_lambda_k1, diff_lambda_q2, diff_lambda_k2, diff_subln_g, w_out, mix_post_g, ffn2_pre_g, ffn2_w_gate, ffn2_w_up, ffn2_w_down, ffn2_post_g):
    batch, seq, d = x.shape
    depth = ffn1_pre_g.shape[0]
    assert depth == 1 and d == D_MODEL
    assert seq % ATT_TQ == 0 and seq % PROJ_TM == 0 and (batch * seq) % FFN_TM == 0
    tables = _rope_tables(seq)
    xt = x.reshape(batch * seq, d)
    for l in range(depth):
        xt = _ffn(xt, ffn1_pre_g[l][None], ffn1_w_gate[l], ffn1_w_up[l], ffn1_w_down[l],
                  ffn1_post_g[l][None])
        q, k, v, dq, dk, dv = _mix_proj(
            xt, mix_pre_g[l][None], w_in[l], mla_q_norm_g[l][None],
            _pack_w_uq(mla_w_uq[l]), mla_kv_norm_g[l][None], mla_w_ukv[l],
            tables, seq)
        o_mla = _mla_attn(q, k, v, batch, seq)
        o_diff = _diff_attn(dq, dk, dv, diff_lambda_q1[l][None], diff_lambda_k1[l][None],
                            diff_lambda_q2[l][None], diff_lambda_k2[l][None],
                            diff_subln_g[l][None], batch, seq)
        wo = w_out[l]
        xt = _out_proj(xt, o_mla, o_diff, wo[:MLA_OUT], wo[MLA_OUT:], mix_post_g[l][None])
        xt = _ffn(xt, ffn2_pre_g[l][None], ffn2_w_gate[l], ffn2_w_up[l], ffn2_w_down[l],
                  ffn2_post_g[l][None])
    return xt.reshape(batch, seq, d)
```

```python
import functools
import math

import jax
import jax.numpy as jnp
import numpy as np
from jax import lax
from jax.experimental import pallas as pl
from jax.experimental.pallas import tpu as pltpu

D_MODEL = 2048
CHUNK = 64
ROPE_THETA = 10000.0
NORM_EPS = 1e-6
MLA_HEADS = 8
MLA_Q_RANK = 512
MLA_KV_RANK = 512
MLA_NOPE_DIM = 128
MLA_ROPE_DIM = 64
MLA_V_DIM = 128
MLA_QK_DIM = MLA_NOPE_DIM + MLA_ROPE_DIM
DIFF_HEADS = 4
DIFF_HEAD_DIM = 128
DIFF_V_DIM = 2 * DIFF_HEAD_DIM
MLA_OUT = MLA_HEADS * MLA_V_DIM
DIFF_OUT = DIFF_HEADS * DIFF_V_DIM
DIFF_QK_W = DIFF_HEADS * 2 * DIFF_HEAD_DIM
D_FF = 5632
LAMBDA_INIT = 0.8 - 0.6 * math.exp(-0.3 * 0)
LOG2_E = math.log2(math.e)

V7X_LANES = 128
V7X_VMEM_BYTES = 64 * 1024 * 1024
MLA_HEAD_PAD = 2 * V7X_LANES

FFN_TM = 1024
FFN_TF = 512
FFN_ROWS = 256
PROJ_TM = 256
ATT_TQ = 512
ATT_TK = 512
MLA_GROUP = 4
DIFF_GROUP = 2
OUT_TM = 512

BF16 = jnp.bfloat16
F32 = jnp.float32


def _vmem_limit(nbytes):
    return int(min(V7X_VMEM_BYTES - 8 * 1024 * 1024, max(32 * 1024 * 1024, nbytes * 3 // 2)))


def _dot_bf16(a, w):
    return lax.dot_general(a, w, (((1,), (0,)), ((), ())), preferred_element_type=F32)


def _rms(v, g):
    ms = jnp.mean(v * v, axis=-1, keepdims=True)
    return v * lax.rsqrt(ms + NORM_EPS) * g


def _swiglu_hidden(h, wg, wu):
    g = _dot_bf16(h, wg)
    u = _dot_bf16(h, wu)
    return (g * (1.0 / (1.0 + jnp.exp(-g))) * u).astype(BF16)


def _ffn_kernel(x_ref, pre_g_ref, wg_ref, wu_ref, wd_ref, post_g_ref, o_ref, h_ref):
    f = pl.program_id(1)
    last = pl.num_programs(1) - 1
    chunks = [pl.ds(c * FFN_ROWS, FFN_ROWS) for c in range(FFN_TM // FFN_ROWS)]

    @pl.when(f == 0)
    def _():
        wg, wu, wd = wg_ref[...].astype(BF16), wu_ref[...].astype(BF16), wd_ref[...].astype(BF16)
        for rows in chunks:
            h = _rms(x_ref[rows, :], pre_g_ref[...]).astype(BF16)
            h_ref[rows, :] = h
            o_ref[rows, :] = _dot_bf16(_swiglu_hidden(h, wg, wu), wd)

    @pl.when(jnp.logical_and(f > 0, f < last))
    def _():
        a = _swiglu_hidden(h_ref[...], wg_ref[...], wu_ref[...])
        o_ref[...] += _dot_bf16(a, wd_ref[...])

    @pl.when(f == last)
    def _():
        wg, wu, wd = wg_ref[...].astype(BF16), wu_ref[...].astype(BF16), wd_ref[...].astype(BF16)
        for rows in chunks:
            acc = o_ref[rows, :] + _dot_bf16(_swiglu_hidden(h_ref[rows, :], wg, wu), wd)
            o_ref[rows, :] = x_ref[rows, :] + 0.5 * _rms(acc, post_g_ref[...])


def _ffn(x, pre_g, wg, wu, wd, post_g):
    t, d = x.shape
    assert D_FF // FFN_TF >= 2, "first and last hidden steps must be distinct"
    nbytes = (2 * FFN_TM * d * 4 + FFN_TM * d * 2 + 2 * 3 * d * FFN_TF * wg.dtype.itemsize
              + 3 * FFN_TM * FFN_TF * 4)
    return pl.pallas_call(
        _ffn_kernel,
        grid=(t // FFN_TM, D_FF // FFN_TF),
        in_specs=[
            pl.BlockSpec((FFN_TM, d), lambda i, f: (i, 0), pipeline_mode=pl.Buffered(1)),
            pl.BlockSpec((1, d), lambda i, f: (0, 0)),
            pl.BlockSpec((d, FFN_TF), lambda i, f: (0, f)),
            pl.BlockSpec((d, FFN_TF), lambda i, f: (0, f)),
            pl.BlockSpec((FFN_TF, d), lambda i, f: (f, 0)),
            pl.BlockSpec((1, d), lambda i, f: (0, 0)),
        ],
        out_specs=pl.BlockSpec((FFN_TM, d), lambda i, f: (i, 0), pipeline_mode=pl.Buffered(1)),
        out_shape=jax.ShapeDtypeStruct((t, d), F32),
        scratch_shapes=[pltpu.VMEM((FFN_TM, d), BF16)],
        compiler_params=pltpu.CompilerParams(
            dimension_semantics=("parallel", "arbitrary"),
            vmem_limit_bytes=_vmem_limit(nbytes)),
        name="ffn",
    )(x, pre_g, wg, wu, wd, post_g)


_IN_A = MLA_Q_RANK + MLA_KV_RANK + V7X_LANES
_IN_DQ = MLA_Q_RANK + MLA_KV_RANK + MLA_ROPE_DIM
_IN_DK = _IN_DQ + DIFF_QK_W
_IN_DV = _IN_DK + DIFF_QK_W
_IN_W = _IN_DV + DIFF_OUT


def _rope_half(v, cos, sin_signed):
    return v * cos + pltpu.roll(v, DIFF_HEAD_DIM // 2, 1) * sin_signed


def _rope_mla(v, cos, sin_up, sin_dn):
    half = MLA_ROPE_DIM // 2
    return (v * cos + pltpu.roll(v, half, 1) * sin_up
            + pltpu.roll(v, V7X_LANES - half, 1) * sin_dn)


def _proj_kernel(x_ref, g_ref, wa_ref, wdq_ref, wdk_ref, wdv_ref, qg_ref, wuq_ref, kvg_ref, wukv_ref,
                 cm_ref, sup_ref, sdn_ref, cd_ref, sd_ref,
                 q_ref, k_ref, v_ref, dq_ref, dk_ref, dv_ref):
    h = _rms(x_ref[...], g_ref[...]).astype(BF16)

    pa = _dot_bf16(h, wa_ref[...])
    cq = _rms(pa[:, 0:MLA_Q_RANK], qg_ref[...]).astype(BF16)
    ckv = _rms(pa[:, MLA_Q_RANK:MLA_Q_RANK + MLA_KV_RANK], kvg_ref[...]).astype(BF16)
    cm, sup, sdn = cm_ref[...], sup_ref[...], sdn_ref[...]
    kpe = _rope_mla(pa[:, MLA_Q_RANK + MLA_KV_RANK:_IN_A], cm, sup, sdn).astype(BF16)

    q = _dot_bf16(cq, wuq_ref[...])
    kv = _dot_bf16(ckv, wukv_ref[...])
    for hh in range(MLA_HEADS):
        a = hh * MLA_HEAD_PAD
        b = a + V7X_LANES
        c = b + V7X_LANES
        q_ref[:, a:b] = q[:, a:b].astype(BF16)
        q_ref[:, b:c] = _rope_mla(q[:, b:c], cm, sup, sdn).astype(BF16)
        k_ref[:, a:b] = kv[:, a:b].astype(BF16)
        k_ref[:, b:c] = kpe
        v_ref[:, hh * MLA_V_DIM:(hh + 1) * MLA_V_DIM] = kv[:, b:c].astype(BF16)

    cd, sd = cd_ref[...], sd_ref[...]
    pdq = _dot_bf16(h, wdq_ref[...])
    for c in range(DIFF_QK_W // DIFF_HEAD_DIM):
        cols = slice(c * DIFF_HEAD_DIM, (c + 1) * DIFF_HEAD_DIM)
        dq_ref[:, cols] = _rope_half(pdq[:, cols], cd, sd).astype(BF16)
    pdk = _dot_bf16(h, wdk_ref[...])
    for c in range(DIFF_QK_W // DIFF_HEAD_DIM):
        cols = slice(c * DIFF_HEAD_DIM, (c + 1) * DIFF_HEAD_DIM)
        dk_ref[:, cols] = _rope_half(pdk[:, cols], cd, sd).astype(BF16)
    dv_ref[...] = _dot_bf16(h, wdv_ref[...]).astype(BF16)


def _mix_proj(x, g, w_in, qg, wuq_p, kvg, wukv, tables, seq):
    t, d = x.shape
    tm = PROJ_TM
    pos_blocks = seq // tm
    const = lambda i: (0, 0)
    row = lambda i: (i, 0)
    pos = lambda i: (i % pos_blocks, 0)
    wq = MLA_HEADS * MLA_HEAD_PAD
    out_w = (wq, wq, MLA_OUT, DIFF_QK_W, DIFF_QK_W, DIFF_OUT)
    w_a = w_in[:, :_IN_A].astype(BF16)
    w_dq = w_in[:, _IN_DQ:_IN_DK].astype(BF16)
    w_dk = w_in[:, _IN_DK:_IN_DV].astype(BF16)
    w_dv = w_in[:, _IN_DV:_IN_W].astype(BF16)
    once = dict(pipeline_mode=pl.Buffered(1))
    nbytes = (d * (_IN_A + 3 * DIFF_QK_W) * 2 + MLA_Q_RANK * wq * (2 + 4) + 2 * tm * d * 4
              + 2 * tm * sum(out_w) * 2 + tm * _IN_W * 4 + 2 * tm * wq * 4)
    return pl.pallas_call(
        _proj_kernel,
        grid=(t // tm,),
        in_specs=[
            pl.BlockSpec((tm, d), row),
            pl.BlockSpec((1, d), const),
            pl.BlockSpec((d, _IN_A), const, **once),
            pl.BlockSpec((d, DIFF_QK_W), const, **once),
            pl.BlockSpec((d, DIFF_QK_W), const, **once),
            pl.BlockSpec((d, DIFF_OUT), const, **once),
            pl.BlockSpec((1, MLA_Q_RANK), const),
            pl.BlockSpec((MLA_Q_RANK, wq), const, **once),
            pl.BlockSpec((1, MLA_KV_RANK), const),
            pl.BlockSpec((MLA_KV_RANK, wq), const, **once),
        ] + [pl.BlockSpec((tm, V7X_LANES), pos)] * 5,
        out_specs=[pl.BlockSpec((tm, w), row) for w in out_w],
        out_shape=[jax.ShapeDtypeStruct((t, w), BF16) for w in out_w],
        compiler_params=pltpu.CompilerParams(
            dimension_semantics=("parallel",),
            vmem_limit_bytes=_vmem_limit(nbytes)),
        name="mix_proj",
    )(x, g, w_a, w_dq, w_dk, w_dv, qg, wuq_p, kvg, wukv, *tables)


def _diag_mask():
    r = lax.broadcasted_iota(jnp.int32, (ATT_TQ, ATT_TK), 0) // CHUNK
    c = lax.broadcasted_iota(jnp.int32, (ATT_TQ, ATT_TK), 1) // CHUNK
    return r >= c


def _scores(q, k, scale_log2):
    t = lax.dot_general(q, k, (((1,), (1,)), ((), ())), preferred_element_type=F32)
    return t * scale_log2


def _flash_scratch(n_maps, v_dim):
    stat = pltpu.VMEM((ATT_TQ, V7X_LANES), F32)
    return [stat, stat, pltpu.VMEM((ATT_TQ, v_dim), F32)] * n_maps


def _flash_init(m_ref, l_ref, acc_ref):
    m_ref[...] = jnp.full(m_ref.shape, -jnp.inf, F32)
    l_ref[...] = jnp.zeros(l_ref.shape, F32)
    acc_ref[...] = jnp.zeros(acc_ref.shape, F32)


def _flash_step(t, v, m_ref, l_ref, acc_ref):
    lanes = m_ref.shape[1]
    m_old = m_ref[...]
    m_new = jnp.maximum(m_old, jnp.max(t, axis=-1, keepdims=True))
    alpha = jnp.exp2(m_old - m_new)
    p = jnp.exp2(t - jnp.tile(m_new, (1, t.shape[1] // lanes)))
    l_ref[...] = alpha * l_ref[...] + jnp.sum(p, axis=-1, keepdims=True)
    pv = jnp.dot(p.astype(BF16), v, preferred_element_type=F32)
    acc_ref[...] = jnp.tile(alpha, (1, acc_ref.shape[1] // lanes)) * acc_ref[...] + pv
    m_ref[...] = m_new


def _mla_attn_kernel(q_ref, k_ref, v_ref, o_ref, *state):
    qi = pl.program_id(2)
    c = MLA_QK_DIM ** -0.5 * LOG2_E
    wq, wv = MLA_HEAD_PAD, MLA_V_DIM
    heads = range(MLA_GROUP)
    mask = _diag_mask()

    def block(j, masked):
        rows = pl.ds(pl.multiple_of(j * ATT_TK, ATT_TK), ATT_TK)
        for g in heads:
            t = _scores(q_ref[:, g * wq:(g + 1) * wq], k_ref[rows, g * wq:(g + 1) * wq], c)
            if masked:
                t = jnp.where(mask, t, -jnp.inf)
            _flash_step(t, v_ref[rows, g * wv:(g + 1) * wv], *state[3 * g:3 * g + 3])

    for g in heads:
        _flash_init(*state[3 * g:3 * g + 3])
    block(qi, True)

    def body(j, carry):
        block(j, False)
        return carry

    lax.fori_loop(0, qi, body, 0)
    for g in heads:
        _, l_ref, acc_ref = state[3 * g:3 * g + 3]
        o_ref[:, g * wv:(g + 1) * wv] = (acc_ref[...] / l_ref[...]).astype(o_ref.dtype)


def _mla_attn(q, k, v, batch, seq):
    nq = seq // ATT_TQ
    wq, wv = MLA_GROUP * MLA_HEAD_PAD, MLA_GROUP * MLA_V_DIM
    return pl.pallas_call(
        _mla_attn_kernel,
        grid=(batch, MLA_HEADS // MLA_GROUP, nq),
        in_specs=[
            pl.BlockSpec((ATT_TQ, wq), lambda b, h, i: (b * nq + i, h)),
            pl.BlockSpec((seq, wq), lambda b, h, i: (b, h)),
            pl.BlockSpec((seq, wv), lambda b, h, i: (b, h)),
        ],
        out_specs=pl.BlockSpec((ATT_TQ, wv), lambda b, h, i: (b * nq + i, h)),
        out_shape=jax.ShapeDtypeStruct((batch * seq, MLA_OUT), BF16),
        scratch_shapes=_flash_scratch(MLA_GROUP, MLA_V_DIM),
        compiler_params=pltpu.CompilerParams(
            dimension_semantics=("parallel", "parallel", "parallel"),
            vmem_limit_bytes=_vmem_limit(2 * (seq + ATT_TQ) * (wq + wv) * 2
                                         + 8 * MLA_GROUP * ATT_TQ * ATT_TK * 4)),
        name="mla_attn",
    )(q, k, v)


def _diff_attn_kernel(q_ref, k_ref, v_ref, lq1_ref, lk1_ref, lq2_ref, lk2_ref, g_ref, o_ref,
                      *state):
    qi = pl.program_id(2)
    c = DIFF_HEAD_DIM ** -0.5 * LOG2_E
    d, wv = DIFF_HEAD_DIM, DIFF_V_DIM
    maps = range(2 * DIFF_GROUP)
    mask = _diag_mask()

    def block(j, masked):
        rows = pl.ds(pl.multiple_of(j * ATT_TK, ATT_TK), ATT_TK)
        for n in maps:
            t = _scores(q_ref[:, n * d:(n + 1) * d], k_ref[rows, n * d:(n + 1) * d], c)
            if masked:
                t = jnp.where(mask, t, -jnp.inf)
            _flash_step(t, v_ref[rows, (n // 2) * wv:(n // 2 + 1) * wv], *state[3 * n:3 * n + 3])

    for n in maps:
        _flash_init(*state[3 * n:3 * n + 3])
    block(qi, True)

    def body(j, carry):
        block(j, False)
        return carry

    lax.fori_loop(0, qi, body, 0)
    lam = (jnp.exp(jnp.sum(lq1_ref[...] * lk1_ref[...], axis=-1, keepdims=True))
           - jnp.exp(jnp.sum(lq2_ref[...] * lk2_ref[...], axis=-1, keepdims=True))
           + LAMBDA_INIT)
    rep = wv // V7X_LANES
    for h in range(DIFF_GROUP):
        _, l1_ref, a1_ref, _, l2_ref, a2_ref = state[6 * h:6 * h + 6]
        o = (a1_ref[...] / jnp.tile(l1_ref[...], (1, rep))
             - lam * (a2_ref[...] / jnp.tile(l2_ref[...], (1, rep))))
        o_ref[:, h * wv:(h + 1) * wv] = (
            _rms(o, g_ref[...]) * (1.0 - LAMBDA_INIT)).astype(o_ref.dtype)


def _diff_attn(q, k, v, lq1, lk1, lq2, lk2, g, batch, seq):
    nq = seq // ATT_TQ
    w = DIFF_GROUP * DIFF_V_DIM
    vec = pl.BlockSpec((1, DIFF_HEAD_DIM), lambda b, h, i: (0, 0))
    return pl.pallas_call(
        _diff_attn_kernel,
        grid=(batch, DIFF_HEADS // DIFF_GROUP, nq),
        in_specs=[
            pl.BlockSpec((ATT_TQ, w), lambda b, h, i: (b * nq + i, h)),
            pl.BlockSpec((seq, w), lambda b, h, i: (b, h)),
            pl.BlockSpec((seq, w), lambda b, h, i: (b, h)),
            vec, vec, vec, vec,
            pl.BlockSpec((1, DIFF_V_DIM), lambda b, h, i: (0, 0)),
        ],
        out_specs=pl.BlockSpec((ATT_TQ, w), lambda b, h, i: (b * nq + i, h)),
        out_shape=jax.ShapeDtypeStruct((batch * seq, DIFF_OUT), BF16),
        scratch_shapes=_flash_scratch(2 * DIFF_GROUP, DIFF_V_DIM),
        compiler_params=pltpu.CompilerParams(
            dimension_semantics=("parallel", "parallel", "parallel"),
            vmem_limit_bytes=_vmem_limit(2 * (2 * seq + 2 * ATT_TQ) * w * 2
                                         + 16 * DIFF_GROUP * ATT_TQ * ATT_TK * 4)),
        name="diff_attn",
    )(q, k, v, lq1, lk1, lq2, lk2, g)


def _out_kernel(x_ref, oa_ref, ob_ref, wa_ref, wb_ref, g_ref, o_ref):
    o = _dot_bf16(oa_ref[...], wa_ref[...]) + _dot_bf16(ob_ref[...], wb_ref[...])
    o_ref[...] = x_ref[...] + _rms(o, g_ref[...])


def _out_proj(x, oa, ob, wa, wb, g):
    t, d = x.shape
    tm = OUT_TM
    const = lambda i: (0, 0)
    row = lambda i: (i, 0)
    once = dict(pipeline_mode=pl.Buffered(1))
    nbytes = (d * d * (wa.dtype.itemsize + 2) + 4 * tm * d * 4 + 2 * tm * d * 2 + 2 * tm * d * 4)
    return pl.pallas_call(
        _out_kernel,
        grid=(t // tm,),
        in_specs=[
            pl.BlockSpec((tm, d), row),
            pl.BlockSpec((tm, MLA_OUT), row),
            pl.BlockSpec((tm, DIFF_OUT), row),
            pl.BlockSpec((MLA_OUT, d), const, **once),
            pl.BlockSpec((DIFF_OUT, d), const, **once),
            pl.BlockSpec((1, d), const),
        ],
        out_specs=pl.BlockSpec((tm, d), row),
        out_shape=jax.ShapeDtypeStruct((t, d), F32),
        compiler_params=pltpu.CompilerParams(
            dimension_semantics=("parallel",),
            vmem_limit_bytes=_vmem_limit(nbytes)),
        name="out_proj",
    )(x, oa, ob, wa, wb, g)


def _pack_w_uq(w_uq):
    r = w_uq.shape[0]
    w = w_uq.reshape(r, MLA_HEADS, MLA_QK_DIM)
    pad = jnp.zeros((r, MLA_HEADS, MLA_HEAD_PAD - MLA_QK_DIM), w_uq.dtype)
    return jnp.concatenate([w, pad], axis=-1).reshape(r, MLA_HEADS * MLA_HEAD_PAD).astype(BF16)


def _rope_tables(seq):
    pos = jnp.arange(seq, dtype=F32)[:, None]

    def angles(d):
        inv_freq = ROPE_THETA ** (-jnp.arange(0, d, 2, dtype=F32) / d)
        return pos * inv_freq[None, :]

    ang = angles(MLA_ROPE_DIM)
    zeros = jnp.zeros((seq, V7X_LANES - MLA_ROPE_DIM), F32)
    half0 = jnp.zeros_like(ang)
    cos_m = jnp.concatenate([jnp.cos(ang), jnp.cos(ang), zeros], axis=1)
    sin_up = jnp.concatenate([half0, jnp.sin(ang), zeros], axis=1)
    sin_dn = jnp.concatenate([-jnp.sin(ang), half0, zeros], axis=1)
    ang = angles(DIFF_HEAD_DIM)
    cos_d = jnp.concatenate([jnp.cos(ang), jnp.cos(ang)], axis=1)
    sin_d = jnp.concatenate([-jnp.sin(ang), jnp.sin(ang)], axis=1)
    return cos_m, sin_up, sin_dn, cos_d, sin_d


def kernel(x, ffn1_pre_g, ffn1_w_gate, ffn1_w_up, ffn1_w_down, ffn1_post_g, mix_pre_g, w_in, mla_q_norm_g, mla_w_uq, mla_kv_norm_g, mla_w_ukv, diff_lambda_q1, diff_lambda_k1, diff_lambda_q2, diff_lambda_k2, diff_subln_g, w_out, mix_post_g, ffn2_pre_g, ffn2_w_gate, ffn2_w_up, ffn2_w_down, ffn2_post_g):
    batch, seq, d = x.shape
    depth = ffn1_pre_g.shape[0]
    assert depth == 1 and d == D_MODEL
    assert seq % ATT_TQ == 0 and seq % PROJ_TM == 0 and (batch * seq) % FFN_TM == 0
    tables = _rope_tables(seq)
    xt = x.reshape(batch * seq, d)
    for l in range(depth):
        xt = _ffn(xt, ffn1_pre_g[l][None], ffn1_w_gate[l], ffn1_w_up[l], ffn1_w_down[l],
                  ffn1_post_g[l][None])
        q, k, v, dq, dk, dv = _mix_proj(
            xt, mix_pre_g[l][None], w_in[l], mla_q_norm_g[l][None],
            _pack_w_uq(mla_w_uq[l]), mla_kv_norm_g[l][None], mla_w_ukv[l],
            tables, seq)
        o_mla = _mla_attn(q, k, v, batch, seq)
        o_diff = _diff_attn(dq, dk, dv, diff_lambda_q1[l][None], diff_lambda_k1[l][None],
                            diff_lambda_q2[l][None], diff_lambda_k2[l][None],
                            diff_subln_g[l][None], batch, seq)
        wo = w_out[l]
        xt = _out_proj(xt, o_mla, o_diff, wo[:MLA_OUT], wo[MLA_OUT:], mix_post_g[l][None])
        xt = _ffn(xt, ffn2_pre_g[l][None], ffn2_w_gate[l], ffn2_w_up[l], ffn2_w_down[l],
                  ffn2_post_g[l][None])
    return xt.reshape(batch, seq, d)
```

```python
import math

import jax
import jax.numpy as jnp
from jax import lax
from jax.experimental import pallas as pl
from jax.experimental.pallas import tpu as pltpu

D_MODEL = 2048
CHUNK = 64
ROPE_THETA = 10000.0
NORM_EPS = 1e-6
MLA_HEADS = 8
MLA_Q_RANK = 512
MLA_KV_RANK = 512
MLA_NOPE_DIM = 128
MLA_ROPE_DIM = 64
MLA_V_DIM = 128
MLA_QK_DIM = MLA_NOPE_DIM + MLA_ROPE_DIM
DIFF_HEADS = 4
DIFF_HEAD_DIM = 128
DIFF_V_DIM = 2 * DIFF_HEAD_DIM
MLA_OUT = MLA_HEADS * MLA_V_DIM
DIFF_OUT = DIFF_HEADS * DIFF_V_DIM
DIFF_QK_W = DIFF_HEADS * 2 * DIFF_HEAD_DIM
D_FF = 5632
LAMBDA_INIT = 0.8 - 0.6 * math.exp(-0.3 * 0)
LOG2_E = math.log2(math.e)
MLA_SCORE_SCALE = MLA_QK_DIM ** -0.5 * LOG2_E
DIFF_SCORE_SCALE = DIFF_HEAD_DIM ** -0.5 * LOG2_E

V7X_LANES = 128
V7X_VMEM_BYTES = 64 * 1024 * 1024
MLA_HEAD_PAD = 2 * V7X_LANES

FFN_TM = 1024
FFN_TF = 512
FFN_ROWS = 256
PROJ_TM = 256
ATT_TQ = 512
ATT_TK = 512
MLA_GROUP = 8
DIFF_GROUP = 4
OUT_TM = 512

BF16 = jnp.bfloat16
F32 = jnp.float32


def _vmem_limit(nbytes):
    return int(min(V7X_VMEM_BYTES - 8 * 1024 * 1024, max(32 * 1024 * 1024, nbytes * 3 // 2)))


def _dot_bf16(a, w):
    return lax.dot_general(a, w, (((1,), (0,)), ((), ())), preferred_element_type=F32)


def _rms(v, g):
    ms = jnp.mean(v * v, axis=-1, keepdims=True)
    return v * lax.rsqrt(ms + NORM_EPS) * g


def _swiglu_hidden(h, wg, wu):
    g = _dot_bf16(h, wg)
    u = _dot_bf16(h, wu)
    return (g * (1.0 / (1.0 + jnp.exp(-g))) * u).astype(BF16)


def _ffn_kernel(x_ref, pre_g_ref, wg_ref, wu_ref, wd_ref, post_g_ref, o_ref, h_ref):
    f = pl.program_id(1)
    last = pl.num_programs(1) - 1
    chunks = [pl.ds(c * FFN_ROWS, FFN_ROWS) for c in range(FFN_TM // FFN_ROWS)]

    @pl.when(f == 0)
    def _():
        wg, wu, wd = wg_ref[...].astype(BF16), wu_ref[...].astype(BF16), wd_ref[...].astype(BF16)
        for rows in chunks:
            h = _rms(x_ref[rows, :], pre_g_ref[...]).astype(BF16)
            h_ref[rows, :] = h
            o_ref[rows, :] = _dot_bf16(_swiglu_hidden(h, wg, wu), wd)

    @pl.when(jnp.logical_and(f > 0, f < last))
    def _():
        a = _swiglu_hidden(h_ref[...], wg_ref[...], wu_ref[...])
        o_ref[...] += _dot_bf16(a, wd_ref[...])

    @pl.when(f == last)
    def _():
        wg, wu, wd = wg_ref[...].astype(BF16), wu_ref[...].astype(BF16), wd_ref[...].astype(BF16)
        for rows in chunks:
            acc = o_ref[rows, :] + _dot_bf16(_swiglu_hidden(h_ref[rows, :], wg, wu), wd)
            o_ref[rows, :] = x_ref[rows, :] + 0.5 * _rms(acc, post_g_ref[...])


def _ffn(x, pre_g, wg, wu, wd, post_g):
    t, d = x.shape
    assert D_FF // FFN_TF >= 2, "first and last hidden steps must be distinct"
    nbytes = (2 * FFN_TM * d * 4 + FFN_TM * d * 2 + 2 * 3 * d * FFN_TF * wg.dtype.itemsize
              + 3 * FFN_TM * FFN_TF * 4)
    return pl.pallas_call(
        _ffn_kernel,
        grid=(t // FFN_TM, D_FF // FFN_TF),
        in_specs=[
            pl.BlockSpec((FFN_TM, d), lambda i, f: (i, 0), pipeline_mode=pl.Buffered(1)),
            pl.BlockSpec((1, d), lambda i, f: (0, 0)),
            pl.BlockSpec((d, FFN_TF), lambda i, f: (0, f)),
            pl.BlockSpec((d, FFN_TF), lambda i, f: (0, f)),
            pl.BlockSpec((FFN_TF, d), lambda i, f: (f, 0)),
            pl.BlockSpec((1, d), lambda i, f: (0, 0)),
        ],
        out_specs=pl.BlockSpec((FFN_TM, d), lambda i, f: (i, 0), pipeline_mode=pl.Buffered(1)),
        out_shape=jax.ShapeDtypeStruct((t, d), F32),
        scratch_shapes=[pltpu.VMEM((FFN_TM, d), BF16)],
        compiler_params=pltpu.CompilerParams(
            dimension_semantics=("parallel", "arbitrary"),
            vmem_limit_bytes=_vmem_limit(nbytes)),
        name="ffn",
    )(x, pre_g, wg, wu, wd, post_g)


_IN_A = MLA_Q_RANK + MLA_KV_RANK + V7X_LANES
_IN_DQ = MLA_Q_RANK + MLA_KV_RANK + MLA_ROPE_DIM
_IN_DK = _IN_DQ + DIFF_QK_W
_IN_DV = _IN_DK + DIFF_QK_W
_IN_W = _IN_DV + DIFF_OUT


def _rope_half(v, cos, sin_signed):
    return v * cos + pltpu.roll(v, DIFF_HEAD_DIM // 2, 1) * sin_signed


def _rope_mla(v, cos, sin_up, sin_dn):
    half = MLA_ROPE_DIM // 2
    return (v * cos + pltpu.roll(v, half, 1) * sin_up
            + pltpu.roll(v, V7X_LANES - half, 1) * sin_dn)


def _proj_kernel(x_ref, g_ref, wa_ref, wdq_ref, wdk_ref, wdv_ref, qg_ref, wuq_ref, kvg_ref, wukv_ref,
                 cm_ref, sup_ref, sdn_ref, cd_ref, sd_ref,
                 q_ref, k_ref, v_ref, dq_ref, dk_ref, dv_ref):
    h = _rms(x_ref[...], g_ref[...]).astype(BF16)

    pa = _dot_bf16(h, wa_ref[...])
    cq = _rms(pa[:, 0:MLA_Q_RANK], qg_ref[...]).astype(BF16)
    ckv = _rms(pa[:, MLA_Q_RANK:MLA_Q_RANK + MLA_KV_RANK], kvg_ref[...]).astype(BF16)
    cm, sup, sdn = cm_ref[...], sup_ref[...], sdn_ref[...]
    kpe = _rope_mla(pa[:, MLA_Q_RANK + MLA_KV_RANK:_IN_A], cm, sup, sdn).astype(BF16)

    q = _dot_bf16(cq, wuq_ref[...]) * MLA_SCORE_SCALE
    kv = _dot_bf16(ckv, wukv_ref[...])
    for hh in range(MLA_HEADS):
        a = hh * MLA_HEAD_PAD
        b = a + V7X_LANES
        c = b + V7X_LANES
        q_ref[:, a:b] = q[:, a:b].astype(BF16)
        q_ref[:, b:c] = _rope_mla(q[:, b:c], cm, sup, sdn).astype(BF16)
        k_ref[:, a:b] = kv[:, a:b].astype(BF16)
        k_ref[:, b:c] = kpe
        v_ref[:, hh * MLA_V_DIM:(hh + 1) * MLA_V_DIM] = kv[:, b:c].astype(BF16)

    cd, sd = cd_ref[...], sd_ref[...]
    pdq = _dot_bf16(h, wdq_ref[...]) * DIFF_SCORE_SCALE
    for c in range(DIFF_QK_W // DIFF_HEAD_DIM):
        cols = slice(c * DIFF_HEAD_DIM, (c + 1) * DIFF_HEAD_DIM)
        dq_ref[:, cols] = _rope_half(pdq[:, cols], cd, sd).astype(BF16)
    pdk = _dot_bf16(h, wdk_ref[...])
    for c in range(DIFF_QK_W // DIFF_HEAD_DIM):
        cols = slice(c * DIFF_HEAD_DIM, (c + 1) * DIFF_HEAD_DIM)
        dk_ref[:, cols] = _rope_half(pdk[:, cols], cd, sd).astype(BF16)
    dv_ref[...] = _dot_bf16(h, wdv_ref[...]).astype(BF16)


def _mix_proj(x, g, w_in, qg, wuq_p, kvg, wukv, tables, seq):
    t, d = x.shape
    tm = PROJ_TM
    pos_blocks = seq // tm
    const = lambda i: (0, 0)
    row = lambda i: (i, 0)
    pos = lambda i: (i % pos_blocks, 0)
    wq = MLA_HEADS * MLA_HEAD_PAD
    out_w = (wq, wq, MLA_OUT, DIFF_QK_W, DIFF_QK_W, DIFF_OUT)
    w_a = w_in[:, :_IN_A].astype(BF16)
    w_dq = w_in[:, _IN_DQ:_IN_DK].astype(BF16)
    w_dk = w_in[:, _IN_DK:_IN_DV].astype(BF16)
    w_dv = w_in[:, _IN_DV:_IN_W].astype(BF16)
    once = dict(pipeline_mode=pl.Buffered(1))
    nbytes = (d * (_IN_A + 3 * DIFF_QK_W) * 2 + MLA_Q_RANK * wq * (2 + 4) + 2 * tm * d * 4
              + 2 * tm * sum(out_w) * 2 + tm * _IN_W * 4 + 2 * tm * wq * 4)
    return pl.pallas_call(
        _proj_kernel,
        grid=(t // tm,),
        in_specs=[
            pl.BlockSpec((tm, d), row),
            pl.BlockSpec((1, d), const),
            pl.BlockSpec((d, _IN_A), const, **once),
            pl.BlockSpec((d, DIFF_QK_W), const, **once),
            pl.BlockSpec((d, DIFF_QK_W), const, **once),
            pl.BlockSpec((d, DIFF_OUT), const, **once),
            pl.BlockSpec((1, MLA_Q_RANK), const),
            pl.BlockSpec((MLA_Q_RANK, wq), const, **once),
            pl.BlockSpec((1, MLA_KV_RANK), const),
            pl.BlockSpec((MLA_KV_RANK, wq), const, **once),
        ] + [pl.BlockSpec((tm, V7X_LANES), pos)] * 5,
        out_specs=[pl.BlockSpec((tm, w), row) for w in out_w],
        out_shape=[jax.ShapeDtypeStruct((t, w), BF16) for w in out_w],
        compiler_params=pltpu.CompilerParams(
            dimension_semantics=("parallel",),
            vmem_limit_bytes=_vmem_limit(nbytes)),
        name="mix_proj",
    )(x, g, w_a, w_dq, w_dk, w_dv, qg, wuq_p, kvg, wukv, *tables)


def _causal_mask(n_q, n_k):
    r = lax.broadcasted_iota(jnp.int32, (n_q, n_k), 0) // CHUNK
    c = lax.broadcasted_iota(jnp.int32, (n_q, n_k), 1) // CHUNK
    return r >= c


def _flash_scratch(n_maps, v_dim):
    stat = pltpu.VMEM((ATT_TQ, V7X_LANES), F32)
    return [stat, stat, pltpu.VMEM((ATT_TQ, v_dim), F32)] * n_maps


def _flash_init(m_ref, l_ref, acc_ref):
    m_ref[...] = jnp.full(m_ref.shape, -jnp.inf, F32)
    l_ref[...] = jnp.zeros(l_ref.shape, F32)
    acc_ref[...] = jnp.zeros(acc_ref.shape, F32)


def _flash_step(t, v, m_ref, l_ref, acc_ref):
    lanes = m_ref.shape[1]
    m_old = m_ref[...]
    m_new = jnp.maximum(m_old, jnp.max(t, axis=-1, keepdims=True))
    alpha = jnp.exp2(m_old - m_new)
    p = jnp.exp2(t - jnp.tile(m_new, (1, t.shape[1] // lanes)))
    l_ref[...] = alpha * l_ref[...] + jnp.sum(p, axis=-1, keepdims=True)
    pv = jnp.dot(p.astype(BF16), v, preferred_element_type=F32)
    acc_ref[...] = jnp.tile(alpha, (1, acc_ref.shape[1] // lanes)) * acc_ref[...] + pv
    m_ref[...] = m_new


def _attn_tile(qi, n_maps, scores, values, state):
    half = ATT_TQ // 2
    assert ATT_TQ == ATT_TK and half % CHUNK == 0

    def update(q_rows, k_rows, mask):
        for n in range(n_maps):
            t = scores(n, q_rows, k_rows)
            if mask is not None:
                t = jnp.where(mask, t, -jnp.inf)
            _flash_step(t, values(n, k_rows), *[r.at[q_rows] for r in state[3 * n:3 * n + 3]])

    for n in range(n_maps):
        _flash_init(*state[3 * n:3 * n + 3])
    base = pl.multiple_of(qi * ATT_TQ, ATT_TQ)
    update(pl.ds(0, ATT_TQ), pl.ds(base, half), _causal_mask(ATT_TQ, half))
    update(pl.ds(half, half), pl.ds(pl.multiple_of(base + half, half), half),
           _causal_mask(half, half))

    def body(j, carry):
        update(pl.ds(0, ATT_TQ), pl.ds(pl.multiple_of(j * ATT_TK, ATT_TK), ATT_TK), None)
        return carry

    lax.fori_loop(0, qi, body, 0)


def _qk(q, k):
    return lax.dot_general(q, k, (((1,), (1,)), ((), ())), preferred_element_type=F32)


def _mla_attn_kernel(q_ref, k_ref, v_ref, o_ref, *state):
    wq, wv = MLA_HEAD_PAD, MLA_V_DIM

    def scores(g, q_rows, k_rows):
        return _qk(q_ref[q_rows, g * wq:(g + 1) * wq], k_ref[k_rows, g * wq:(g + 1) * wq])

    def values(g, k_rows):
        return v_ref[k_rows, g * wv:(g + 1) * wv]

    _attn_tile(pl.program_id(2), MLA_GROUP, scores, values, state)
    for g in range(MLA_GROUP):
        _, l_ref, acc_ref = state[3 * g:3 * g + 3]
        o_ref[:, g * wv:(g + 1) * wv] = (acc_ref[...] / l_ref[...]).astype(o_ref.dtype)


def _mla_attn(q, k, v, batch, seq):
    nq = seq // ATT_TQ
    wq, wv = MLA_GROUP * MLA_HEAD_PAD, MLA_GROUP * MLA_V_DIM
    return pl.pallas_call(
        _mla_attn_kernel,
        grid=(batch, MLA_HEADS // MLA_GROUP, nq),
        in_specs=[
            pl.BlockSpec((ATT_TQ, wq), lambda b, h, i: (b * nq + i, h)),
            pl.BlockSpec((seq, wq), lambda b, h, i: (b, h)),
            pl.BlockSpec((seq, wv), lambda b, h, i: (b, h)),
        ],
        out_specs=pl.BlockSpec((ATT_TQ, wv), lambda b, h, i: (b * nq + i, h)),
        out_shape=jax.ShapeDtypeStruct((batch * seq, MLA_OUT), BF16),
        scratch_shapes=_flash_scratch(MLA_GROUP, MLA_V_DIM),
        compiler_params=pltpu.CompilerParams(
            dimension_semantics=("parallel", "parallel", "parallel"),
            vmem_limit_bytes=_vmem_limit(2 * (seq + ATT_TQ) * (wq + wv) * 2
                                         + 8 * MLA_GROUP * ATT_TQ * ATT_TK * 4)),
        name="mla_attn",
    )(q, k, v)


def _diff_attn_kernel(q_ref, k_ref, v_ref, lq1_ref, lk1_ref, lq2_ref, lk2_ref, g_ref, o_ref,
                      *state):
    d, wv = DIFF_HEAD_DIM, DIFF_V_DIM

    def scores(n, q_rows, k_rows):
        return _qk(q_ref[q_rows, n * d:(n + 1) * d], k_ref[k_rows, n * d:(n + 1) * d])

    def values(n, k_rows):
        return v_ref[k_rows, (n // 2) * wv:(n // 2 + 1) * wv]

    _attn_tile(pl.program_id(2), 2 * DIFF_GROUP, scores, values, state)
    lam = (jnp.exp(jnp.sum(lq1_ref[...] * lk1_ref[...], axis=-1, keepdims=True))
           - jnp.exp(jnp.sum(lq2_ref[...] * lk2_ref[...], axis=-1, keepdims=True))
           + LAMBDA_INIT)
    rep = wv // V7X_LANES
    for h in range(DIFF_GROUP):
        _, l1_ref, a1_ref, _, l2_ref, a2_ref = state[6 * h:6 * h + 6]
        o = (a1_ref[...] / jnp.tile(l1_ref[...], (1, rep))
             - lam * (a2_ref[...] / jnp.tile(l2_ref[...], (1, rep))))
        o_ref[:, h * wv:(h + 1) * wv] = (
            _rms(o, g_ref[...]) * (1.0 - LAMBDA_INIT)).astype(o_ref.dtype)


def _diff_attn(q, k, v, lq1, lk1, lq2, lk2, g, batch, seq):
    nq = seq // ATT_TQ
    w = DIFF_GROUP * DIFF_V_DIM
    vec = pl.BlockSpec((1, DIFF_HEAD_DIM), lambda b, h, i: (0, 0))
    return pl.pallas_call(
        _diff_attn_kernel,
        grid=(batch, DIFF_HEADS // DIFF_GROUP, nq),
        in_specs=[
            pl.BlockSpec((ATT_TQ, w), lambda b, h, i: (b * nq + i, h)),
            pl.BlockSpec((seq, w), lambda b, h, i: (b, h)),
            pl.BlockSpec((seq, w), lambda b, h, i: (b, h)),
            vec, vec, vec, vec,
            pl.BlockSpec((1, DIFF_V_DIM), lambda b, h, i: (0, 0)),
        ],
        out_specs=pl.BlockSpec((ATT_TQ, w), lambda b, h, i: (b * nq + i, h)),
        out_shape=jax.ShapeDtypeStruct((batch * seq, DIFF_OUT), BF16),
        scratch_shapes=_flash_scratch(2 * DIFF_GROUP, DIFF_V_DIM),
        compiler_params=pltpu.CompilerParams(
            dimension_semantics=("parallel", "parallel", "parallel"),
            vmem_limit_bytes=_vmem_limit(2 * (2 * seq + 2 * ATT_TQ) * w * 2
                                         + 16 * DIFF_GROUP * ATT_TQ * ATT_TK * 4)),
        name="diff_attn",
    )(q, k, v, lq1, lk1, lq2, lk2, g)


def _out_kernel(x_ref, oa_ref, ob_ref, wa_ref, wb_ref, g_ref, o_ref):
    o = _dot_bf16(oa_ref[...], wa_ref[...]) + _dot_bf16(ob_ref[...], wb_ref[...])
    o_ref[...] = x_ref[...] + _rms(o, g_ref[...])


def _out_proj(x, oa, ob, w, g):
    t, d = x.shape
    assert MLA_OUT == DIFF_OUT, "the two row blocks of w share one block shape"
    tm = OUT_TM
    const = lambda i: (0, 0)
    row = lambda i: (i, 0)
    once = dict(pipeline_mode=pl.Buffered(1))
    nbytes = (d * d * (w.dtype.itemsize + 2) + 4 * tm * d * 4 + 2 * tm * d * 2 + 2 * tm * d * 4)
    return pl.pallas_call(
        _out_kernel,
        grid=(t // tm,),
        in_specs=[
            pl.BlockSpec((tm, d), row),
            pl.BlockSpec((tm, MLA_OUT), row),
            pl.BlockSpec((tm, DIFF_OUT), row),
            pl.BlockSpec((MLA_OUT, d), const, **once),
            pl.BlockSpec((DIFF_OUT, d), lambda i: (1, 0), **once),
            pl.BlockSpec((1, d), const),
        ],
        out_specs=pl.BlockSpec((tm, d), row),
        out_shape=jax.ShapeDtypeStruct((t, d), F32),
        compiler_params=pltpu.CompilerParams(
            dimension_semantics=("parallel",),
            vmem_limit_bytes=_vmem_limit(nbytes)),
        name="out_proj",
    )(x, oa, ob, w, w, g)


def _pack_w_uq(w_uq):
    r = w_uq.shape[0]
    w = w_uq.reshape(r, MLA_HEADS, MLA_QK_DIM)
    pad = jnp.zeros((r, MLA_HEADS, MLA_HEAD_PAD - MLA_QK_DIM), w_uq.dtype)
    return jnp.concatenate([w, pad], axis=-1).reshape(r, MLA_HEADS * MLA_HEAD_PAD).astype(BF16)


def _rope_tables(seq):
    pos = jnp.arange(seq, dtype=F32)[:, None]

    def angles(d):
        inv_freq = ROPE_THETA ** (-jnp.arange(0, d, 2, dtype=F32) / d)
        return pos * inv_freq[None, :]

    ang = angles(MLA_ROPE_DIM)
    zeros = jnp.zeros((seq, V7X_LANES - MLA_ROPE_DIM), F32)
    half0 = jnp.zeros_like(ang)
    cos_m = jnp.concatenate([jnp.cos(ang), jnp.cos(ang), zeros], axis=1)
    sin_up = jnp.concatenate([half0, jnp.sin(ang), zeros], axis=1)
    sin_dn = jnp.concatenate([-jnp.sin(ang), half0, zeros], axis=1)
    ang = angles(DIFF_HEAD_DIM)
    cos_d = jnp.concatenate([jnp.cos(ang), jnp.cos(ang)], axis=1)
    sin_d = jnp.concatenate([-jnp.sin(ang), jnp.sin(ang)], axis=1)
    return cos_m, sin_up, sin_dn, cos_d, sin_d


def kernel(x, ffn1_pre_g, ffn1_w_gate, ffn1_w_up, ffn1_w_down, ffn1_post_g, mix_pre_g, w_in, mla_q_norm_g, mla_w_uq, mla_kv_norm_g, mla_w_ukv, diff_lambda_q1, diff_lambda_k1, diff_lambda_q2, diff_lambda_k2, diff_subln_g, w_out, mix_post_g, ffn2_pre_g, ffn2_w_gate, ffn2_w_up, ffn2_w_down, ffn2_post_g):
    batch, seq, d = x.shape
    depth = ffn1_pre_g.shape[0]
    assert depth == 1 and d == D_MODEL
    assert seq % ATT_TQ == 0 and seq % PROJ_TM == 0 and (batch * seq) % FFN_TM == 0
    tables = _rope_tables(seq)
    xt = x.reshape(batch * seq, d)
    for l in range(depth):
        xt = _ffn(xt, ffn1_pre_g[l][None], ffn1_w_gate[l], ffn1_w_up[l], ffn1_w_down[l],
                  ffn1_post_g[l][None])
        q, k, v, dq, dk, dv = _mix_proj(
            xt, mix_pre_g[l][None], w_in[l], mla_q_norm_g[l][None],
            _pack_w_uq(mla_w_uq[l]), mla_kv_norm_g[l][None], mla_w_ukv[l],
            tables, seq)
        o_mla = _mla_attn(q, k, v, batch, seq)
        o_diff = _diff_attn(dq, dk, dv, diff_lambda_q1[l][None], diff_lambda_k1[l][None],
                            diff_lambda_q2[l][None], diff_lambda_k2[l][None],
                            diff_subln_g[l][None], batch, seq)
        xt = _out_proj(xt, o_mla, o_diff, w_out[l], mix_post_g[l][None])
        xt = _ffn(xt, ffn2_pre_g[l][None], ffn2_w_gate[l], ffn2_w_up[l], ffn2_w_down[l],
                  ffn2_post_g[l][None])
    return xt.reshape(batch, seq, d)
```

```python
import math

import jax
import jax.numpy as jnp
from jax import lax
from jax.experimental import pallas as pl
from jax.experimental.pallas import tpu as pltpu

D_MODEL = 2048
CHUNK = 64
ROPE_THETA = 10000.0
NORM_EPS = 1e-6
MLA_HEADS = 8
MLA_Q_RANK = 512
MLA_KV_RANK = 512
MLA_NOPE_DIM = 128
MLA_ROPE_DIM = 64
MLA_V_DIM = 128
MLA_QK_DIM = MLA_NOPE_DIM + MLA_ROPE_DIM
DIFF_HEADS = 4
DIFF_HEAD_DIM = 128
DIFF_V_DIM = 2 * DIFF_HEAD_DIM
MLA_OUT = MLA_HEADS * MLA_V_DIM
DIFF_OUT = DIFF_HEADS * DIFF_V_DIM
DIFF_QK_W = DIFF_HEADS * 2 * DIFF_HEAD_DIM
D_FF = 5632
LAMBDA_INIT = 0.8 - 0.6 * math.exp(-0.3 * 0)
LOG2_E = math.log2(math.e)
MLA_SCORE_SCALE = MLA_QK_DIM ** -0.5 * LOG2_E
DIFF_SCORE_SCALE = DIFF_HEAD_DIM ** -0.5 * LOG2_E

V7X_LANES = 128
V7X_VMEM_BYTES = 64 * 1024 * 1024
MLA_HEAD_PAD = 2 * V7X_LANES

FFN_TM = 1024
FFN_TF = 512
FFN_ROWS = 256
PROJ_TM = 256
ATT_TQ = 512
ATT_TK = 512
MLA_GROUP = 8
DIFF_GROUP = 4
OUT_TM = 512

BF16 = jnp.bfloat16
F32 = jnp.float32


def _vmem_limit(nbytes):
    return int(min(V7X_VMEM_BYTES - 4 * 1024 * 1024, max(32 * 1024 * 1024, nbytes * 3 // 2)))


def _dot_bf16(a, w):
    return lax.dot_general(a, w, (((1,), (0,)), ((), ())), preferred_element_type=F32)


def _rms(v, g):
    ms = jnp.mean(v * v, axis=-1, keepdims=True)
    return v * lax.rsqrt(ms + NORM_EPS) * g


def _swiglu_hidden(h, wg, wu):
    g = _dot_bf16(h, wg)
    u = _dot_bf16(h, wu)
    return (g * (1.0 / (1.0 + jnp.exp(-g))) * u).astype(BF16)


def _ffn_kernel(x_ref, pre_g_ref, wg_ref, wu_ref, wd_ref, post_g_ref, o_ref, h_ref):
    f = pl.program_id(1)
    last = pl.num_programs(1) - 1
    chunks = [pl.ds(c * FFN_ROWS, FFN_ROWS) for c in range(FFN_TM // FFN_ROWS)]

    @pl.when(f == 0)
    def _():
        for rows in chunks:
            h = _rms(x_ref[rows, :], pre_g_ref[...]).astype(BF16)
            h_ref[rows, :] = h
            o_ref[rows, :] = _dot_bf16(_swiglu_hidden(h, wg_ref[...], wu_ref[...]), wd_ref[...])

    @pl.when(jnp.logical_and(f > 0, f < last))
    def _():
        a = _swiglu_hidden(h_ref[...], wg_ref[...], wu_ref[...])
        o_ref[...] += _dot_bf16(a, wd_ref[...])

    @pl.when(f == last)
    def _():
        for rows in chunks:
            a = _swiglu_hidden(h_ref[rows, :], wg_ref[...], wu_ref[...])
            acc = o_ref[rows, :] + _dot_bf16(a, wd_ref[...])
            o_ref[rows, :] = x_ref[rows, :] + 0.5 * _rms(acc, post_g_ref[...])


def _ffn(x, pre_g, wg, wu, wd, post_g):
    t, d = x.shape
    assert D_FF // FFN_TF >= 2, "first and last hidden steps must be distinct"
    nbytes = (2 * FFN_TM * d * 4 + FFN_TM * d * 2 + 2 * 3 * d * FFN_TF * wg.dtype.itemsize
              + 3 * FFN_TM * FFN_TF * 4)
    return pl.pallas_call(
        _ffn_kernel,
        grid=(t // FFN_TM, D_FF // FFN_TF),
        in_specs=[
            pl.BlockSpec((FFN_TM, d), lambda i, f: (i, 0)),
            pl.BlockSpec((1, d), lambda i, f: (0, 0)),
            pl.BlockSpec((d, FFN_TF), lambda i, f: (0, f)),
            pl.BlockSpec((d, FFN_TF), lambda i, f: (0, f)),
            pl.BlockSpec((FFN_TF, d), lambda i, f: (f, 0)),
            pl.BlockSpec((1, d), lambda i, f: (0, 0)),
        ],
        out_specs=pl.BlockSpec((FFN_TM, d), lambda i, f: (i, 0), pipeline_mode=pl.Buffered(1)),
        out_shape=jax.ShapeDtypeStruct((t, d), F32),
        scratch_shapes=[pltpu.VMEM((FFN_TM, d), BF16)],
        compiler_params=pltpu.CompilerParams(
            dimension_semantics=("parallel", "arbitrary"),
            vmem_limit_bytes=_vmem_limit(nbytes)),
        name="ffn",
    )(x, pre_g, wg, wu, wd, post_g)


_IN_A = MLA_Q_RANK + MLA_KV_RANK + V7X_LANES
_IN_DQ = MLA_Q_RANK + MLA_KV_RANK + MLA_ROPE_DIM
_IN_DK = _IN_DQ + DIFF_QK_W
_IN_DV = _IN_DK + DIFF_QK_W
_IN_W = _IN_DV + DIFF_OUT


def _rope_half(v, cos, sin_signed):
    return v * cos + pltpu.roll(v, DIFF_HEAD_DIM // 2, 1) * sin_signed


def _rope_mla(v, cos, sin_up, sin_dn):
    half = MLA_ROPE_DIM // 2
    return (v * cos + pltpu.roll(v, half, 1) * sin_up
            + pltpu.roll(v, V7X_LANES - half, 1) * sin_dn)


def _proj_kernel(x_ref, g_ref, wa_ref, wdq_ref, wdk_ref, wdv_ref, qg_ref, wuq_ref, kvg_ref, wukv_ref,
                 cm_ref, sup_ref, sdn_ref, cd_ref, sd_ref,
                 q_ref, k_ref, v_ref, dq_ref, dk_ref, dv_ref):
    h = _rms(x_ref[...], g_ref[...]).astype(BF16)

    pa = _dot_bf16(h, wa_ref[...])
    cq = _rms(pa[:, 0:MLA_Q_RANK], qg_ref[...]).astype(BF16)
    ckv = _rms(pa[:, MLA_Q_RANK:MLA_Q_RANK + MLA_KV_RANK], kvg_ref[...]).astype(BF16)
    cm, sup, sdn = cm_ref[...], sup_ref[...], sdn_ref[...]
    kpe = _rope_mla(pa[:, MLA_Q_RANK + MLA_KV_RANK:_IN_A], cm, sup, sdn).astype(BF16)

    q = _dot_bf16(cq, wuq_ref[...]) * MLA_SCORE_SCALE
    kv = _dot_bf16(ckv, wukv_ref[...])
    for hh in range(MLA_HEADS):
        a = hh * MLA_HEAD_PAD
        b = a + V7X_LANES
        c = b + V7X_LANES
        q_ref[:, a:b] = q[:, a:b].astype(BF16)
        q_ref[:, b:c] = _rope_mla(q[:, b:c], cm, sup, sdn).astype(BF16)
        k_ref[:, a:b] = kv[:, a:b].astype(BF16)
        k_ref[:, b:c] = kpe
        v_ref[:, hh * MLA_V_DIM:(hh + 1) * MLA_V_DIM] = kv[:, b:c].astype(BF16)

    cd, sd = cd_ref[...], sd_ref[...]
    pdq = _dot_bf16(h, wdq_ref[...]) * DIFF_SCORE_SCALE
    for c in range(DIFF_QK_W // DIFF_HEAD_DIM):
        cols = slice(c * DIFF_HEAD_DIM, (c + 1) * DIFF_HEAD_DIM)
        dq_ref[:, cols] = _rope_half(pdq[:, cols], cd, sd).astype(BF16)
    pdk = _dot_bf16(h, wdk_ref[...])
    for c in range(DIFF_QK_W // DIFF_HEAD_DIM):
        cols = slice(c * DIFF_HEAD_DIM, (c + 1) * DIFF_HEAD_DIM)
        dk_ref[:, cols] = _rope_half(pdk[:, cols], cd, sd).astype(BF16)
    dv_ref[...] = _dot_bf16(h, wdv_ref[...]).astype(BF16)


def _mix_proj(x, g, w_in, qg, wuq_p, kvg, wukv, tables, seq):
    t, d = x.shape
    tm = PROJ_TM
    pos_blocks = seq // tm
    const = lambda i: (0, 0)
    row = lambda i: (i, 0)
    pos = lambda i: (i % pos_blocks, 0)
    wq = MLA_HEADS * MLA_HEAD_PAD
    out_w = (wq, wq, MLA_OUT, DIFF_QK_W, DIFF_QK_W, DIFF_OUT)
    w_a = w_in[:, :_IN_A].astype(BF16)
    w_dq = w_in[:, _IN_DQ:_IN_DK].astype(BF16)
    w_dk = w_in[:, _IN_DK:_IN_DV].astype(BF16)
    w_dv = w_in[:, _IN_DV:_IN_W].astype(BF16)
    once = dict(pipeline_mode=pl.Buffered(1))
    nbytes = (d * (_IN_A + 3 * DIFF_QK_W) * 2 + MLA_Q_RANK * wq * (2 + 4) + 2 * tm * d * 4
              + 2 * tm * sum(out_w) * 2 + tm * _IN_W * 4 + 2 * tm * wq * 4)
    return pl.pallas_call(
        _proj_kernel,
        grid=(t // tm,),
        in_specs=[
            pl.BlockSpec((tm, d), row),
            pl.BlockSpec((1, d), const),
            pl.BlockSpec((d, _IN_A), const, **once),
            pl.BlockSpec((d, DIFF_QK_W), const, **once),
            pl.BlockSpec((d, DIFF_QK_W), const, **once),
            pl.BlockSpec((d, DIFF_OUT), const, **once),
            pl.BlockSpec((1, MLA_Q_RANK), const),
            pl.BlockSpec((MLA_Q_RANK, wq), const, **once),
            pl.BlockSpec((1, MLA_KV_RANK), const),
            pl.BlockSpec((MLA_KV_RANK, wq), const, **once),
        ] + [pl.BlockSpec((tm, V7X_LANES), pos)] * 5,
        out_specs=[pl.BlockSpec((tm, w), row) for w in out_w],
        out_shape=[jax.ShapeDtypeStruct((t, w), BF16) for w in out_w],
        compiler_params=pltpu.CompilerParams(
            dimension_semantics=("parallel",),
            vmem_limit_bytes=_vmem_limit(nbytes)),
        name="mix_proj",
    )(x, g, w_a, w_dq, w_dk, w_dv, qg, wuq_p, kvg, wukv, *tables)


def _causal_mask(n_q, n_k):
    r = lax.broadcasted_iota(jnp.int32, (n_q, n_k), 0) // CHUNK
    c = lax.broadcasted_iota(jnp.int32, (n_q, n_k), 1) // CHUNK
    return r >= c


def _flash_scratch(n_maps, v_dim):
    stat = pltpu.VMEM((ATT_TQ, V7X_LANES), F32)
    return [stat, stat, pltpu.VMEM((ATT_TQ, v_dim), F32)] * n_maps


def _flash_init(m_ref, l_ref, acc_ref):
    m_ref[...] = jnp.full(m_ref.shape, -jnp.inf, F32)
    l_ref[...] = jnp.zeros(l_ref.shape, F32)
    acc_ref[...] = jnp.zeros(acc_ref.shape, F32)


def _flash_step(t, v, m_ref, l_ref, acc_ref):
    lanes = m_ref.shape[1]
    m_old = m_ref[...]
    m_new = jnp.maximum(m_old, jnp.max(t, axis=-1, keepdims=True))
    alpha = jnp.exp2(m_old - m_new)
    p = jnp.exp2(t - jnp.tile(m_new, (1, t.shape[1] // lanes)))
    l_ref[...] = alpha * l_ref[...] + jnp.sum(p, axis=-1, keepdims=True)
    pv = jnp.dot(p.astype(BF16), v, preferred_element_type=F32)
    acc_ref[...] = jnp.tile(alpha, (1, acc_ref.shape[1] // lanes)) * acc_ref[...] + pv
    m_ref[...] = m_new


def _attn_tile(qi, n_maps, scores, values, state):
    half = ATT_TQ // 2
    assert ATT_TQ == ATT_TK and half % CHUNK == 0

    def update(q_rows, k_rows, mask):
        for n in range(n_maps):
            t = scores(n, q_rows, k_rows)
            if mask is not None:
                t = jnp.where(mask, t, -jnp.inf)
            _flash_step(t, values(n, k_rows), *[r.at[q_rows] for r in state[3 * n:3 * n + 3]])

    for n in range(n_maps):
        _flash_init(*state[3 * n:3 * n + 3])
    base = pl.multiple_of(qi * ATT_TQ, ATT_TQ)
    update(pl.ds(0, ATT_TQ), pl.ds(base, half), _causal_mask(ATT_TQ, half))
    update(pl.ds(half, half), pl.ds(pl.multiple_of(base + half, half), half),
           _causal_mask(half, half))

    def body(j, carry):
        update(pl.ds(0, ATT_TQ), pl.ds(pl.multiple_of(j * ATT_TK, ATT_TK), ATT_TK), None)
        return carry

    lax.fori_loop(0, qi, body, 0)


def _qk(q, k):
    return lax.dot_general(q, k, (((1,), (1,)), ((), ())), preferred_element_type=F32)


def _mla_attn_kernel(q_ref, k_ref, v_ref, o_ref, *state):
    wq, wv = MLA_HEAD_PAD, MLA_V_DIM

    def scores(g, q_rows, k_rows):
        return _qk(q_ref[q_rows, g * wq:(g + 1) * wq], k_ref[k_rows, g * wq:(g + 1) * wq])

    def values(g, k_rows):
        return v_ref[k_rows, g * wv:(g + 1) * wv]

    _attn_tile(pl.program_id(2), MLA_GROUP, scores, values, state)
    for g in range(MLA_GROUP):
        _, l_ref, acc_ref = state[3 * g:3 * g + 3]
        o_ref[:, g * wv:(g + 1) * wv] = (acc_ref[...] / l_ref[...]).astype(o_ref.dtype)


def _mla_attn(q, k, v, batch, seq):
    nq = seq // ATT_TQ
    wq, wv = MLA_GROUP * MLA_HEAD_PAD, MLA_GROUP * MLA_V_DIM
    return pl.pallas_call(
        _mla_attn_kernel,
        grid=(batch, MLA_HEADS // MLA_GROUP, nq),
        in_specs=[
            pl.BlockSpec((ATT_TQ, wq), lambda b, h, i: (b * nq + i, h)),
            pl.BlockSpec((seq, wq), lambda b, h, i: (b, h)),
            pl.BlockSpec((seq, wv), lambda b, h, i: (b, h)),
        ],
        out_specs=pl.BlockSpec((ATT_TQ, wv), lambda b, h, i: (b * nq + i, h)),
        out_shape=jax.ShapeDtypeStruct((batch * seq, MLA_OUT), BF16),
        scratch_shapes=_flash_scratch(MLA_GROUP, MLA_V_DIM),
        compiler_params=pltpu.CompilerParams(
            dimension_semantics=("parallel", "parallel", "parallel"),
            vmem_limit_bytes=_vmem_limit(2 * (seq + ATT_TQ) * (wq + wv) * 2
                                         + 8 * MLA_GROUP * ATT_TQ * ATT_TK * 4)),
        name="mla_attn",
    )(q, k, v)


def _diff_attn_kernel(q_ref, k_ref, v_ref, lq1_ref, lk1_ref, lq2_ref, lk2_ref, g_ref, o_ref,
                      *state):
    d, wv = DIFF_HEAD_DIM, DIFF_V_DIM

    def scores(n, q_rows, k_rows):
        return _qk(q_ref[q_rows, n * d:(n + 1) * d], k_ref[k_rows, n * d:(n + 1) * d])

    def values(n, k_rows):
        return v_ref[k_rows, (n // 2) * wv:(n // 2 + 1) * wv]

    _attn_tile(pl.program_id(2), 2 * DIFF_GROUP, scores, values, state)
    lam = (jnp.exp(jnp.sum(lq1_ref[...] * lk1_ref[...], axis=-1, keepdims=True))
           - jnp.exp(jnp.sum(lq2_ref[...] * lk2_ref[...], axis=-1, keepdims=True))
           + LAMBDA_INIT)
    rep = wv // V7X_LANES
    for h in range(DIFF_GROUP):
        _, l1_ref, a1_ref, _, l2_ref, a2_ref = state[6 * h:6 * h + 6]
        o = (a1_ref[...] / jnp.tile(l1_ref[...], (1, rep))
             - lam * (a2_ref[...] / jnp.tile(l2_ref[...], (1, rep))))
        o_ref[:, h * wv:(h + 1) * wv] = (
            _rms(o, g_ref[...]) * (1.0 - LAMBDA_INIT)).astype(o_ref.dtype)


def _diff_attn(q, k, v, lq1, lk1, lq2, lk2, g, batch, seq):
    nq = seq // ATT_TQ
    w = DIFF_GROUP * DIFF_V_DIM
    vec = pl.BlockSpec((1, DIFF_HEAD_DIM), lambda b, h, i: (0, 0))
    return pl.pallas_call(
        _diff_attn_kernel,
        grid=(batch, DIFF_HEADS // DIFF_GROUP, nq),
        in_specs=[
            pl.BlockSpec((ATT_TQ, w), lambda b, h, i: (b * nq + i, h)),
            pl.BlockSpec((seq, w), lambda b, h, i: (b, h)),
            pl.BlockSpec((seq, w), lambda b, h, i: (b, h)),
            vec, vec, vec, vec,
            pl.BlockSpec((1, DIFF_V_DIM), lambda b, h, i: (0, 0)),
        ],
        out_specs=pl.BlockSpec((ATT_TQ, w), lambda b, h, i: (b * nq + i, h)),
        out_shape=jax.ShapeDtypeStruct((batch * seq, DIFF_OUT), BF16),
        scratch_shapes=_flash_scratch(2 * DIFF_GROUP, DIFF_V_DIM),
        compiler_params=pltpu.CompilerParams(
            dimension_semantics=("parallel", "parallel", "parallel"),
            vmem_limit_bytes=_vmem_limit(2 * (2 * seq + 2 * ATT_TQ) * w * 2
                                         + 16 * DIFF_GROUP * ATT_TQ * ATT_TK * 4)),
        name="diff_attn",
    )(q, k, v, lq1, lk1, lq2, lk2, g)


def _out_kernel(x_ref, oa_ref, ob_ref, wa_ref, wb_ref, g_ref, o_ref):
    o = _dot_bf16(oa_ref[...], wa_ref[...]) + _dot_bf16(ob_ref[...], wb_ref[...])
    o_ref[...] = x_ref[...] + _rms(o, g_ref[...])


def _out_proj(x, oa, ob, w, g):
    t, d = x.shape
    assert MLA_OUT == DIFF_OUT, "the two row blocks of w share one block shape"
    tm = OUT_TM
    const = lambda i: (0, 0)
    row = lambda i: (i, 0)
    once = dict(pipeline_mode=pl.Buffered(1))
    nbytes = (d * d * (w.dtype.itemsize + 2) + 4 * tm * d * 4 + 2 * tm * d * 2 + 2 * tm * d * 4)
    return pl.pallas_call(
        _out_kernel,
        grid=(t // tm,),
        in_specs=[
            pl.BlockSpec((tm, d), row),
            pl.BlockSpec((tm, MLA_OUT), row),
            pl.BlockSpec((tm, DIFF_OUT), row),
            pl.BlockSpec((MLA_OUT, d), const, **once),
            pl.BlockSpec((DIFF_OUT, d), lambda i: (1, 0), **once),
            pl.BlockSpec((1, d), const),
        ],
        out_specs=pl.BlockSpec((tm, d), row),
        out_shape=jax.ShapeDtypeStruct((t, d), F32),
        compiler_params=pltpu.CompilerParams(
            dimension_semantics=("parallel",),
            vmem_limit_bytes=_vmem_limit(nbytes)),
        name="out_proj",
    )(x, oa, ob, w, w, g)


def _pack_w_uq(w_uq):
    r = w_uq.shape[0]
    w = w_uq.reshape(r, MLA_HEADS, MLA_QK_DIM)
    pad = jnp.zeros((r, MLA_HEADS, MLA_HEAD_PAD - MLA_QK_DIM), w_uq.dtype)
    return jnp.concatenate([w, pad], axis=-1).reshape(r, MLA_HEADS * MLA_HEAD_PAD).astype(BF16)


def _rope_tables(seq):
    pos = jnp.arange(seq, dtype=F32)[:, None]

    def angles(d):
        inv_freq = ROPE_THETA ** (-jnp.arange(0, d, 2, dtype=F32) / d)
        return pos * inv_freq[None, :]

    ang = angles(MLA_ROPE_DIM)
    zeros = jnp.zeros((seq, V7X_LANES - MLA_ROPE_DIM), F32)
    half0 = jnp.zeros_like(ang)
    cos_m = jnp.concatenate([jnp.cos(ang), jnp.cos(ang), zeros], axis=1)
    sin_up = jnp.concatenate([half0, jnp.sin(ang), zeros], axis=1)
    sin_dn = jnp.concatenate([-jnp.sin(ang), half0, zeros], axis=1)
    ang = angles(DIFF_HEAD_DIM)
    cos_d = jnp.concatenate([jnp.cos(ang), jnp.cos(ang)], axis=1)
    sin_d = jnp.concatenate([-jnp.sin(ang), jnp.sin(ang)], axis=1)
    return cos_m, sin_up, sin_dn, cos_d, sin_d


def kernel(x, ffn1_pre_g, ffn1_w_gate, ffn1_w_up, ffn1_w_down, ffn1_post_g, mix_pre_g, w_in, mla_q_norm_g, mla_w_uq, mla_kv_norm_g, mla_w_ukv, diff_lambda_q1, diff_lambda_k1, diff_lambda_q2, diff_lambda_k2, diff_subln_g, w_out, mix_post_g, ffn2_pre_g, ffn2_w_gate, ffn2_w_up, ffn2_w_down, ffn2_post_g):
    batch, seq, d = x.shape
    depth = ffn1_pre_g.shape[0]
    assert depth == 1 and d == D_MODEL
    assert seq % ATT_TQ == 0 and seq % PROJ_TM == 0 and (batch * seq) % FFN_TM == 0
    tables = _rope_tables(seq)
    xt = x.reshape(batch * seq, d)
    for l in range(depth):
        xt = _ffn(xt, ffn1_pre_g[l][None], ffn1_w_gate[l], ffn1_w_up[l], ffn1_w_down[l],
                  ffn1_post_g[l][None])
        q, k, v, dq, dk, dv = _mix_proj(
            xt, mix_pre_g[l][None], w_in[l], mla_q_norm_g[l][None],
            _pack_w_uq(mla_w_uq[l]), mla_kv_norm_g[l][None], mla_w_ukv[l],
            tables, seq)
        o_mla = _mla_attn(q, k, v, batch, seq)
        o_diff = _diff_attn(dq, dk, dv, diff_lambda_q1[l][None], diff_lambda_k1[l][None],
                            diff_lambda_q2[l][None], diff_lambda_k2[l][None],
                            diff_subln_g[l][None], batch, seq)
        xt = _out_proj(xt, o_mla, o_diff, w_out[l], mix_post_g[l][None])
        xt = _ffn(xt, ffn2_pre_g[l][None], ffn2_w_gate[l], ffn2_w_up[l], ffn2_w_down[l],
                  ffn2_post_g[l][None])
    return xt.reshape(batch, seq, d)
```

```python
import math

import jax
import jax.numpy as jnp
from jax import lax
from jax.experimental import pallas as pl
from jax.experimental.pallas import tpu as pltpu

D_MODEL = 2048
CHUNK = 64
ROPE_THETA = 10000.0
NORM_EPS = 1e-6
MLA_HEADS = 8
MLA_Q_RANK = 512
MLA_KV_RANK = 512
MLA_NOPE_DIM = 128
MLA_ROPE_DIM = 64
MLA_V_DIM = 128
MLA_QK_DIM = MLA_NOPE_DIM + MLA_ROPE_DIM
DIFF_HEADS = 4
DIFF_HEAD_DIM = 128
DIFF_V_DIM = 2 * DIFF_HEAD_DIM
MLA_OUT = MLA_HEADS * MLA_V_DIM
DIFF_OUT = DIFF_HEADS * DIFF_V_DIM
DIFF_QK_W = DIFF_HEADS * 2 * DIFF_HEAD_DIM
D_FF = 5632
LAMBDA_INIT = 0.8 - 0.6 * math.exp(-0.3 * 0)
LOG2_E = math.log2(math.e)
MLA_SCORE_SCALE = MLA_QK_DIM ** -0.5 * LOG2_E
DIFF_SCORE_SCALE = DIFF_HEAD_DIM ** -0.5 * LOG2_E

V7X_LANES = 128
V7X_VMEM_BYTES = 64 * 1024 * 1024
MLA_HEAD_PAD = 2 * V7X_LANES

FFN_TM = 1024
FFN_TF = 512
FFN_ROWS = 256
PROJ_TM = 256
ATT_TQ = 512
ATT_TK = 512
MLA_GROUP = 8
DIFF_GROUP = 4
OUT_TM = 512

BF16 = jnp.bfloat16
F32 = jnp.float32


def _vmem_limit(nbytes):
    return int(min(V7X_VMEM_BYTES - 4 * 1024 * 1024, max(32 * 1024 * 1024, nbytes * 3 // 2)))


def _dot_bf16(a, w):
    return lax.dot_general(a, w, (((1,), (0,)), ((), ())), preferred_element_type=F32)


def _rms(v, g):
    ms = jnp.mean(v * v, axis=-1, keepdims=True)
    return v * lax.rsqrt(ms + NORM_EPS) * g


def _swiglu_hidden(h, wg, wu):
    g = _dot_bf16(h, wg)
    u = _dot_bf16(h, wu)
    return (g * (1.0 / (1.0 + jnp.exp(-g))) * u).astype(BF16)


def _ffn_kernel(x_ref, pre_g_ref, wg_ref, wu_ref, wd_ref, post_g_ref, o_ref, h_ref):
    f = pl.program_id(1)
    last = pl.num_programs(1) - 1
    chunks = [pl.ds(c * FFN_ROWS, FFN_ROWS) for c in range(FFN_TM // FFN_ROWS)]

    @pl.when(f == 0)
    def _():
        for rows in chunks:
            h = _rms(x_ref[rows, :], pre_g_ref[...]).astype(BF16)
            h_ref[rows, :] = h
            o_ref[rows, :] = _dot_bf16(_swiglu_hidden(h, wg_ref[...], wu_ref[...]), wd_ref[...])

    @pl.when(jnp.logical_and(f > 0, f < last))
    def _():
        a = _swiglu_hidden(h_ref[...], wg_ref[...], wu_ref[...])
        o_ref[...] += _dot_bf16(a, wd_ref[...])

    @pl.when(f == last)
    def _():
        for rows in chunks:
            a = _swiglu_hidden(h_ref[rows, :], wg_ref[...], wu_ref[...])
            acc = o_ref[rows, :] + _dot_bf16(a, wd_ref[...])
            o_ref[rows, :] = x_ref[rows, :] + 0.5 * _rms(acc, post_g_ref[...])


def _ffn(x, pre_g, wg, wu, wd, post_g):
    t, d = x.shape
    assert D_FF // FFN_TF >= 2, "first and last hidden steps must be distinct"
    nbytes = (2 * FFN_TM * d * 4 + FFN_TM * d * 2 + 2 * 3 * d * FFN_TF * wg.dtype.itemsize
              + 3 * FFN_TM * FFN_TF * 4)
    return pl.pallas_call(
        _ffn_kernel,
        grid=(t // FFN_TM, D_FF // FFN_TF),
        in_specs=[
            pl.BlockSpec((FFN_TM, d), lambda i, f: (i, 0)),
            pl.BlockSpec((1, d), lambda i, f: (0, 0)),
            pl.BlockSpec((d, FFN_TF), lambda i, f: (0, f)),
            pl.BlockSpec((d, FFN_TF), lambda i, f: (0, f)),
            pl.BlockSpec((FFN_TF, d), lambda i, f: (f, 0)),
            pl.BlockSpec((1, d), lambda i, f: (0, 0)),
        ],
        out_specs=pl.BlockSpec((FFN_TM, d), lambda i, f: (i, 0), pipeline_mode=pl.Buffered(1)),
        out_shape=jax.ShapeDtypeStruct((t, d), F32),
        scratch_shapes=[pltpu.VMEM((FFN_TM, d), BF16)],
        compiler_params=pltpu.CompilerParams(
            dimension_semantics=("parallel", "arbitrary"),
            vmem_limit_bytes=_vmem_limit(nbytes)),
        name="ffn",
    )(x, pre_g, wg, wu, wd, post_g)


_IN_LATENT = MLA_Q_RANK + MLA_KV_RANK
_IN_W = _IN_LATENT + MLA_ROPE_DIM + 2 * DIFF_QK_W + DIFF_OUT


def _rope_half(v, cos, sin_signed):
    return v * cos + pltpu.roll(v, DIFF_HEAD_DIM // 2, 1) * sin_signed


def _rope_mla(v, cos, sin_up, sin_dn):
    half = MLA_ROPE_DIM // 2
    return (v * cos + pltpu.roll(v, half, 1) * sin_up
            + pltpu.roll(v, V7X_LANES - half, 1) * sin_dn)


def _proj_kernel(x_ref, g_ref, win_ref, qg_ref, wuq_ref, kvg_ref, wukv_ref,
                 cm_ref, sup_ref, sdn_ref, cd_ref, sd_ref,
                 q_ref, k_ref, v_ref, dq_ref, dk_ref, dv_ref):
    h = _rms(x_ref[...], g_ref[...]).astype(BF16)

    pa = _dot_bf16(h, win_ref[:, 0:_IN_LATENT])
    cq = _rms(pa[:, 0:MLA_Q_RANK], qg_ref[...]).astype(BF16)
    ckv = _rms(pa[:, MLA_Q_RANK:_IN_LATENT], kvg_ref[...]).astype(BF16)
    rest = _dot_bf16(h, win_ref[:, _IN_LATENT:_IN_W])
    cm, sup, sdn = cm_ref[...], sup_ref[...], sdn_ref[...]
    kpe = _rope_mla(rest[:, 0:V7X_LANES], cm, sup, sdn).astype(BF16)

    q = _dot_bf16(cq, wuq_ref[...]) * MLA_SCORE_SCALE
    kv = _dot_bf16(ckv, wukv_ref[...])
    for hh in range(MLA_HEADS):
        a = hh * MLA_HEAD_PAD
        b = a + V7X_LANES
        c = b + V7X_LANES
        q_ref[:, a:b] = q[:, a:b].astype(BF16)
        q_ref[:, b:c] = _rope_mla(q[:, b:c], cm, sup, sdn).astype(BF16)
        k_ref[:, a:b] = kv[:, a:b].astype(BF16)
        k_ref[:, b:c] = kpe
        v_ref[:, hh * MLA_V_DIM:(hh + 1) * MLA_V_DIM] = kv[:, b:c].astype(BF16)

    cd, sd = cd_ref[...], sd_ref[...]
    off = MLA_ROPE_DIM
    pdq = rest[:, off:off + DIFF_QK_W] * DIFF_SCORE_SCALE
    for c in range(DIFF_QK_W // DIFF_HEAD_DIM):
        cols = slice(c * DIFF_HEAD_DIM, (c + 1) * DIFF_HEAD_DIM)
        dq_ref[:, cols] = _rope_half(pdq[:, cols], cd, sd).astype(BF16)
    pdk = rest[:, off + DIFF_QK_W:off + 2 * DIFF_QK_W]
    for c in range(DIFF_QK_W // DIFF_HEAD_DIM):
        cols = slice(c * DIFF_HEAD_DIM, (c + 1) * DIFF_HEAD_DIM)
        dk_ref[:, cols] = _rope_half(pdk[:, cols], cd, sd).astype(BF16)
    dv_ref[...] = rest[:, off + 2 * DIFF_QK_W:].astype(BF16)


def _mix_proj(x, g, w_in, qg, wuq_p, kvg, wukv, tables, seq):
    t, d = x.shape
    tm = PROJ_TM
    pos_blocks = seq // tm
    const = lambda i: (0, 0)
    row = lambda i: (i, 0)
    pos = lambda i: (i % pos_blocks, 0)
    wq = MLA_HEADS * MLA_HEAD_PAD
    out_w = (wq, wq, MLA_OUT, DIFF_QK_W, DIFF_QK_W, DIFF_OUT)
    once = dict(pipeline_mode=pl.Buffered(1))
    nbytes = (d * _IN_W * 2 + MLA_Q_RANK * wq * (2 + 4) + 2 * tm * d * 4
              + 2 * tm * sum(out_w) * 2 + tm * _IN_W * 4 + 2 * tm * wq * 4)
    return pl.pallas_call(
        _proj_kernel,
        grid=(t // tm,),
        in_specs=[
            pl.BlockSpec((tm, d), row),
            pl.BlockSpec((1, d), const),
            pl.BlockSpec((d, _IN_W), const, **once),
            pl.BlockSpec((1, MLA_Q_RANK), const),
            pl.BlockSpec((MLA_Q_RANK, wq), const, **once),
            pl.BlockSpec((1, MLA_KV_RANK), const),
            pl.BlockSpec((MLA_KV_RANK, wq), const, **once),
        ] + [pl.BlockSpec((tm, V7X_LANES), pos)] * 5,
        out_specs=[pl.BlockSpec((tm, w), row) for w in out_w],
        out_shape=[jax.ShapeDtypeStruct((t, w), BF16) for w in out_w],
        compiler_params=pltpu.CompilerParams(
            dimension_semantics=("parallel",),
            vmem_limit_bytes=_vmem_limit(nbytes)),
        name="mix_proj",
    )(x, g, w_in.astype(BF16), qg, wuq_p, kvg, wukv, *tables)


def _causal_mask(n_q, n_k):
    r = lax.broadcasted_iota(jnp.int32, (n_q, n_k), 0) // CHUNK
    c = lax.broadcasted_iota(jnp.int32, (n_q, n_k), 1) // CHUNK
    return r >= c


def _flash_scratch(n_maps, v_dim):
    stat = pltpu.VMEM((ATT_TQ, V7X_LANES), F32)
    return [stat, stat, pltpu.VMEM((ATT_TQ, v_dim), F32)] * n_maps


def _flash_init(m_ref, l_ref, acc_ref):
    m_ref[...] = jnp.full(m_ref.shape, -jnp.inf, F32)
    l_ref[...] = jnp.zeros(l_ref.shape, F32)
    acc_ref[...] = jnp.zeros(acc_ref.shape, F32)


def _flash_step(t, v, m_ref, l_ref, acc_ref):
    lanes = m_ref.shape[1]
    m_old = m_ref[...]
    m_new = jnp.maximum(m_old, jnp.max(t, axis=-1, keepdims=True))
    alpha = jnp.exp2(m_old - m_new)
    p = jnp.exp2(t - jnp.tile(m_new, (1, t.shape[1] // lanes)))
    l_ref[...] = alpha * l_ref[...] + jnp.sum(p, axis=-1, keepdims=True)
    pv = jnp.dot(p.astype(BF16), v, preferred_element_type=F32)
    acc_ref[...] = jnp.tile(alpha, (1, acc_ref.shape[1] // lanes)) * acc_ref[...] + pv
    m_ref[...] = m_new


def _attn_tile(qi, n_maps, scores, values, state):
    half = ATT_TQ // 2
    assert ATT_TQ == ATT_TK and half % CHUNK == 0

    def update(q_rows, k_rows, mask):
        for n in range(n_maps):
            t = scores(n, q_rows, k_rows)
            if mask is not None:
                t = jnp.where(mask, t, -jnp.inf)
            _flash_step(t, values(n, k_rows), *[r.at[q_rows] for r in state[3 * n:3 * n + 3]])

    for n in range(n_maps):
        _flash_init(*state[3 * n:3 * n + 3])
    base = pl.multiple_of(qi * ATT_TQ, ATT_TQ)
    update(pl.ds(0, ATT_TQ), pl.ds(base, half), _causal_mask(ATT_TQ, half))
    update(pl.ds(half, half), pl.ds(pl.multiple_of(base + half, half), half),
           _causal_mask(half, half))

    def body(j, carry):
        update(pl.ds(0, ATT_TQ), pl.ds(pl.multiple_of(j * ATT_TK, ATT_TK), ATT_TK), None)
        return carry

    lax.fori_loop(0, qi, body, 0)


def _qk(q, k):
    return lax.dot_general(q, k, (((1,), (1,)), ((), ())), preferred_element_type=F32)


def _mla_attn_kernel(q_ref, k_ref, v_ref, o_ref, *state):
    wq, wv = MLA_HEAD_PAD, MLA_V_DIM

    def scores(g, q_rows, k_rows):
        return _qk(q_ref[q_rows, g * wq:(g + 1) * wq], k_ref[k_rows, g * wq:(g + 1) * wq])

    def values(g, k_rows):
        return v_ref[k_rows, g * wv:(g + 1) * wv]

    _attn_tile(pl.program_id(2), MLA_GROUP, scores, values, state)
    for g in range(MLA_GROUP):
        _, l_ref, acc_ref = state[3 * g:3 * g + 3]
        o_ref[:, g * wv:(g + 1) * wv] = (acc_ref[...] / l_ref[...]).astype(o_ref.dtype)


def _mla_attn(q, k, v, batch, seq):
    nq = seq // ATT_TQ
    wq, wv = MLA_GROUP * MLA_HEAD_PAD, MLA_GROUP * MLA_V_DIM
    return pl.pallas_call(
        _mla_attn_kernel,
        grid=(batch, MLA_HEADS // MLA_GROUP, nq),
        in_specs=[
            pl.BlockSpec((ATT_TQ, wq), lambda b, h, i: (b * nq + i, h)),
            pl.BlockSpec((seq, wq), lambda b, h, i: (b, h)),
            pl.BlockSpec((seq, wv), lambda b, h, i: (b, h)),
        ],
        out_specs=pl.BlockSpec((ATT_TQ, wv), lambda b, h, i: (b * nq + i, h)),
        out_shape=jax.ShapeDtypeStruct((batch * seq, MLA_OUT), BF16),
        scratch_shapes=_flash_scratch(MLA_GROUP, MLA_V_DIM),
        compiler_params=pltpu.CompilerParams(
            dimension_semantics=("parallel", "parallel", "parallel"),
            vmem_limit_bytes=_vmem_limit(2 * (seq + ATT_TQ) * (wq + wv) * 2
                                         + 8 * MLA_GROUP * ATT_TQ * ATT_TK * 4)),
        name="mla_attn",
    )(q, k, v)


def _diff_attn_kernel(q_ref, k_ref, v_ref, lq1_ref, lk1_ref, lq2_ref, lk2_ref, g_ref, o_ref,
                      *state):
    d, wv = DIFF_HEAD_DIM, DIFF_V_DIM

    def scores(n, q_rows, k_rows):
        return _qk(q_ref[q_rows, n * d:(n + 1) * d], k_ref[k_rows, n * d:(n + 1) * d])

    def values(n, k_rows):
        return v_ref[k_rows, (n // 2) * wv:(n // 2 + 1) * wv]

    _attn_tile(pl.program_id(2), 2 * DIFF_GROUP, scores, values, state)
    lam = (jnp.exp(jnp.sum(lq1_ref[...] * lk1_ref[...], axis=-1, keepdims=True))
           - jnp.exp(jnp.sum(lq2_ref[...] * lk2_ref[...], axis=-1, keepdims=True))
           + LAMBDA_INIT)
    rep = wv // V7X_LANES
    for h in range(DIFF_GROUP):
        _, l1_ref, a1_ref, _, l2_ref, a2_ref = state[6 * h:6 * h + 6]
        o = (a1_ref[...] / jnp.tile(l1_ref[...], (1, rep))
             - lam * (a2_ref[...] / jnp.tile(l2_ref[...], (1, rep))))
        o_ref[:, h * wv:(h + 1) * wv] = (
            _rms(o, g_ref[...]) * (1.0 - LAMBDA_INIT)).astype(o_ref.dtype)


def _diff_attn(q, k, v, lq1, lk1, lq2, lk2, g, batch, seq):
    nq = seq // ATT_TQ
    w = DIFF_GROUP * DIFF_V_DIM
    vec = pl.BlockSpec((1, DIFF_HEAD_DIM), lambda b, h, i: (0, 0))
    return pl.pallas_call(
        _diff_attn_kernel,
        grid=(batch, DIFF_HEADS // DIFF_GROUP, nq),
        in_specs=[
            pl.BlockSpec((ATT_TQ, w), lambda b, h, i: (b * nq + i, h)),
            pl.BlockSpec((seq, w), lambda b, h, i: (b, h)),
            pl.BlockSpec((seq, w), lambda b, h, i: (b, h)),
            vec, vec, vec, vec,
            pl.BlockSpec((1, DIFF_V_DIM), lambda b, h, i: (0, 0)),
        ],
        out_specs=pl.BlockSpec((ATT_TQ, w), lambda b, h, i: (b * nq + i, h)),
        out_shape=jax.ShapeDtypeStruct((batch * seq, DIFF_OUT), BF16),
        scratch_shapes=_flash_scratch(2 * DIFF_GROUP, DIFF_V_DIM),
        compiler_params=pltpu.CompilerParams(
            dimension_semantics=("parallel", "parallel", "parallel"),
            vmem_limit_bytes=_vmem_limit(2 * (2 * seq + 2 * ATT_TQ) * w * 2
                                         + 16 * DIFF_GROUP * ATT_TQ * ATT_TK * 4)),
        name="diff_attn",
    )(q, k, v, lq1, lk1, lq2, lk2, g)


def _out_kernel(x_ref, oa_ref, ob_ref, wa_ref, wb_ref, g_ref, o_ref):
    o = _dot_bf16(oa_ref[...], wa_ref[...]) + _dot_bf16(ob_ref[...], wb_ref[...])
    o_ref[...] = x_ref[...] + _rms(o, g_ref[...])


def _out_proj(x, oa, ob, w, g):
    t, d = x.shape
    assert MLA_OUT == DIFF_OUT, "the two row blocks of w share one block shape"
    tm = OUT_TM
    const = lambda i: (0, 0)
    row = lambda i: (i, 0)
    once = dict(pipeline_mode=pl.Buffered(1))
    nbytes = (d * d * (w.dtype.itemsize + 2) + 4 * tm * d * 4 + 2 * tm * d * 2 + 2 * tm * d * 4)
    return pl.pallas_call(
        _out_kernel,
        grid=(t // tm,),
        in_specs=[
            pl.BlockSpec((tm, d), row),
            pl.BlockSpec((tm, MLA_OUT), row),
            pl.BlockSpec((tm, DIFF_OUT), row),
            pl.BlockSpec((MLA_OUT, d), const, **once),
            pl.BlockSpec((DIFF_OUT, d), lambda i: (1, 0), **once),
            pl.BlockSpec((1, d), const),
        ],
        out_specs=pl.BlockSpec((tm, d), row),
        out_shape=jax.ShapeDtypeStruct((t, d), F32),
        compiler_params=pltpu.CompilerParams(
            dimension_semantics=("parallel",),
            vmem_limit_bytes=_vmem_limit(nbytes)),
        name="out_proj",
    )(x, oa, ob, w, w, g)


def _pack_w_uq(w_uq):
    r = w_uq.shape[0]
    w = w_uq.reshape(r, MLA_HEADS, MLA_QK_DIM)
    pad = jnp.zeros((r, MLA_HEADS, MLA_HEAD_PAD - MLA_QK_DIM), w_uq.dtype)
    return jnp.concatenate([w, pad], axis=-1).reshape(r, MLA_HEADS * MLA_HEAD_PAD).astype(BF16)


def _rope_tables(seq):
    pos = jnp.arange(seq, dtype=F32)[:, None]

    def angles(d):
        inv_freq = ROPE_THETA ** (-jnp.arange(0, d, 2, dtype=F32) / d)
        return pos * inv_freq[None, :]

    ang = angles(MLA_ROPE_DIM)
    zeros = jnp.zeros((seq, V7X_LANES - MLA_ROPE_DIM), F32)
    half0 = jnp.zeros_like(ang)
    cos_m = jnp.concatenate([jnp.cos(ang), jnp.cos(ang), zeros], axis=1)
    sin_up = jnp.concatenate([half0, jnp.sin(ang), zeros], axis=1)
    sin_dn = jnp.concatenate([-jnp.sin(ang), half0, zeros], axis=1)
    ang = angles(DIFF_HEAD_DIM)
    cos_d = jnp.concatenate([jnp.cos(ang), jnp.cos(ang)], axis=1)
    sin_d = jnp.concatenate([-jnp.sin(ang), jnp.sin(ang)], axis=1)
    return cos_m, sin_up, sin_dn, cos_d, sin_d


def kernel(x, ffn1_pre_g, ffn1_w_gate, ffn1_w_up, ffn1_w_down, ffn1_post_g, mix_pre_g, w_in, mla_q_norm_g, mla_w_uq, mla_kv_norm_g, mla_w_ukv, diff_lambda_q1, diff_lambda_k1, diff_lambda_q2, diff_lambda_k2, diff_subln_g, w_out, mix_post_g, ffn2_pre_g, ffn2_w_gate, ffn2_w_up, ffn2_w_down, ffn2_post_g):
    batch, seq, d = x.shape
    depth = ffn1_pre_g.shape[0]
    assert depth == 1 and d == D_MODEL
    assert seq % ATT_TQ == 0 and seq % PROJ_TM == 0 and (batch * seq) % FFN_TM == 0
    tables = _rope_tables(seq)
    xt = x.reshape(batch * seq, d)
    for l in range(depth):
        xt = _ffn(xt, ffn1_pre_g[l][None], ffn1_w_gate[l], ffn1_w_up[l], ffn1_w_down[l],
                  ffn1_post_g[l][None])
        q, k, v, dq, dk, dv = _mix_proj(
            xt, mix_pre_g[l][None], w_in[l], mla_q_norm_g[l][None],
            _pack_w_uq(mla_w_uq[l]), mla_kv_norm_g[l][None], mla_w_ukv[l],
            tables, seq)
        o_mla = _mla_attn(q, k, v, batch, seq)
        o_diff = _diff_attn(dq, dk, dv, diff_lambda_q1[l][None], diff_lambda_k1[l][None],
                            diff_lambda_q2[l][None], diff_lambda_k2[l][None],
                            diff_subln_g[l][None], batch, seq)
        xt = _out_proj(xt, o_mla, o_diff, w_out[l], mix_post_g[l][None])
        xt = _ffn(xt, ffn2_pre_g[l][None], ffn2_w_gate[l], ffn2_w_up[l], ffn2_w_down[l],
                  ffn2_post_g[l][None])
    return xt.reshape(batch, seq, d)
```

```python
import math

import jax
import jax.numpy as jnp
from jax import lax
from jax.experimental import pallas as pl
from jax.experimental.pallas import tpu as pltpu

D_MODEL = 2048
CHUNK = 64
ROPE_THETA = 10000.0
NORM_EPS = 1e-6
MLA_HEADS = 8
MLA_Q_RANK = 512
MLA_KV_RANK = 512
MLA_NOPE_DIM = 128
MLA_ROPE_DIM = 64
MLA_V_DIM = 128
MLA_QK_DIM = MLA_NOPE_DIM + MLA_ROPE_DIM
DIFF_HEADS = 4
DIFF_HEAD_DIM = 128
DIFF_V_DIM = 2 * DIFF_HEAD_DIM
MLA_OUT = MLA_HEADS * MLA_V_DIM
DIFF_OUT = DIFF_HEADS * DIFF_V_DIM
DIFF_QK_W = DIFF_HEADS * 2 * DIFF_HEAD_DIM
D_FF = 5632
LAMBDA_INIT = 0.8 - 0.6 * math.exp(-0.3 * 0)
LOG2_E = math.log2(math.e)
MLA_SCORE_SCALE = MLA_QK_DIM ** -0.5 * LOG2_E
DIFF_SCORE_SCALE = DIFF_HEAD_DIM ** -0.5 * LOG2_E

V7X_LANES = 128
V7X_VMEM_BYTES = 64 * 1024 * 1024
MLA_HEAD_PAD = 2 * V7X_LANES

FFN_TM = 1024
FFN_TF = 512
FFN_ROWS = 256
PROJ_TM = 256
ATT_TQ = 512
ATT_TK = 512
MLA_GROUP = 8
DIFF_GROUP = 4
OUT_TM = 512

BF16 = jnp.bfloat16
F32 = jnp.float32


def _vmem_limit(nbytes):
    return int(min(V7X_VMEM_BYTES - 4 * 1024 * 1024, max(32 * 1024 * 1024, nbytes * 3 // 2)))


def _dot_bf16(a, w):
    return lax.dot_general(a, w, (((1,), (0,)), ((), ())), preferred_element_type=F32)


def _rms(v, g):
    ms = jnp.mean(v * v, axis=-1, keepdims=True)
    return v * lax.rsqrt(ms + NORM_EPS) * g


def _swiglu_hidden(h, wg, wu):
    g = _dot_bf16(h, wg)
    u = _dot_bf16(h, wu)
    return (g * (1.0 / (1.0 + jnp.exp(-g))) * u).astype(BF16)


def _ffn_kernel(x_ref, pre_g_ref, wg_ref, wu_ref, wd_ref, post_g_ref, o_hbm, acc_ref, h_ref, sem):
    i = pl.program_id(0)
    f = pl.program_id(1)
    last = pl.num_programs(1) - 1
    n_chunks = FFN_TM // FFN_ROWS
    chunks = [pl.ds(c * FFN_ROWS, FFN_ROWS) for c in range(n_chunks)]

    def out_copy(tile, c):
        dst = o_hbm.at[pl.ds(pl.multiple_of(tile * FFN_TM + c * FFN_ROWS, FFN_ROWS), FFN_ROWS)]
        return pltpu.make_async_copy(acc_ref.at[chunks[c]], dst, sem.at[c])

    @pl.when(jnp.logical_and(f == 0, i > 0))
    def _():
        for c in range(n_chunks):
            out_copy(i - 1, c).wait()

    @pl.when(f == 0)
    def _():
        for rows in chunks:
            h = _rms(x_ref[rows, :], pre_g_ref[...]).astype(BF16)
            h_ref[rows, :] = h
            acc_ref[rows, :] = _dot_bf16(_swiglu_hidden(h, wg_ref[...], wu_ref[...]), wd_ref[...])

    @pl.when(jnp.logical_and(f > 0, f < last))
    def _():
        a = _swiglu_hidden(h_ref[...], wg_ref[...], wu_ref[...])
        acc_ref[...] += _dot_bf16(a, wd_ref[...])

    @pl.when(f == last)
    def _():
        for c, rows in enumerate(chunks):
            a = _swiglu_hidden(h_ref[rows, :], wg_ref[...], wu_ref[...])
            acc = acc_ref[rows, :] + _dot_bf16(a, wd_ref[...])
            acc_ref[rows, :] = x_ref[rows, :] + 0.5 * _rms(acc, post_g_ref[...])
            out_copy(i, c).start()

        @pl.when(i == pl.num_programs(0) - 1)
        def _():
            for c in range(n_chunks):
                out_copy(i, c).wait()


def _ffn(x, pre_g, wg, wu, wd, post_g):
    t, d = x.shape
    assert D_FF // FFN_TF >= 2, "first and last hidden steps must be distinct"
    nbytes = (3 * FFN_TM * d * 4 + FFN_TM * d * 2 + 2 * 3 * d * FFN_TF * wg.dtype.itemsize
              + 3 * FFN_TM * FFN_TF * 4)
    return pl.pallas_call(
        _ffn_kernel,
        grid=(t // FFN_TM, D_FF // FFN_TF),
        in_specs=[
            pl.BlockSpec((FFN_TM, d), lambda i, f: (i, 0)),
            pl.BlockSpec((1, d), lambda i, f: (0, 0)),
            pl.BlockSpec((d, FFN_TF), lambda i, f: (0, f)),
            pl.BlockSpec((d, FFN_TF), lambda i, f: (0, f)),
            pl.BlockSpec((FFN_TF, d), lambda i, f: (f, 0)),
            pl.BlockSpec((1, d), lambda i, f: (0, 0)),
        ],
        out_specs=pl.BlockSpec(memory_space=pl.ANY),
        out_shape=jax.ShapeDtypeStruct((t, d), F32),
        scratch_shapes=[pltpu.VMEM((FFN_TM, d), F32), pltpu.VMEM((FFN_TM, d), BF16),
                        pltpu.SemaphoreType.DMA((FFN_TM // FFN_ROWS,))],
        compiler_params=pltpu.CompilerParams(
            dimension_semantics=("arbitrary", "arbitrary"),
            vmem_limit_bytes=_vmem_limit(nbytes)),
        name="ffn",
    )(x, pre_g, wg, wu, wd, post_g)


_IN_LATENT = MLA_Q_RANK + MLA_KV_RANK
_IN_W = _IN_LATENT + MLA_ROPE_DIM + 2 * DIFF_QK_W + DIFF_OUT


def _rope_half(v, cos, sin_signed):
    return v * cos + pltpu.roll(v, DIFF_HEAD_DIM // 2, 1) * sin_signed


def _rope_mla(v, cos, sin_up, sin_dn):
    half = MLA_ROPE_DIM // 2
    return (v * cos + pltpu.roll(v, half, 1) * sin_up
            + pltpu.roll(v, V7X_LANES - half, 1) * sin_dn)


def _proj_kernel(x_ref, g_ref, win_ref, qg_ref, wuq_ref, kvg_ref, wukv_ref,
                 cm_ref, sup_ref, sdn_ref, cd_ref, sd_ref,
                 q_ref, k_ref, v_ref, dq_ref, dk_ref, dv_ref):
    h = _rms(x_ref[...], g_ref[...]).astype(BF16)

    pa = _dot_bf16(h, win_ref[:, 0:_IN_LATENT])
    cq = _rms(pa[:, 0:MLA_Q_RANK], qg_ref[...]).astype(BF16)
    ckv = _rms(pa[:, MLA_Q_RANK:_IN_LATENT], kvg_ref[...]).astype(BF16)
    rest = _dot_bf16(h, win_ref[:, _IN_LATENT:_IN_W])
    cm, sup, sdn = cm_ref[...], sup_ref[...], sdn_ref[...]
    kpe = _rope_mla(rest[:, 0:V7X_LANES], cm, sup, sdn).astype(BF16)

    q = _dot_bf16(cq, wuq_ref[...]) * MLA_SCORE_SCALE
    kv = _dot_bf16(ckv, wukv_ref[...])
    for hh in range(MLA_HEADS):
        a = hh * MLA_HEAD_PAD
        b = a + V7X_LANES
        c = b + V7X_LANES
        q_ref[:, a:b] = q[:, a:b].astype(BF16)
        q_ref[:, b:c] = _rope_mla(q[:, b:c], cm, sup, sdn).astype(BF16)
        k_ref[:, a:b] = kv[:, a:b].astype(BF16)
        k_ref[:, b:c] = kpe
        v_ref[:, hh * MLA_V_DIM:(hh + 1) * MLA_V_DIM] = kv[:, b:c].astype(BF16)

    cd, sd = cd_ref[...], sd_ref[...]
    off = MLA_ROPE_DIM
    pdq = rest[:, off:off + DIFF_QK_W] * DIFF_SCORE_SCALE
    for c in range(DIFF_QK_W // DIFF_HEAD_DIM):
        cols = slice(c * DIFF_HEAD_DIM, (c + 1) * DIFF_HEAD_DIM)
        dq_ref[:, cols] = _rope_half(pdq[:, cols], cd, sd).astype(BF16)
    pdk = rest[:, off + DIFF_QK_W:off + 2 * DIFF_QK_W]
    for c in range(DIFF_QK_W // DIFF_HEAD_DIM):
        cols = slice(c * DIFF_HEAD_DIM, (c + 1) * DIFF_HEAD_DIM)
        dk_ref[:, cols] = _rope_half(pdk[:, cols], cd, sd).astype(BF16)
    dv_ref[...] = rest[:, off + 2 * DIFF_QK_W:].astype(BF16)


def _mix_proj(x, g, w_in, qg, wuq_p, kvg, wukv, tables, seq):
    t, d = x.shape
    tm = PROJ_TM
    pos_blocks = seq // tm
    const = lambda i: (0, 0)
    row = lambda i: (i, 0)
    pos = lambda i: (i % pos_blocks, 0)
    wq = MLA_HEADS * MLA_HEAD_PAD
    out_w = (wq, wq, MLA_OUT, DIFF_QK_W, DIFF_QK_W, DIFF_OUT)
    once = dict(pipeline_mode=pl.Buffered(1))
    nbytes = (d * _IN_W * 2 + MLA_Q_RANK * wq * (2 + 4) + 2 * tm * d * 4
              + 2 * tm * sum(out_w) * 2 + tm * _IN_W * 4 + 2 * tm * wq * 4)
    return pl.pallas_call(
        _proj_kernel,
        grid=(t // tm,),
        in_specs=[
            pl.BlockSpec((tm, d), row),
            pl.BlockSpec((1, d), const),
            pl.BlockSpec((d, _IN_W), const, **once),
            pl.BlockSpec((1, MLA_Q_RANK), const),
            pl.BlockSpec((MLA_Q_RANK, wq), const, **once),
            pl.BlockSpec((1, MLA_KV_RANK), const),
            pl.BlockSpec((MLA_KV_RANK, wq), const, **once),
        ] + [pl.BlockSpec((tm, V7X_LANES), pos)] * 5,
        out_specs=[pl.BlockSpec((tm, w), row) for w in out_w],
        out_shape=[jax.ShapeDtypeStruct((t, w), BF16) for w in out_w],
        compiler_params=pltpu.CompilerParams(
            dimension_semantics=("parallel",),
            vmem_limit_bytes=_vmem_limit(nbytes)),
        name="mix_proj",
    )(x, g, w_in.astype(BF16), qg, wuq_p, kvg, wukv, *tables)


def _causal_mask(n_q, n_k):
    r = lax.broadcasted_iota(jnp.int32, (n_q, n_k), 0) // CHUNK
    c = lax.broadcasted_iota(jnp.int32, (n_q, n_k), 1) // CHUNK
    return r >= c


def _flash_scratch(n_maps, v_dim):
    stat = pltpu.VMEM((ATT_TQ, V7X_LANES), F32)
    return [stat, stat, pltpu.VMEM((ATT_TQ, v_dim), F32)] * n_maps


def _flash_init(m_ref, l_ref, acc_ref):
    m_ref[...] = jnp.full(m_ref.shape, -jnp.inf, F32)
    l_ref[...] = jnp.zeros(l_ref.shape, F32)
    acc_ref[...] = jnp.zeros(acc_ref.shape, F32)


def _flash_step(t, v, m_ref, l_ref, acc_ref):
    lanes = m_ref.shape[1]
    m_old = m_ref[...]
    m_new = jnp.maximum(m_old, jnp.max(t, axis=-1, keepdims=True))
    alpha = jnp.exp2(m_old - m_new)
    p = jnp.exp2(t - jnp.tile(m_new, (1, t.shape[1] // lanes)))
    l_ref[...] = alpha * l_ref[...] + jnp.sum(p, axis=-1, keepdims=True)
    pv = jnp.dot(p.astype(BF16), v, preferred_element_type=F32)
    acc_ref[...] = jnp.tile(alpha, (1, acc_ref.shape[1] // lanes)) * acc_ref[...] + pv
    m_ref[...] = m_new


def _attn_tile(qi, n_maps, scores, values, state):
    half = ATT_TQ // 2
    assert ATT_TQ == ATT_TK and half % CHUNK == 0

    def update(q_rows, k_rows, mask):
        for n in range(n_maps):
            t = scores(n, q_rows, k_rows)
            if mask is not None:
                t = jnp.where(mask, t, -jnp.inf)
            _flash_step(t, values(n, k_rows), *[r.at[q_rows] for r in state[3 * n:3 * n + 3]])

    for n in range(n_maps):
        _flash_init(*state[3 * n:3 * n + 3])
    base = pl.multiple_of(qi * ATT_TQ, ATT_TQ)
    update(pl.ds(0, ATT_TQ), pl.ds(base, half), _causal_mask(ATT_TQ, half))
    update(pl.ds(half, half), pl.ds(pl.multiple_of(base + half, half), half),
           _causal_mask(half, half))

    def body(j, carry):
        update(pl.ds(0, ATT_TQ), pl.ds(pl.multiple_of(j * ATT_TK, ATT_TK), ATT_TK), None)
        return carry

    lax.fori_loop(0, qi, body, 0)


def _qk(q, k):
    return lax.dot_general(q, k, (((1,), (1,)), ((), ())), preferred_element_type=F32)


def _mla_attn_kernel(q_ref, k_ref, v_ref, o_ref, *state):
    wq, wv = MLA_HEAD_PAD, MLA_V_DIM

    def scores(g, q_rows, k_rows):
        return _qk(q_ref[q_rows, g * wq:(g + 1) * wq], k_ref[k_rows, g * wq:(g + 1) * wq])

    def values(g, k_rows):
        return v_ref[k_rows, g * wv:(g + 1) * wv]

    _attn_tile(pl.program_id(2), MLA_GROUP, scores, values, state)
    for g in range(MLA_GROUP):
        _, l_ref, acc_ref = state[3 * g:3 * g + 3]
        o_ref[:, g * wv:(g + 1) * wv] = (acc_ref[...] / l_ref[...]).astype(o_ref.dtype)


def _mla_attn(q, k, v, batch, seq):
    nq = seq // ATT_TQ
    wq, wv = MLA_GROUP * MLA_HEAD_PAD, MLA_GROUP * MLA_V_DIM
    return pl.pallas_call(
        _mla_attn_kernel,
        grid=(batch, MLA_HEADS // MLA_GROUP, nq),
        in_specs=[
            pl.BlockSpec((ATT_TQ, wq), lambda b, h, i: (b * nq + i, h)),
            pl.BlockSpec((seq, wq), lambda b, h, i: (b, h)),
            pl.BlockSpec((seq, wv), lambda b, h, i: (b, h)),
        ],
        out_specs=pl.BlockSpec((ATT_TQ, wv), lambda b, h, i: (b * nq + i, h)),
        out_shape=jax.ShapeDtypeStruct((batch * seq, MLA_OUT), BF16),
        scratch_shapes=_flash_scratch(MLA_GROUP, MLA_V_DIM),
        compiler_params=pltpu.CompilerParams(
            dimension_semantics=("parallel", "parallel", "parallel"),
            vmem_limit_bytes=_vmem_limit(2 * (seq + ATT_TQ) * (wq + wv) * 2
                                         + 8 * MLA_GROUP * ATT_TQ * ATT_TK * 4)),
        name="mla_attn",
    )(q, k, v)


def _diff_attn_kernel(q_ref, k_ref, v_ref, lq1_ref, lk1_ref, lq2_ref, lk2_ref, g_ref, o_ref,
                      *state):
    d, wv = DIFF_HEAD_DIM, DIFF_V_DIM

    def scores(n, q_rows, k_rows):
        return _qk(q_ref[q_rows, n * d:(n + 1) * d], k_ref[k_rows, n * d:(n + 1) * d])

    def values(n, k_rows):
        return v_ref[k_rows, (n // 2) * wv:(n // 2 + 1) * wv]

    _attn_tile(pl.program_id(2), 2 * DIFF_GROUP, scores, values, state)
    lam = (jnp.exp(jnp.sum(lq1_ref[...] * lk1_ref[...], axis=-1, keepdims=True))
           - jnp.exp(jnp.sum(lq2_ref[...] * lk2_ref[...], axis=-1, keepdims=True))
           + LAMBDA_INIT)
    rep = wv // V7X_LANES
    for h in range(DIFF_GROUP):
        _, l1_ref, a1_ref, _, l2_ref, a2_ref = state[6 * h:6 * h + 6]
        o = (a1_ref[...] / jnp.tile(l1_ref[...], (1, rep))
             - lam * (a2_ref[...] / jnp.tile(l2_ref[...], (1, rep))))
        o_ref[:, h * wv:(h + 1) * wv] = (
            _rms(o, g_ref[...]) * (1.0 - LAMBDA_INIT)).astype(o_ref.dtype)


def _diff_attn(q, k, v, lq1, lk1, lq2, lk2, g, batch, seq):
    nq = seq // ATT_TQ
    w = DIFF_GROUP * DIFF_V_DIM
    vec = pl.BlockSpec((1, DIFF_HEAD_DIM), lambda b, h, i: (0, 0))
    return pl.pallas_call(
        _diff_attn_kernel,
        grid=(batch, DIFF_HEADS // DIFF_GROUP, nq),
        in_specs=[
            pl.BlockSpec((ATT_TQ, w), lambda b, h, i: (b * nq + i, h)),
            pl.BlockSpec((seq, w), lambda b, h, i: (b, h)),
            pl.BlockSpec((seq, w), lambda b, h, i: (b, h)),
            vec, vec, vec, vec,
            pl.BlockSpec((1, DIFF_V_DIM), lambda b, h, i: (0, 0)),
        ],
        out_specs=pl.BlockSpec((ATT_TQ, w), lambda b, h, i: (b * nq + i, h)),
        out_shape=jax.ShapeDtypeStruct((batch * seq, DIFF_OUT), BF16),
        scratch_shapes=_flash_scratch(2 * DIFF_GROUP, DIFF_V_DIM),
        compiler_params=pltpu.CompilerParams(
            dimension_semantics=("parallel", "parallel", "parallel"),
            vmem_limit_bytes=_vmem_limit(2 * (2 * seq + 2 * ATT_TQ) * w * 2
                                         + 16 * DIFF_GROUP * ATT_TQ * ATT_TK * 4)),
        name="diff_attn",
    )(q, k, v, lq1, lk1, lq2, lk2, g)


def _out_kernel(x_ref, oa_ref, ob_ref, wa_ref, wb_ref, g_ref, o_ref):
    o = _dot_bf16(oa_ref[...], wa_ref[...]) + _dot_bf16(ob_ref[...], wb_ref[...])
    o_ref[...] = x_ref[...] + _rms(o, g_ref[...])


def _out_proj(x, oa, ob, w, g):
    t, d = x.shape
    assert MLA_OUT == DIFF_OUT, "the two row blocks of w share one block shape"
    tm = OUT_TM
    const = lambda i: (0, 0)
    row = lambda i: (i, 0)
    once = dict(pipeline_mode=pl.Buffered(1))
    nbytes = (d * d * (w.dtype.itemsize + 2) + 4 * tm * d * 4 + 2 * tm * d * 2 + 2 * tm * d * 4)
    return pl.pallas_call(
        _out_kernel,
        grid=(t // tm,),
        in_specs=[
            pl.BlockSpec((tm, d), row),
            pl.BlockSpec((tm, MLA_OUT), row),
            pl.BlockSpec((tm, DIFF_OUT), row),
            pl.BlockSpec((MLA_OUT, d), const, **once),
            pl.BlockSpec((DIFF_OUT, d), lambda i: (1, 0), **once),
            pl.BlockSpec((1, d), const),
        ],
        out_specs=pl.BlockSpec((tm, d), row),
        out_shape=jax.ShapeDtypeStruct((t, d), F32),
        compiler_params=pltpu.CompilerParams(
            dimension_semantics=("parallel",),
            vmem_limit_bytes=_vmem_limit(nbytes)),
        name="out_proj",
    )(x, oa, ob, w, w, g)


def _pack_w_uq(w_uq):
    r = w_uq.shape[0]
    w = w_uq.reshape(r, MLA_HEADS, MLA_QK_DIM)
    pad = jnp.zeros((r, MLA_HEADS, MLA_HEAD_PAD - MLA_QK_DIM), w_uq.dtype)
    return jnp.concatenate([w, pad], axis=-1).reshape(r, MLA_HEADS * MLA_HEAD_PAD).astype(BF16)


def _rope_tables(seq):
    pos = jnp.arange(seq, dtype=F32)[:, None]

    def angles(d):
        inv_freq = ROPE_THETA ** (-jnp.arange(0, d, 2, dtype=F32) / d)
        return pos * inv_freq[None, :]

    ang = angles(MLA_ROPE_DIM)
    zeros = jnp.zeros((seq, V7X_LANES - MLA_ROPE_DIM), F32)
    half0 = jnp.zeros_like(ang)
    cos_m = jnp.concatenate([jnp.cos(ang), jnp.cos(ang), zeros], axis=1)
    sin_up = jnp.concatenate([half0, jnp.sin(ang), zeros], axis=1)
    sin_dn = jnp.concatenate([-jnp.sin(ang), half0, zeros], axis=1)
    ang = angles(DIFF_HEAD_DIM)
    cos_d = jnp.concatenate([jnp.cos(ang), jnp.cos(ang)], axis=1)
    sin_d = jnp.concatenate([-jnp.sin(ang), jnp.sin(ang)], axis=1)
    return cos_m, sin_up, sin_dn, cos_d, sin_d


def kernel(x, ffn1_pre_g, ffn1_w_gate, ffn1_w_up, ffn1_w_down, ffn1_post_g, mix_pre_g, w_in, mla_q_norm_g, mla_w_uq, mla_kv_norm_g, mla_w_ukv, diff_lambda_q1, diff_lambda_k1, diff_lambda_q2, diff_lambda_k2, diff_subln_g, w_out, mix_post_g, ffn2_pre_g, ffn2_w_gate, ffn2_w_up, ffn2_w_down, ffn2_post_g):
    batch, seq, d = x.shape
    depth = ffn1_pre_g.shape[0]
    assert depth == 1 and d == D_MODEL
    assert seq % ATT_TQ == 0 and seq % PROJ_TM == 0 and (batch * seq) % FFN_TM == 0
    tables = _rope_tables(seq)
    xt = x.reshape(batch * seq, d)
    for l in range(depth):
        xt = _ffn(xt, ffn1_pre_g[l][None], ffn1_w_gate[l], ffn1_w_up[l], ffn1_w_down[l],
                  ffn1_post_g[l][None])
        q, k, v, dq, dk, dv = _mix_proj(
            xt, mix_pre_g[l][None], w_in[l], mla_q_norm_g[l][None],
            _pack_w_uq(mla_w_uq[l]), mla_kv_norm_g[l][None], mla_w_ukv[l],
            tables, seq)
        o_mla = _mla_attn(q, k, v, batch, seq)
        o_diff = _diff_attn(dq, dk, dv, diff_lambda_q1[l][None], diff_lambda_k1[l][None],
                            diff_lambda_q2[l][None], diff_lambda_k2[l][None],
                            diff_subln_g[l][None], batch, seq)
        xt = _out_proj(xt, o_mla, o_diff, w_out[l], mix_post_g[l][None])
        xt = _ffn(xt, ffn2_pre_g[l][None], ffn2_w_gate[l], ffn2_w_up[l], ffn2_w_down[l],
                  ffn2_post_g[l][None])
    return xt.reshape(batch, seq, d)
```

```python
import math

import jax
import jax.numpy as jnp
import numpy as np
from jax import lax
from jax.experimental import pallas as pl
from jax.experimental.pallas import tpu as pltpu

D_MODEL = 2048
CHUNK = 64
ROPE_THETA = 10000.0
NORM_EPS = 1e-6
MLA_HEADS = 8
MLA_Q_RANK = 512
MLA_KV_RANK = 512
MLA_NOPE_DIM = 128
MLA_ROPE_DIM = 64
MLA_V_DIM = 128
MLA_QK_DIM = MLA_NOPE_DIM + MLA_ROPE_DIM
DIFF_HEADS = 4
DIFF_HEAD_DIM = 128
DIFF_V_DIM = 2 * DIFF_HEAD_DIM
MLA_OUT = MLA_HEADS * MLA_V_DIM
DIFF_OUT = DIFF_HEADS * DIFF_V_DIM
DIFF_QK_W = DIFF_HEADS * 2 * DIFF_HEAD_DIM
D_FF = 5632
LAMBDA_INIT = 0.8 - 0.6 * math.exp(-0.3 * 0)
LOG2_E = math.log2(math.e)
MLA_SCORE_SCALE = MLA_QK_DIM ** -0.5 * LOG2_E
DIFF_SCORE_SCALE = DIFF_HEAD_DIM ** -0.5 * LOG2_E

V7X_LANES = 128
V7X_VMEM_BYTES = 64 * 1024 * 1024
MLA_HEAD_PAD = 2 * V7X_LANES

FFN_TM = 1024
FFN_TF = 512
FFN_ROWS = 256
PROJ_TM = 256
ATT_TQ = 512
ATT_TK = 512
MLA_GROUP = 8
DIFF_GROUP = 4
OUT_TM = 512

BF16 = jnp.bfloat16
F32 = jnp.float32


def _vmem_limit(nbytes):
    return int(min(V7X_VMEM_BYTES - 4 * 1024 * 1024, max(32 * 1024 * 1024, nbytes * 3 // 2)))


def _dot_bf16(a, w):
    return lax.dot_general(a, w, (((1,), (0,)), ((), ())), preferred_element_type=F32)


def _rms(v, g):
    ms = jnp.mean(v * v, axis=-1, keepdims=True)
    return v * lax.rsqrt(ms + NORM_EPS) * g


def _swiglu_hidden(h, wg, wu):
    g = _dot_bf16(h, wg)
    u = _dot_bf16(h, wu)
    return (g * (1.0 / (1.0 + jnp.exp(-g))) * u).astype(BF16)


def _ffn_kernel(x_ref, pre_g_ref, wg_ref, wu_ref, wd_ref, post_g_ref, o_hbm, acc_ref, h_ref, sem):
    i = pl.program_id(0)
    f = pl.program_id(1)
    last = pl.num_programs(1) - 1
    n_chunks = FFN_TM // FFN_ROWS
    chunks = [pl.ds(c * FFN_ROWS, FFN_ROWS) for c in range(n_chunks)]

    def out_copy(tile, c):
        dst = o_hbm.at[pl.ds(pl.multiple_of(tile * FFN_TM + c * FFN_ROWS, FFN_ROWS), FFN_ROWS)]
        return pltpu.make_async_copy(acc_ref.at[chunks[c]], dst, sem.at[c])

    @pl.when(jnp.logical_and(f == 0, i > 0))
    def _():
        for c in range(n_chunks):
            out_copy(i - 1, c).wait()

    @pl.when(f == 0)
    def _():
        for rows in chunks:
            h = _rms(x_ref[rows, :], pre_g_ref[...]).astype(BF16)
            h_ref[rows, :] = h
            acc_ref[rows, :] = _dot_bf16(_swiglu_hidden(h, wg_ref[...], wu_ref[...]), wd_ref[...])

    @pl.when(jnp.logical_and(f > 0, f < last))
    def _():
        a = _swiglu_hidden(h_ref[...], wg_ref[...], wu_ref[...])
        acc_ref[...] += _dot_bf16(a, wd_ref[...])

    @pl.when(f == last)
    def _():
        for c, rows in enumerate(chunks):
            a = _swiglu_hidden(h_ref[rows, :], wg_ref[...], wu_ref[...])
            acc = acc_ref[rows, :] + _dot_bf16(a, wd_ref[...])
            acc_ref[rows, :] = x_ref[rows, :] + 0.5 * _rms(acc, post_g_ref[...])
            out_copy(i, c).start()

        @pl.when(i == pl.num_programs(0) - 1)
        def _():
            for c in range(n_chunks):
                out_copy(i, c).wait()


def _ffn(x, pre_g, wg, wu, wd, post_g):
    t, d = x.shape
    assert D_FF // FFN_TF >= 2, "first and last hidden steps must be distinct"
    nbytes = (3 * FFN_TM * d * 4 + FFN_TM * d * 2 + 2 * 3 * d * FFN_TF * wg.dtype.itemsize
              + 3 * FFN_TM * FFN_TF * 4)
    return pl.pallas_call(
        _ffn_kernel,
        grid=(t // FFN_TM, D_FF // FFN_TF),
        in_specs=[
            pl.BlockSpec((FFN_TM, d), lambda i, f: (i, 0)),
            pl.BlockSpec((1, d), lambda i, f: (0, 0)),
            pl.BlockSpec((d, FFN_TF), lambda i, f: (0, f)),
            pl.BlockSpec((d, FFN_TF), lambda i, f: (0, f)),
            pl.BlockSpec((FFN_TF, d), lambda i, f: (f, 0)),
            pl.BlockSpec((1, d), lambda i, f: (0, 0)),
        ],
        out_specs=pl.BlockSpec(memory_space=pl.ANY),
        out_shape=jax.ShapeDtypeStruct((t, d), F32),
        scratch_shapes=[pltpu.VMEM((FFN_TM, d), F32), pltpu.VMEM((FFN_TM, d), BF16),
                        pltpu.SemaphoreType.DMA((FFN_TM // FFN_ROWS,))],
        compiler_params=pltpu.CompilerParams(
            dimension_semantics=("arbitrary", "arbitrary"),
            vmem_limit_bytes=_vmem_limit(nbytes)),
        name="ffn",
    )(x, pre_g, wg, wu, wd, post_g)


_IN_LATENT = MLA_Q_RANK + MLA_KV_RANK
_IN_W = _IN_LATENT + MLA_ROPE_DIM + 2 * DIFF_QK_W + DIFF_OUT


def _rope_half(v, cos, sin_signed):
    return v * cos + pltpu.roll(v, DIFF_HEAD_DIM // 2, 1) * sin_signed


def _rope_mla(v, cos, sin_up, sin_dn):
    half = MLA_ROPE_DIM // 2
    return (v * cos + pltpu.roll(v, half, 1) * sin_up
            + pltpu.roll(v, V7X_LANES - half, 1) * sin_dn)


def _proj_kernel(x_ref, g_ref, win_ref, qg_ref, wuq_ref, kvg_ref, wukv_ref,
                 cm_ref, sup_ref, sdn_ref, cd_ref, sd_ref,
                 q_ref, k_ref, v_ref, dq_ref, dk_ref, dv_ref):
    h = _rms(x_ref[...], g_ref[...]).astype(BF16)

    pa = _dot_bf16(h, win_ref[:, 0:_IN_LATENT])
    cq = _rms(pa[:, 0:MLA_Q_RANK], qg_ref[...]).astype(BF16)
    ckv = _rms(pa[:, MLA_Q_RANK:_IN_LATENT], kvg_ref[...]).astype(BF16)
    rest = _dot_bf16(h, win_ref[:, _IN_LATENT:_IN_W])
    cm, sup, sdn = cm_ref[...], sup_ref[...], sdn_ref[...]
    kpe = _rope_mla(rest[:, 0:V7X_LANES], cm, sup, sdn).astype(BF16)

    q = _dot_bf16(cq, wuq_ref[...]) * MLA_SCORE_SCALE
    kv = _dot_bf16(ckv, wukv_ref[...])
    for hh in range(MLA_HEADS):
        a = hh * MLA_HEAD_PAD
        b = a + V7X_LANES
        c = b + V7X_LANES
        q_ref[:, a:b] = q[:, a:b].astype(BF16)
        q_ref[:, b:c] = _rope_mla(q[:, b:c], cm, sup, sdn).astype(BF16)
        k_ref[:, a:b] = kv[:, a:b].astype(BF16)
        k_ref[:, b:c] = kpe
        v_ref[:, hh * MLA_V_DIM:(hh + 1) * MLA_V_DIM] = kv[:, b:c].astype(BF16)

    cd, sd = cd_ref[...], sd_ref[...]
    off = MLA_ROPE_DIM
    pdq = rest[:, off:off + DIFF_QK_W] * DIFF_SCORE_SCALE
    for c in range(DIFF_QK_W // DIFF_HEAD_DIM):
        cols = slice(c * DIFF_HEAD_DIM, (c + 1) * DIFF_HEAD_DIM)
        dq_ref[:, cols] = _rope_half(pdq[:, cols], cd, sd).astype(BF16)
    pdk = rest[:, off + DIFF_QK_W:off + 2 * DIFF_QK_W]
    for c in range(DIFF_QK_W // DIFF_HEAD_DIM):
        cols = slice(c * DIFF_HEAD_DIM, (c + 1) * DIFF_HEAD_DIM)
        dk_ref[:, cols] = _rope_half(pdk[:, cols], cd, sd).astype(BF16)
    dv_ref[...] = rest[:, off + 2 * DIFF_QK_W:].astype(BF16)


def _mix_proj(x, g, w_in, qg, wuq_p, kvg, wukv, tables, seq):
    t, d = x.shape
    tm = PROJ_TM
    pos_blocks = seq // tm
    const = lambda i: (0, 0)
    row = lambda i: (i, 0)
    pos = lambda i: (i % pos_blocks, 0)
    wq = MLA_HEADS * MLA_HEAD_PAD
    out_w = (wq, wq, MLA_OUT, DIFF_QK_W, DIFF_QK_W, DIFF_OUT)
    once = dict(pipeline_mode=pl.Buffered(1))
    nbytes = (d * _IN_W * 2 + MLA_Q_RANK * wq * (2 + 4) + 2 * tm * d * 4
              + 2 * tm * sum(out_w) * 2 + tm * _IN_W * 4 + 2 * tm * wq * 4)
    return pl.pallas_call(
        _proj_kernel,
        grid=(t // tm,),
        in_specs=[
            pl.BlockSpec((tm, d), row),
            pl.BlockSpec((1, d), const),
            pl.BlockSpec((d, _IN_W), const, **once),
            pl.BlockSpec((1, MLA_Q_RANK), const),
            pl.BlockSpec((MLA_Q_RANK, wq), const, **once),
            pl.BlockSpec((1, MLA_KV_RANK), const),
            pl.BlockSpec((MLA_KV_RANK, wq), const, **once),
        ] + [pl.BlockSpec((tm, V7X_LANES), pos)] * 5,
        out_specs=[pl.BlockSpec((tm, w), row) for w in out_w],
        out_shape=[jax.ShapeDtypeStruct((t, w), BF16) for w in out_w],
        compiler_params=pltpu.CompilerParams(
            dimension_semantics=("parallel",),
            vmem_limit_bytes=_vmem_limit(nbytes)),
        name="mix_proj",
    )(x, g, w_in.astype(BF16), qg, wuq_p, kvg, wukv, *tables)


def _causal_mask(n_q, n_k):
    r = lax.broadcasted_iota(jnp.int32, (n_q, n_k), 0) // CHUNK
    c = lax.broadcasted_iota(jnp.int32, (n_q, n_k), 1) // CHUNK
    return r >= c


def _flash_scratch(n_maps, v_dim):
    stat = pltpu.VMEM((ATT_TQ, V7X_LANES), F32)
    return [stat, stat, pltpu.VMEM((ATT_TQ, v_dim), F32)] * n_maps


def _flash_init(m_ref, l_ref, acc_ref):
    m_ref[...] = jnp.full(m_ref.shape, -jnp.inf, F32)
    l_ref[...] = jnp.zeros(l_ref.shape, F32)
    acc_ref[...] = jnp.zeros(acc_ref.shape, F32)


def _flash_step(t, v, m_ref, l_ref, acc_ref):
    lanes = m_ref.shape[1]
    m_old = m_ref[...]
    m_new = jnp.maximum(m_old, jnp.max(t, axis=-1, keepdims=True))
    alpha = jnp.exp2(m_old - m_new)
    p = jnp.exp2(t - jnp.tile(m_new, (1, t.shape[1] // lanes)))
    l_ref[...] = alpha * l_ref[...] + jnp.sum(p, axis=-1, keepdims=True)
    pv = jnp.dot(p.astype(BF16), v, preferred_element_type=F32)
    acc_ref[...] = jnp.tile(alpha, (1, acc_ref.shape[1] // lanes)) * acc_ref[...] + pv
    m_ref[...] = m_new


def _attn_tile(qi, n_maps, scores, values, state):
    half = ATT_TQ // 2
    assert ATT_TQ == ATT_TK and half % CHUNK == 0

    def update(q_rows, k_rows, mask):
        for n in range(n_maps):
            t = scores(n, q_rows, k_rows)
            if mask is not None:
                t = jnp.where(mask, t, -jnp.inf)
            _flash_step(t, values(n, k_rows), *[r.at[q_rows] for r in state[3 * n:3 * n + 3]])

    for n in range(n_maps):
        _flash_init(*state[3 * n:3 * n + 3])
    base = pl.multiple_of(qi * ATT_TQ, ATT_TQ)
    update(pl.ds(0, ATT_TQ), pl.ds(base, half), _causal_mask(ATT_TQ, half))
    update(pl.ds(half, half), pl.ds(pl.multiple_of(base + half, half), half),
           _causal_mask(half, half))

    def body(j, carry):
        update(pl.ds(0, ATT_TQ), pl.ds(pl.multiple_of(j * ATT_TK, ATT_TK), ATT_TK), None)
        return carry

    lax.fori_loop(0, qi, body, 0)


def _qk(q, k):
    return lax.dot_general(q, k, (((1,), (1,)), ((), ())), preferred_element_type=F32)


def _mla_attn_kernel(q_ref, k_ref, v_ref, o_ref, *state):
    wq, wv = MLA_HEAD_PAD, MLA_V_DIM

    def scores(g, q_rows, k_rows):
        return _qk(q_ref[q_rows, g * wq:(g + 1) * wq], k_ref[k_rows, g * wq:(g + 1) * wq])

    def values(g, k_rows):
        return v_ref[k_rows, g * wv:(g + 1) * wv]

    _attn_tile(pl.program_id(2), MLA_GROUP, scores, values, state)
    for g in range(MLA_GROUP):
        _, l_ref, acc_ref = state[3 * g:3 * g + 3]
        o_ref[:, g * wv:(g + 1) * wv] = (acc_ref[...] / l_ref[...]).astype(o_ref.dtype)


def _mla_attn(q, k, v, batch, seq):
    nq = seq // ATT_TQ
    wq, wv = MLA_GROUP * MLA_HEAD_PAD, MLA_GROUP * MLA_V_DIM
    return pl.pallas_call(
        _mla_attn_kernel,
        grid=(batch, MLA_HEADS // MLA_GROUP, nq),
        in_specs=[
            pl.BlockSpec((ATT_TQ, wq), lambda b, h, i: (b * nq + i, h)),
            pl.BlockSpec((seq, wq), lambda b, h, i: (b, h)),
            pl.BlockSpec((seq, wv), lambda b, h, i: (b, h)),
        ],
        out_specs=pl.BlockSpec((ATT_TQ, wv), lambda b, h, i: (b * nq + i, h)),
        out_shape=jax.ShapeDtypeStruct((batch * seq, MLA_OUT), BF16),
        scratch_shapes=_flash_scratch(MLA_GROUP, MLA_V_DIM),
        compiler_params=pltpu.CompilerParams(
            dimension_semantics=("parallel", "parallel", "parallel"),
            vmem_limit_bytes=_vmem_limit(2 * (seq + ATT_TQ) * (wq + wv) * 2
                                         + 8 * MLA_GROUP * ATT_TQ * ATT_TK * 4)),
        name="mla_attn",
    )(q, k, v)


def _diff_attn_kernel(q_ref, k_ref, v_ref, lq1_ref, lk1_ref, lq2_ref, lk2_ref, g_ref, o_ref,
                      *state):
    d, wv = DIFF_HEAD_DIM, DIFF_V_DIM

    def scores(n, q_rows, k_rows):
        return _qk(q_ref[q_rows, n * d:(n + 1) * d], k_ref[k_rows, n * d:(n + 1) * d])

    def values(n, k_rows):
        return v_ref[k_rows, (n // 2) * wv:(n // 2 + 1) * wv]

    _attn_tile(pl.program_id(2), 2 * DIFF_GROUP, scores, values, state)
    lam = (jnp.exp(jnp.sum(lq1_ref[...] * lk1_ref[...], axis=-1, keepdims=True))
           - jnp.exp(jnp.sum(lq2_ref[...] * lk2_ref[...], axis=-1, keepdims=True))
           + LAMBDA_INIT)
    rep = wv // V7X_LANES
    for h in range(DIFF_GROUP):
        _, l1_ref, a1_ref, _, l2_ref, a2_ref = state[6 * h:6 * h + 6]
        o = (a1_ref[...] / jnp.tile(l1_ref[...], (1, rep))
             - lam * (a2_ref[...] / jnp.tile(l2_ref[...], (1, rep))))
        o_ref[:, h * wv:(h + 1) * wv] = (
            _rms(o, g_ref[...]) * (1.0 - LAMBDA_INIT)).astype(o_ref.dtype)


def _diff_attn(q, k, v, lq1, lk1, lq2, lk2, g, batch, seq):
    nq = seq // ATT_TQ
    w = DIFF_GROUP * DIFF_V_DIM
    vec = pl.BlockSpec((1, DIFF_HEAD_DIM), lambda b, h, i: (0, 0))
    return pl.pallas_call(
        _diff_attn_kernel,
        grid=(batch, DIFF_HEADS // DIFF_GROUP, nq),
        in_specs=[
            pl.BlockSpec((ATT_TQ, w), lambda b, h, i: (b * nq + i, h)),
            pl.BlockSpec((seq, w), lambda b, h, i: (b, h)),
            pl.BlockSpec((seq, w), lambda b, h, i: (b, h)),
            vec, vec, vec, vec,
            pl.BlockSpec((1, DIFF_V_DIM), lambda b, h, i: (0, 0)),
        ],
        out_specs=pl.BlockSpec((ATT_TQ, w), lambda b, h, i: (b * nq + i, h)),
        out_shape=jax.ShapeDtypeStruct((batch * seq, DIFF_OUT), BF16),
        scratch_shapes=_flash_scratch(2 * DIFF_GROUP, DIFF_V_DIM),
        compiler_params=pltpu.CompilerParams(
            dimension_semantics=("parallel", "parallel", "parallel"),
            vmem_limit_bytes=_vmem_limit(2 * (2 * seq + 2 * ATT_TQ) * w * 2
                                         + 16 * DIFF_GROUP * ATT_TQ * ATT_TK * 4)),
        name="diff_attn",
    )(q, k, v, lq1, lk1, lq2, lk2, g)


def _out_kernel(x_ref, oa_ref, ob_ref, wa_ref, wb_ref, g_ref, o_ref):
    o = _dot_bf16(oa_ref[...], wa_ref[...]) + _dot_bf16(ob_ref[...], wb_ref[...])
    o_ref[...] = x_ref[...] + _rms(o, g_ref[...])


def _out_proj(x, oa, ob, w, g):
    t, d = x.shape
    assert MLA_OUT == DIFF_OUT, "the two row blocks of w share one block shape"
    tm = OUT_TM
    const = lambda i: (0, 0)
    row = lambda i: (i, 0)
    once = dict(pipeline_mode=pl.Buffered(1))
    nbytes = (d * d * (w.dtype.itemsize + 2) + 4 * tm * d * 4 + 2 * tm * d * 2 + 2 * tm * d * 4)
    return pl.pallas_call(
        _out_kernel,
        grid=(t // tm,),
        in_specs=[
            pl.BlockSpec((tm, d), row),
            pl.BlockSpec((tm, MLA_OUT), row),
            pl.BlockSpec((tm, DIFF_OUT), row),
            pl.BlockSpec((MLA_OUT, d), const, **once),
            pl.BlockSpec((DIFF_OUT, d), lambda i: (1, 0), **once),
            pl.BlockSpec((1, d), const),
        ],
        out_specs=pl.BlockSpec((tm, d), row),
        out_shape=jax.ShapeDtypeStruct((t, d), F32),
        compiler_params=pltpu.CompilerParams(
            dimension_semantics=("parallel",),
            vmem_limit_bytes=_vmem_limit(nbytes)),
        name="out_proj",
    )(x, oa, ob, w, w, g)


def _pack_w_uq(w_uq):
    r = w_uq.shape[0]
    w = w_uq.reshape(r, MLA_HEADS, MLA_QK_DIM)
    pad = jnp.zeros((r, MLA_HEADS, MLA_HEAD_PAD - MLA_QK_DIM), w_uq.dtype)
    return jnp.concatenate([w, pad], axis=-1).reshape(r, MLA_HEADS * MLA_HEAD_PAD).astype(BF16)


def _rope_tables(seq):
    pos = np.arange(seq, dtype=np.float64)[:, None]

    def angles(d):
        inv_freq = ROPE_THETA ** (-np.arange(0, d, 2, dtype=np.float64) / d)
        return pos * inv_freq[None, :]

    ang = angles(MLA_ROPE_DIM)
    zeros = np.zeros((seq, V7X_LANES - MLA_ROPE_DIM))
    half0 = np.zeros_like(ang)
    cos_m = np.concatenate([np.cos(ang), np.cos(ang), zeros], axis=1)
    sin_up = np.concatenate([half0, np.sin(ang), zeros], axis=1)
    sin_dn = np.concatenate([-np.sin(ang), half0, zeros], axis=1)
    ang = angles(DIFF_HEAD_DIM)
    cos_d = np.concatenate([np.cos(ang), np.cos(ang)], axis=1)
    sin_d = np.concatenate([-np.sin(ang), np.sin(ang)], axis=1)
    return tuple(jnp.asarray(t, F32) for t in (cos_m, sin_up, sin_dn, cos_d, sin_d))


def kernel(x, ffn1_pre_g, ffn1_w_gate, ffn1_w_up, ffn1_w_down, ffn1_post_g, mix_pre_g, w_in, mla_q_norm_g, mla_w_uq, mla_kv_norm_g, mla_w_ukv, diff_lambda_q1, diff_lambda_k1, diff_lambda_q2, diff_lambda_k2, diff_subln_g, w_out, mix_post_g, ffn2_pre_g, ffn2_w_gate, ffn2_w_up, ffn2_w_down, ffn2_post_g):
    batch, seq, d = x.shape
    depth = ffn1_pre_g.shape[0]
    assert depth == 1 and d == D_MODEL
    assert seq % ATT_TQ == 0 and seq % PROJ_TM == 0 and (batch * seq) % FFN_TM == 0
    tables = _rope_tables(seq)
    xt = x.reshape(batch * seq, d)
    for l in range(depth):
        xt = _ffn(xt, ffn1_pre_g[l][None], ffn1_w_gate[l], ffn1_w_up[l], ffn1_w_down[l],
                  ffn1_post_g[l][None])
        q, k, v, dq, dk, dv = _mix_proj(
            xt, mix_pre_g[l][None], w_in[l], mla_q_norm_g[l][None],
            _pack_w_uq(mla_w_uq[l]), mla_kv_norm_g[l][None], mla_w_ukv[l],
            tables, seq)
        o_mla = _mla_attn(q, k, v, batch, seq)
        o_diff = _diff_attn(dq, dk, dv, diff_lambda_q1[l][None], diff_lambda_k1[l][None],
                            diff_lambda_q2[l][None], diff_lambda_k2[l][None],
                            diff_subln_g[l][None], batch, seq)
        xt = _out_proj(xt, o_mla, o_diff, w_out[l], mix_post_g[l][None])
        xt = _ffn(xt, ffn2_pre_g[l][None], ffn2_w_gate[l], ffn2_w_up[l], ffn2_w_down[l],
                  ffn2_post_g[l][None])
    return xt.reshape(batch, seq, d)
```

```python
import math

import jax
import jax.numpy as jnp
import numpy as np
from jax import lax
from jax.experimental import pallas as pl
from jax.experimental.pallas import tpu as pltpu

D_MODEL = 2048
CHUNK = 64
ROPE_THETA = 10000.0
NORM_EPS = 1e-6
MLA_HEADS = 8
MLA_Q_RANK = 512
MLA_KV_RANK = 512
MLA_NOPE_DIM = 128
MLA_ROPE_DIM = 64
MLA_V_DIM = 128
MLA_QK_DIM = MLA_NOPE_DIM + MLA_ROPE_DIM
DIFF_HEADS = 4
DIFF_HEAD_DIM = 128
DIFF_V_DIM = 2 * DIFF_HEAD_DIM
MLA_OUT = MLA_HEADS * MLA_V_DIM
DIFF_OUT = DIFF_HEADS * DIFF_V_DIM
DIFF_QK_W = DIFF_HEADS * 2 * DIFF_HEAD_DIM
D_FF = 5632
LAMBDA_INIT = 0.8 - 0.6 * math.exp(-0.3 * 0)
LOG2_E = math.log2(math.e)
MLA_SCORE_SCALE = MLA_QK_DIM ** -0.5 * LOG2_E
DIFF_SCORE_SCALE = DIFF_HEAD_DIM ** -0.5 * LOG2_E

V7X_LANES = 128
V7X_VMEM_BYTES = 64 * 1024 * 1024
MLA_HEAD_PAD = 2 * V7X_LANES

FFN_TM = 1024
FFN_TF = 512
FFN_ROWS = 256
PROJ_TM = 512
ATT_TQ = 512
ATT_TK = 512
MLA_GROUP = 8
DIFF_GROUP = 4
OUT_TM = 512

BF16 = jnp.bfloat16
F32 = jnp.float32


def _vmem_limit(nbytes):
    return int(min(V7X_VMEM_BYTES - 4 * 1024 * 1024, max(32 * 1024 * 1024, nbytes * 3 // 2)))


def _dot_bf16(a, w):
    return lax.dot_general(a, w, (((1,), (0,)), ((), ())), preferred_element_type=F32)


def _rms(v, g):
    ms = jnp.mean(v * v, axis=-1, keepdims=True)
    return v * lax.rsqrt(ms + NORM_EPS) * g


def _swiglu_hidden(h, wg, wu):
    g = _dot_bf16(h, wg)
    u = _dot_bf16(h, wu)
    return (g * (1.0 / (1.0 + jnp.exp(-g))) * u).astype(BF16)


def _ffn_kernel(x_ref, pre_g_ref, wg_ref, wu_ref, wd_ref, post_g_ref, o_hbm, acc_ref, h_ref, sem):
    i = pl.program_id(0)
    f = pl.program_id(1)
    last = pl.num_programs(1) - 1
    n_chunks = FFN_TM // FFN_ROWS
    chunks = [pl.ds(c * FFN_ROWS, FFN_ROWS) for c in range(n_chunks)]

    def out_copy(tile, c):
        dst = o_hbm.at[pl.ds(pl.multiple_of(tile * FFN_TM + c * FFN_ROWS, FFN_ROWS), FFN_ROWS)]
        return pltpu.make_async_copy(acc_ref.at[chunks[c]], dst, sem.at[c])

    @pl.when(jnp.logical_and(f == 0, i > 0))
    def _():
        for c in range(n_chunks):
            out_copy(i - 1, c).wait()

    @pl.when(f == 0)
    def _():
        for rows in chunks:
            h = _rms(x_ref[rows, :], pre_g_ref[...]).astype(BF16)
            h_ref[rows, :] = h
            acc_ref[rows, :] = _dot_bf16(_swiglu_hidden(h, wg_ref[...], wu_ref[...]), wd_ref[...])

    @pl.when(jnp.logical_and(f > 0, f < last))
    def _():
        a = _swiglu_hidden(h_ref[...], wg_ref[...], wu_ref[...])
        acc_ref[...] += _dot_bf16(a, wd_ref[...])

    @pl.when(f == last)
    def _():
        for c, rows in enumerate(chunks):
            a = _swiglu_hidden(h_ref[rows, :], wg_ref[...], wu_ref[...])
            acc = acc_ref[rows, :] + _dot_bf16(a, wd_ref[...])
            acc_ref[rows, :] = x_ref[rows, :] + 0.5 * _rms(acc, post_g_ref[...])
            out_copy(i, c).start()

        @pl.when(i == pl.num_programs(0) - 1)
        def _():
            for c in range(n_chunks):
                out_copy(i, c).wait()


def _ffn(x, pre_g, wg, wu, wd, post_g):
    t, d = x.shape
    assert D_FF // FFN_TF >= 2, "first and last hidden steps must be distinct"
    nbytes = (3 * FFN_TM * d * 4 + FFN_TM * d * 2 + 2 * 3 * d * FFN_TF * wg.dtype.itemsize
              + 3 * FFN_TM * FFN_TF * 4)
    return pl.pallas_call(
        _ffn_kernel,
        grid=(t // FFN_TM, D_FF // FFN_TF),
        in_specs=[
            pl.BlockSpec((FFN_TM, d), lambda i, f: (i, 0)),
            pl.BlockSpec((1, d), lambda i, f: (0, 0)),
            pl.BlockSpec((d, FFN_TF), lambda i, f: (0, f)),
            pl.BlockSpec((d, FFN_TF), lambda i, f: (0, f)),
            pl.BlockSpec((FFN_TF, d), lambda i, f: (f, 0)),
            pl.BlockSpec((1, d), lambda i, f: (0, 0)),
        ],
        out_specs=pl.BlockSpec(memory_space=pl.ANY),
        out_shape=jax.ShapeDtypeStruct((t, d), F32),
        scratch_shapes=[pltpu.VMEM((FFN_TM, d), F32), pltpu.VMEM((FFN_TM, d), BF16),
                        pltpu.SemaphoreType.DMA((FFN_TM // FFN_ROWS,))],
        compiler_params=pltpu.CompilerParams(
            dimension_semantics=("arbitrary", "arbitrary"),
            vmem_limit_bytes=_vmem_limit(nbytes)),
        name="ffn",
    )(x, pre_g, wg, wu, wd, post_g)


_IN_LATENT = MLA_Q_RANK + MLA_KV_RANK
_IN_W = _IN_LATENT + MLA_ROPE_DIM + 2 * DIFF_QK_W + DIFF_OUT


def _rope_half(v, cos, sin_signed):
    return v * cos + pltpu.roll(v, DIFF_HEAD_DIM // 2, 1) * sin_signed


def _rope_mla(v, cos, sin_up, sin_dn):
    half = MLA_ROPE_DIM // 2
    return (v * cos + pltpu.roll(v, half, 1) * sin_up
            + pltpu.roll(v, V7X_LANES - half, 1) * sin_dn)


def _proj_kernel(x_ref, g_ref, win_ref, qg_ref, wuq_ref, kvg_ref, wukv_ref,
                 cm_ref, sup_ref, sdn_ref, cd_ref, sd_ref,
                 q_ref, k_ref, v_ref, dq_ref, dk_ref, dv_ref):
    h = _rms(x_ref[...], g_ref[...]).astype(BF16)

    pa = _dot_bf16(h, win_ref[:, 0:_IN_LATENT])
    cq = _rms(pa[:, 0:MLA_Q_RANK], qg_ref[...]).astype(BF16)
    ckv = _rms(pa[:, MLA_Q_RANK:_IN_LATENT], kvg_ref[...]).astype(BF16)
    rest = _dot_bf16(h, win_ref[:, _IN_LATENT:_IN_W])
    cm, sup, sdn = cm_ref[...], sup_ref[...], sdn_ref[...]
    kpe = _rope_mla(rest[:, 0:V7X_LANES], cm, sup, sdn).astype(BF16)

    q = _dot_bf16(cq, wuq_ref[...]) * MLA_SCORE_SCALE
    kv = _dot_bf16(ckv, wukv_ref[...])
    for hh in range(MLA_HEADS):
        a = hh * MLA_HEAD_PAD
        b = a + V7X_LANES
        c = b + V7X_LANES
        qa = hh * MLA_QK_DIM
        q_ref[:, a:b] = q[:, qa:qa + MLA_NOPE_DIM].astype(BF16)
        if (qa + MLA_NOPE_DIM) % V7X_LANES == 0:
            pe = q[:, qa + MLA_NOPE_DIM:qa + MLA_NOPE_DIM + V7X_LANES]
        else:
            pe = pltpu.roll(q[:, qa + MLA_ROPE_DIM:qa + MLA_QK_DIM], MLA_ROPE_DIM, 1)
        q_ref[:, b:c] = _rope_mla(pe, cm, sup, sdn).astype(BF16)
        k_ref[:, a:b] = kv[:, a:b].astype(BF16)
        k_ref[:, b:c] = kpe
        v_ref[:, hh * MLA_V_DIM:(hh + 1) * MLA_V_DIM] = kv[:, b:c].astype(BF16)

    cd, sd = cd_ref[...], sd_ref[...]
    off = MLA_ROPE_DIM
    pdq = rest[:, off:off + DIFF_QK_W] * DIFF_SCORE_SCALE
    for c in range(DIFF_QK_W // DIFF_HEAD_DIM):
        cols = slice(c * DIFF_HEAD_DIM, (c + 1) * DIFF_HEAD_DIM)
        dq_ref[:, cols] = _rope_half(pdq[:, cols], cd, sd).astype(BF16)
    pdk = rest[:, off + DIFF_QK_W:off + 2 * DIFF_QK_W]
    for c in range(DIFF_QK_W // DIFF_HEAD_DIM):
        cols = slice(c * DIFF_HEAD_DIM, (c + 1) * DIFF_HEAD_DIM)
        dk_ref[:, cols] = _rope_half(pdk[:, cols], cd, sd).astype(BF16)
    dv_ref[...] = rest[:, off + 2 * DIFF_QK_W:].astype(BF16)


def _mix_proj(x, g, w_in, qg, wuq, kvg, wukv, tables, seq):
    t, d = x.shape
    tm = PROJ_TM
    pos_blocks = seq // tm
    const = lambda i: (0, 0)
    row = lambda i: (i, 0)
    pos = lambda i: (i % pos_blocks, 0)
    wq = MLA_HEADS * MLA_HEAD_PAD
    out_w = (wq, wq, MLA_OUT, DIFF_QK_W, DIFF_QK_W, DIFF_OUT)
    once = dict(pipeline_mode=pl.Buffered(1))
    nbytes = (d * _IN_W * 2 + MLA_Q_RANK * wq * (4 + 4) + 2 * tm * d * 4
              + 2 * tm * sum(out_w) * 2 + tm * _IN_W * 4 + 2 * tm * wq * 4)
    return pl.pallas_call(
        _proj_kernel,
        grid=(t // tm,),
        in_specs=[
            pl.BlockSpec((tm, d), row),
            pl.BlockSpec((1, d), const),
            pl.BlockSpec((d, _IN_W), const, **once),
            pl.BlockSpec((1, MLA_Q_RANK), const),
            pl.BlockSpec((MLA_Q_RANK, MLA_HEADS * MLA_QK_DIM), const, **once),
            pl.BlockSpec((1, MLA_KV_RANK), const),
            pl.BlockSpec((MLA_KV_RANK, wq), const, **once),
        ] + [pl.BlockSpec((tm, V7X_LANES), pos)] * 5,
        out_specs=[pl.BlockSpec((tm, w), row) for w in out_w],
        out_shape=[jax.ShapeDtypeStruct((t, w), BF16) for w in out_w],
        compiler_params=pltpu.CompilerParams(
            dimension_semantics=("parallel",),
            vmem_limit_bytes=_vmem_limit(nbytes)),
        name="mix_proj",
    )(x, g, w_in.astype(BF16), qg, wuq, kvg, wukv, *tables)


def _causal_mask(n_q, n_k):
    r = lax.broadcasted_iota(jnp.int32, (n_q, n_k), 0) // CHUNK
    c = lax.broadcasted_iota(jnp.int32, (n_q, n_k), 1) // CHUNK
    return r >= c


def _flash_scratch(n_maps, v_dim):
    stat = pltpu.VMEM((ATT_TQ, V7X_LANES), F32)
    return [stat, stat, pltpu.VMEM((ATT_TQ, v_dim), F32)] * n_maps


def _flash_init(m_ref, l_ref, acc_ref):
    m_ref[...] = jnp.full(m_ref.shape, -jnp.inf, F32)
    l_ref[...] = jnp.zeros(l_ref.shape, F32)
    acc_ref[...] = jnp.zeros(acc_ref.shape, F32)


def _flash_step(t, v, m_ref, l_ref, acc_ref):
    lanes = m_ref.shape[1]
    m_old = m_ref[...]
    m_new = jnp.maximum(m_old, jnp.max(t, axis=-1, keepdims=True))
    alpha = jnp.exp2(m_old - m_new)
    p = jnp.exp2(t - jnp.tile(m_new, (1, t.shape[1] // lanes)))
    l_ref[...] = alpha * l_ref[...] + jnp.sum(p, axis=-1, keepdims=True)
    pv = jnp.dot(p.astype(BF16), v, preferred_element_type=F32)
    acc_ref[...] = jnp.tile(alpha, (1, acc_ref.shape[1] // lanes)) * acc_ref[...] + pv
    m_ref[...] = m_new


def _attn_tile(qi, n_maps, scores, values, state):
    half = ATT_TQ // 2
    assert ATT_TQ == ATT_TK and half % CHUNK == 0

    def update(q_rows, k_rows, mask):
        for n in range(n_maps):
            t = scores(n, q_rows, k_rows)
            if mask is not None:
                t = jnp.where(mask, t, -jnp.inf)
            _flash_step(t, values(n, k_rows), *[r.at[q_rows] for r in state[3 * n:3 * n + 3]])

    for n in range(n_maps):
        _flash_init(*state[3 * n:3 * n + 3])
    base = pl.multiple_of(qi * ATT_TQ, ATT_TQ)
    update(pl.ds(0, ATT_TQ), pl.ds(base, half), _causal_mask(ATT_TQ, half))
    update(pl.ds(half, half), pl.ds(pl.multiple_of(base + half, half), half),
           _causal_mask(half, half))

    def body(j, carry):
        update(pl.ds(0, ATT_TQ), pl.ds(pl.multiple_of(j * ATT_TK, ATT_TK), ATT_TK), None)
        return carry

    lax.fori_loop(0, qi, body, 0)


def _qk(q, k):
    return lax.dot_general(q, k, (((1,), (1,)), ((), ())), preferred_element_type=F32)


def _mla_attn_kernel(q_ref, k_ref, v_ref, o_ref, *state):
    wq, wv = MLA_HEAD_PAD, MLA_V_DIM

    def scores(g, q_rows, k_rows):
        return _qk(q_ref[q_rows, g * wq:(g + 1) * wq], k_ref[k_rows, g * wq:(g + 1) * wq])

    def values(g, k_rows):
        return v_ref[k_rows, g * wv:(g + 1) * wv]

    _attn_tile(pl.program_id(2), MLA_GROUP, scores, values, state)
    for g in range(MLA_GROUP):
        _, l_ref, acc_ref = state[3 * g:3 * g + 3]
        o_ref[:, g * wv:(g + 1) * wv] = (acc_ref[...] / l_ref[...]).astype(o_ref.dtype)


def _mla_attn(q, k, v, batch, seq):
    nq = seq // ATT_TQ
    wq, wv = MLA_GROUP * MLA_HEAD_PAD, MLA_GROUP * MLA_V_DIM
    return pl.pallas_call(
        _mla_attn_kernel,
        grid=(batch, MLA_HEADS // MLA_GROUP, nq),
        in_specs=[
            pl.BlockSpec((ATT_TQ, wq), lambda b, h, i: (b * nq + i, h)),
            pl.BlockSpec((seq, wq), lambda b, h, i: (b, h)),
            pl.BlockSpec((seq, wv), lambda b, h, i: (b, h)),
        ],
        out_specs=pl.BlockSpec((ATT_TQ, wv), lambda b, h, i: (b * nq + i, h)),
        out_shape=jax.ShapeDtypeStruct((batch * seq, MLA_OUT), BF16),
        scratch_shapes=_flash_scratch(MLA_GROUP, MLA_V_DIM),
        compiler_params=pltpu.CompilerParams(
            dimension_semantics=("parallel", "parallel", "parallel"),
            vmem_limit_bytes=_vmem_limit(2 * (seq + ATT_TQ) * (wq + wv) * 2
                                         + 8 * MLA_GROUP * ATT_TQ * ATT_TK * 4)),
        name="mla_attn",
    )(q, k, v)


def _diff_attn_kernel(q_ref, k_ref, v_ref, lq1_ref, lk1_ref, lq2_ref, lk2_ref, g_ref, o_ref,
                      *state):
    d, wv = DIFF_HEAD_DIM, DIFF_V_DIM

    def scores(n, q_rows, k_rows):
        return _qk(q_ref[q_rows, n * d:(n + 1) * d], k_ref[k_rows, n * d:(n + 1) * d])

    def values(n, k_rows):
        return v_ref[k_rows, (n // 2) * wv:(n // 2 + 1) * wv]

    _attn_tile(pl.program_id(2), 2 * DIFF_GROUP, scores, values, state)
    lam = (jnp.exp(jnp.sum(lq1_ref[...] * lk1_ref[...], axis=-1, keepdims=True))
           - jnp.exp(jnp.sum(lq2_ref[...] * lk2_ref[...], axis=-1, keepdims=True))
           + LAMBDA_INIT)
    rep = wv // V7X_LANES
    for h in range(DIFF_GROUP):
        _, l1_ref, a1_ref, _, l2_ref, a2_ref = state[6 * h:6 * h + 6]
        o = (a1_ref[...] / jnp.tile(l1_ref[...], (1, rep))
             - lam * (a2_ref[...] / jnp.tile(l2_ref[...], (1, rep))))
        o_ref[:, h * wv:(h + 1) * wv] = (
            _rms(o, g_ref[...]) * (1.0 - LAMBDA_INIT)).astype(o_ref.dtype)


def _diff_attn(q, k, v, lq1, lk1, lq2, lk2, g, batch, seq):
    nq = seq // ATT_TQ
    w = DIFF_GROUP * DIFF_V_DIM
    vec = pl.BlockSpec((1, DIFF_HEAD_DIM), lambda b, h, i: (0, 0))
    return pl.pallas_call(
        _diff_attn_kernel,
        grid=(batch, DIFF_HEADS // DIFF_GROUP, nq),
        in_specs=[
            pl.BlockSpec((ATT_TQ, w), lambda b, h, i: (b * nq + i, h)),
            pl.BlockSpec((seq, w), lambda b, h, i: (b, h)),
            pl.BlockSpec((seq, w), lambda b, h, i: (b, h)),
            vec, vec, vec, vec,
            pl.BlockSpec((1, DIFF_V_DIM), lambda b, h, i: (0, 0)),
        ],
        out_specs=pl.BlockSpec((ATT_TQ, w), lambda b, h, i: (b * nq + i, h)),
        out_shape=jax.ShapeDtypeStruct((batch * seq, DIFF_OUT), BF16),
        scratch_shapes=_flash_scratch(2 * DIFF_GROUP, DIFF_V_DIM),
        compiler_params=pltpu.CompilerParams(
            dimension_semantics=("parallel", "parallel", "parallel"),
            vmem_limit_bytes=_vmem_limit(2 * (2 * seq + 2 * ATT_TQ) * w * 2
                                         + 16 * DIFF_GROUP * ATT_TQ * ATT_TK * 4)),
        name="diff_attn",
    )(q, k, v, lq1, lk1, lq2, lk2, g)


def _out_kernel(x_ref, oa_ref, ob_ref, wa_ref, wb_ref, g_ref, o_ref):
    o = _dot_bf16(oa_ref[...], wa_ref[...]) + _dot_bf16(ob_ref[...], wb_ref[...])
    o_ref[...] = x_ref[...] + _rms(o, g_ref[...])


def _out_proj(x, oa, ob, w, g):
    t, d = x.shape
    assert MLA_OUT == DIFF_OUT, "the two row blocks of w share one block shape"
    tm = OUT_TM
    const = lambda i: (0, 0)
    row = lambda i: (i, 0)
    once = dict(pipeline_mode=pl.Buffered(1))
    nbytes = (d * d * (w.dtype.itemsize + 2) + 4 * tm * d * 4 + 2 * tm * d * 2 + 2 * tm * d * 4)
    return pl.pallas_call(
        _out_kernel,
        grid=(t // tm,),
        in_specs=[
            pl.BlockSpec((tm, d), row),
            pl.BlockSpec((tm, MLA_OUT), row),
            pl.BlockSpec((tm, DIFF_OUT), row),
            pl.BlockSpec((MLA_OUT, d), const, **once),
            pl.BlockSpec((DIFF_OUT, d), lambda i: (1, 0), **once),
            pl.BlockSpec((1, d), const),
        ],
        out_specs=pl.BlockSpec((tm, d), row),
        out_shape=jax.ShapeDtypeStruct((t, d), F32),
        compiler_params=pltpu.CompilerParams(
            dimension_semantics=("parallel",),
            vmem_limit_bytes=_vmem_limit(nbytes)),
        name="out_proj",
    )(x, oa, ob, w, w, g)


def _rope_tables(seq):
    pos = np.arange(seq, dtype=np.float64)[:, None]

    def angles(d):
        inv_freq = ROPE_THETA ** (-np.arange(0, d, 2, dtype=np.float64) / d)
        return pos * inv_freq[None, :]

    ang = angles(MLA_ROPE_DIM)
    zeros = np.zeros((seq, V7X_LANES - MLA_ROPE_DIM))
    half0 = np.zeros_like(ang)
    cos_m = np.concatenate([np.cos(ang), np.cos(ang), zeros], axis=1)
    sin_up = np.concatenate([half0, np.sin(ang), zeros], axis=1)
    sin_dn = np.concatenate([-np.sin(ang), half0, zeros], axis=1)
    ang = angles(DIFF_HEAD_DIM)
    cos_d = np.concatenate([np.cos(ang), np.cos(ang)], axis=1)
    sin_d = np.concatenate([-np.sin(ang), np.sin(ang)], axis=1)
    return tuple(jnp.asarray(t, F32) for t in (cos_m, sin_up, sin_dn, cos_d, sin_d))


def kernel(x, ffn1_pre_g, ffn1_w_gate, ffn1_w_up, ffn1_w_down, ffn1_post_g, mix_pre_g, w_in, mla_q_norm_g, mla_w_uq, mla_kv_norm_g, mla_w_ukv, diff_lambda_q1, diff_lambda_k1, diff_lambda_q2, diff_lambda_k2, diff_subln_g, w_out, mix_post_g, ffn2_pre_g, ffn2_w_gate, ffn2_w_up, ffn2_w_down, ffn2_post_g):
    batch, seq, d = x.shape
    depth = ffn1_pre_g.shape[0]
    assert depth == 1 and d == D_MODEL
    assert seq % ATT_TQ == 0 and seq % PROJ_TM == 0 and (batch * seq) % FFN_TM == 0
    tables = _rope_tables(seq)
    xt = x.reshape(batch * seq, d)
    for l in range(depth):
        xt = _ffn(xt, ffn1_pre_g[l][None], ffn1_w_gate[l], ffn1_w_up[l], ffn1_w_down[l],
                  ffn1_post_g[l][None])
        q, k, v, dq, dk, dv = _mix_proj(
            xt, mix_pre_g[l][None], w_in[l], mla_q_norm_g[l][None],
            mla_w_uq[l], mla_kv_norm_g[l][None], mla_w_ukv[l],
            tables, seq)
        o_mla = _mla_attn(q, k, v, batch, seq)
        o_diff = _diff_attn(dq, dk, dv, diff_lambda_q1[l][None], diff_lambda_k1[l][None],
                            diff_lambda_q2[l][None], diff_lambda_k2[l][None],
                            diff_subln_g[l][None], batch, seq)
        xt = _out_proj(xt, o_mla, o_diff, w_out[l], mix_post_g[l][None])
        xt = _ffn(xt, ffn2_pre_g[l][None], ffn2_w_gate[l], ffn2_w_up[l], ffn2_w_down[l],
                  ffn2_post_g[l][None])
    return xt.reshape(batch, seq, d)
```

```python
import functools
import math

import jax
import jax.numpy as jnp
import numpy as np
from jax import lax
from jax.experimental import pallas as pl
from jax.experimental.pallas import tpu as pltpu

D_MODEL = 2048
CHUNK = 64
ROPE_THETA = 10000.0
NORM_EPS = 1e-6
MLA_HEADS = 8
MLA_Q_RANK = 512
MLA_KV_RANK = 512
MLA_NOPE_DIM = 128
MLA_ROPE_DIM = 64
MLA_V_DIM = 128
MLA_QK_DIM = MLA_NOPE_DIM + MLA_ROPE_DIM
DIFF_HEADS = 4
DIFF_HEAD_DIM = 128
DIFF_V_DIM = 2 * DIFF_HEAD_DIM
MLA_OUT = MLA_HEADS * MLA_V_DIM
DIFF_OUT = DIFF_HEADS * DIFF_V_DIM
DIFF_QK_W = DIFF_HEADS * 2 * DIFF_HEAD_DIM
D_FF = 5632
LAMBDA_INIT = 0.8 - 0.6 * math.exp(-0.3 * 0)
LOG2_E = math.log2(math.e)
MLA_SCORE_SCALE = MLA_QK_DIM ** -0.5 * LOG2_E
DIFF_SCORE_SCALE = DIFF_HEAD_DIM ** -0.5 * LOG2_E

V7X_LANES = 128
V7X_VMEM_BYTES = 64 * 1024 * 1024
MLA_HEAD_PAD = 2 * V7X_LANES

FFN_TM = 1024
FFN_TF = 512
FFN_ROWS = 256
FFN_CAST_ROWS = 32
PROJ_TM = 512
ATT_TQ = 512
ATT_TK = 512
MLA_GROUP = 8
DIFF_GROUP = 4
OUT_TM = 512

BF16 = jnp.bfloat16
F32 = jnp.float32


def _vmem_limit(nbytes):
    return int(min(V7X_VMEM_BYTES - 4 * 1024 * 1024, max(32 * 1024 * 1024, nbytes * 3 // 2)))


def _dot_bf16(a, w):
    return lax.dot_general(a, w, (((1,), (0,)), ((), ())), preferred_element_type=F32)


def _rms(v, g):
    ms = jnp.mean(v * v, axis=-1, keepdims=True)
    return v * lax.rsqrt(ms + NORM_EPS) * g


def _swiglu_hidden(h, wg, wu):
    g = _dot_bf16(h, wg)
    u = _dot_bf16(h, wu)
    return (g * (1.0 / (1.0 + jnp.exp(-g))) * u).astype(BF16)


def _ffn_kernel(*refs, n_cast_slabs):
    if n_cast_slabs:
        (x_ref, pre_g_ref, wg_ref, wu_ref, wd_ref, post_g_ref, src_ref,
         o_hbm, dst_ref, acc_ref, h_ref, sem) = refs
    else:
        x_ref, pre_g_ref, wg_ref, wu_ref, wd_ref, post_g_ref, o_hbm, acc_ref, h_ref, sem = refs
    i = pl.program_id(0)
    f = pl.program_id(1)
    last = pl.num_programs(1) - 1
    if n_cast_slabs:
        @pl.when(i * pl.num_programs(1) + f < n_cast_slabs)
        def _():
            dst_ref[...] = src_ref[...].astype(BF16)
    n_chunks = FFN_TM // FFN_ROWS
    chunks = [pl.ds(c * FFN_ROWS, FFN_ROWS) for c in range(n_chunks)]

    def out_copy(tile, c):
        dst = o_hbm.at[pl.ds(pl.multiple_of(tile * FFN_TM + c * FFN_ROWS, FFN_ROWS), FFN_ROWS)]
        return pltpu.make_async_copy(acc_ref.at[chunks[c]], dst, sem.at[c])

    @pl.when(jnp.logical_and(f == 0, i > 0))
    def _():
        for c in range(n_chunks):
            out_copy(i - 1, c).wait()

    @pl.when(f == 0)
    def _():
        for rows in chunks:
            h = _rms(x_ref[rows, :], pre_g_ref[...]).astype(BF16)
            h_ref[rows, :] = h
            acc_ref[rows, :] = _dot_bf16(_swiglu_hidden(h, wg_ref[...], wu_ref[...]), wd_ref[...])

    @pl.when(jnp.logical_and(f > 0, f < last))
    def _():
        a = _swiglu_hidden(h_ref[...], wg_ref[...], wu_ref[...])
        acc_ref[...] += _dot_bf16(a, wd_ref[...])

    @pl.when(f == last)
    def _():
        for c, rows in enumerate(chunks):
            a = _swiglu_hidden(h_ref[rows, :], wg_ref[...], wu_ref[...])
            acc = acc_ref[rows, :] + _dot_bf16(a, wd_ref[...])
            acc_ref[rows, :] = x_ref[rows, :] + 0.5 * _rms(acc, post_g_ref[...])
            out_copy(i, c).start()

        @pl.when(i == pl.num_programs(0) - 1)
        def _():
            for c in range(n_chunks):
                out_copy(i, c).wait()


def _ffn(x, pre_g, wg, wu, wd, post_g, cast_src=None):
    t, d = x.shape
    n_f = D_FF // FFN_TF
    assert n_f >= 2, "first and last hidden steps must be distinct"
    nbytes = (3 * FFN_TM * d * 4 + FFN_TM * d * 2 + 2 * 3 * d * FFN_TF * wg.dtype.itemsize
              + 3 * FFN_TM * FFN_TF * 4)
    in_specs = [
        pl.BlockSpec((FFN_TM, d), lambda i, f: (i, 0)),
        pl.BlockSpec((1, d), lambda i, f: (0, 0)),
        pl.BlockSpec((d, FFN_TF), lambda i, f: (0, f)),
        pl.BlockSpec((d, FFN_TF), lambda i, f: (0, f)),
        pl.BlockSpec((FFN_TF, d), lambda i, f: (f, 0)),
        pl.BlockSpec((1, d), lambda i, f: (0, 0)),
    ]
    out_specs = [pl.BlockSpec(memory_space=pl.ANY)]
    out_shape = [jax.ShapeDtypeStruct((t, d), F32)]
    operands = [x, pre_g, wg, wu, wd, post_g]
    n_slabs = 0
    if cast_src is not None:
        rows, cols = cast_src.shape
        n_slabs = rows // FFN_CAST_ROWS
        assert rows % FFN_CAST_ROWS == 0 and n_slabs <= (t // FFN_TM) * n_f
        slab = pl.BlockSpec((FFN_CAST_ROWS, cols),
                            lambda i, f: (jnp.minimum(i * n_f + f, n_slabs - 1), 0))
        in_specs.append(slab)
        out_specs.append(slab)
        out_shape.append(jax.ShapeDtypeStruct((rows, cols), BF16))
        operands.append(cast_src)
        nbytes += 2 * FFN_CAST_ROWS * cols * (4 + 2)
    outs = pl.pallas_call(
        functools.partial(_ffn_kernel, n_cast_slabs=n_slabs),
        grid=(t // FFN_TM, n_f),
        in_specs=in_specs,
        out_specs=out_specs,
        out_shape=out_shape,
        scratch_shapes=[pltpu.VMEM((FFN_TM, d), F32), pltpu.VMEM((FFN_TM, d), BF16),
                        pltpu.SemaphoreType.DMA((FFN_TM // FFN_ROWS,))],
        compiler_params=pltpu.CompilerParams(
            dimension_semantics=("arbitrary", "arbitrary"),
            vmem_limit_bytes=_vmem_limit(nbytes)),
        name="ffn",
    )(*operands)
    return outs[0] if cast_src is None else tuple(outs)


_IN_LATENT = MLA_Q_RANK + MLA_KV_RANK
_IN_W = _IN_LATENT + MLA_ROPE_DIM + 2 * DIFF_QK_W + DIFF_OUT


def _rope_half(v, cos, sin_signed):
    return v * cos + pltpu.roll(v, DIFF_HEAD_DIM // 2, 1) * sin_signed


def _rope_mla(v, cos, sin_up, sin_dn):
    half = MLA_ROPE_DIM // 2
    return (v * cos + pltpu.roll(v, half, 1) * sin_up
            + pltpu.roll(v, V7X_LANES - half, 1) * sin_dn)


def _proj_kernel(x_ref, g_ref, win_ref, qg_ref, wuq_ref, kvg_ref, wukv_ref,
                 cm_ref, sup_ref, sdn_ref, cd_ref, sd_ref,
                 q_ref, k_ref, v_ref, dq_ref, dk_ref, dv_ref):
    h = _rms(x_ref[...], g_ref[...]).astype(BF16)

    pa = _dot_bf16(h, win_ref[:, 0:_IN_LATENT])
    cq = _rms(pa[:, 0:MLA_Q_RANK], qg_ref[...]).astype(BF16)
    ckv = _rms(pa[:, MLA_Q_RANK:_IN_LATENT], kvg_ref[...]).astype(BF16)
    rest = _dot_bf16(h, win_ref[:, _IN_LATENT:_IN_W])
    cm, sup, sdn = cm_ref[...], sup_ref[...], sdn_ref[...]
    kpe = _rope_mla(rest[:, 0:V7X_LANES], cm, sup, sdn).astype(BF16)

    q = _dot_bf16(cq, wuq_ref[...]) * MLA_SCORE_SCALE
    kv = _dot_bf16(ckv, wukv_ref[...])
    for hh in range(MLA_HEADS):
        a = hh * MLA_HEAD_PAD
        b = a + V7X_LANES
        c = b + V7X_LANES
        qa = hh * MLA_QK_DIM
        q_ref[:, a:b] = q[:, qa:qa + MLA_NOPE_DIM].astype(BF16)
        if (qa + MLA_NOPE_DIM) % V7X_LANES == 0:
            pe = q[:, qa + MLA_NOPE_DIM:qa + MLA_NOPE_DIM + V7X_LANES]
        else:
            pe = pltpu.roll(q[:, qa + MLA_ROPE_DIM:qa + MLA_QK_DIM], MLA_ROPE_DIM, 1)
        q_ref[:, b:c] = _rope_mla(pe, cm, sup, sdn).astype(BF16)
        k_ref[:, a:b] = kv[:, a:b].astype(BF16)
        k_ref[:, b:c] = kpe
        v_ref[:, hh * MLA_V_DIM:(hh + 1) * MLA_V_DIM] = kv[:, b:c].astype(BF16)

    cd, sd = cd_ref[...], sd_ref[...]
    off = MLA_ROPE_DIM
    pdq = rest[:, off:off + DIFF_QK_W] * DIFF_SCORE_SCALE
    for c in range(DIFF_QK_W // DIFF_HEAD_DIM):
        cols = slice(c * DIFF_HEAD_DIM, (c + 1) * DIFF_HEAD_DIM)
        dq_ref[:, cols] = _rope_half(pdq[:, cols], cd, sd).astype(BF16)
    pdk = rest[:, off + DIFF_QK_W:off + 2 * DIFF_QK_W]
    for c in range(DIFF_QK_W // DIFF_HEAD_DIM):
        cols = slice(c * DIFF_HEAD_DIM, (c + 1) * DIFF_HEAD_DIM)
        dk_ref[:, cols] = _rope_half(pdk[:, cols], cd, sd).astype(BF16)
    dv_ref[...] = rest[:, off + 2 * DIFF_QK_W:].astype(BF16)


def _mix_proj(x, g, w_in, qg, wuq, kvg, wukv, tables, seq):
    t, d = x.shape
    tm = PROJ_TM
    pos_blocks = seq // tm
    const = lambda i: (0, 0)
    row = lambda i: (i, 0)
    pos = lambda i: (i % pos_blocks, 0)
    wq = MLA_HEADS * MLA_HEAD_PAD
    out_w = (wq, wq, MLA_OUT, DIFF_QK_W, DIFF_QK_W, DIFF_OUT)
    once = dict(pipeline_mode=pl.Buffered(1))
    nbytes = (d * _IN_W * 2 + MLA_Q_RANK * wq * (4 + 4) + 2 * tm * d * 4
              + 2 * tm * sum(out_w) * 2 + tm * _IN_W * 4 + 2 * tm * wq * 4)
    return pl.pallas_call(
        _proj_kernel,
        grid=(t // tm,),
        in_specs=[
            pl.BlockSpec((tm, d), row),
            pl.BlockSpec((1, d), const),
            pl.BlockSpec((d, _IN_W), const, **once),
            pl.BlockSpec((1, MLA_Q_RANK), const),
            pl.BlockSpec((MLA_Q_RANK, MLA_HEADS * MLA_QK_DIM), const, **once),
            pl.BlockSpec((1, MLA_KV_RANK), const),
            pl.BlockSpec((MLA_KV_RANK, wq), const, **once),
        ] + [pl.BlockSpec((tm, V7X_LANES), pos)] * 5,
        out_specs=[pl.BlockSpec((tm, w), row) for w in out_w],
        out_shape=[jax.ShapeDtypeStruct((t, w), BF16) for w in out_w],
        compiler_params=pltpu.CompilerParams(
            dimension_semantics=("parallel",),
            vmem_limit_bytes=_vmem_limit(nbytes)),
        name="mix_proj",
    )(x, g, w_in, qg, wuq, kvg, wukv, *tables)


def _causal_mask(n_q, n_k):
    r = lax.broadcasted_iota(jnp.int32, (n_q, n_k), 0) // CHUNK
    c = lax.broadcasted_iota(jnp.int32, (n_q, n_k), 1) // CHUNK
    return r >= c


def _flash_scratch(n_maps, v_dim):
    stat = pltpu.VMEM((ATT_TQ, V7X_LANES), F32)
    return [stat, stat, pltpu.VMEM((ATT_TQ, v_dim), F32)] * n_maps


def _flash_init(m_ref, l_ref, acc_ref):
    m_ref[...] = jnp.full(m_ref.shape, -jnp.inf, F32)
    l_ref[...] = jnp.zeros(l_ref.shape, F32)
    acc_ref[...] = jnp.zeros(acc_ref.shape, F32)


def _flash_step(t, v, m_ref, l_ref, acc_ref):
    lanes = m_ref.shape[1]
    m_old = m_ref[...]
    m_new = jnp.maximum(m_old, jnp.max(t, axis=-1, keepdims=True))
    alpha = jnp.exp2(m_old - m_new)
    p = jnp.exp2(t - jnp.tile(m_new, (1, t.shape[1] // lanes)))
    l_ref[...] = alpha * l_ref[...] + jnp.sum(p, axis=-1, keepdims=True)
    pv = jnp.dot(p.astype(BF16), v, preferred_element_type=F32)
    acc_ref[...] = jnp.tile(alpha, (1, acc_ref.shape[1] // lanes)) * acc_ref[...] + pv
    m_ref[...] = m_new


def _attn_tile(qi, n_maps, scores, values, state):
    half = ATT_TQ // 2
    assert ATT_TQ == ATT_TK and half % CHUNK == 0

    def update(q_rows, k_rows, mask):
        for n in range(n_maps):
            t = scores(n, q_rows, k_rows)
            if mask is not None:
                t = jnp.where(mask, t, -jnp.inf)
            _flash_step(t, values(n, k_rows), *[r.at[q_rows] for r in state[3 * n:3 * n + 3]])

    for n in range(n_maps):
        _flash_init(*state[3 * n:3 * n + 3])
    base = pl.multiple_of(qi * ATT_TQ, ATT_TQ)
    update(pl.ds(0, ATT_TQ), pl.ds(base, half), _causal_mask(ATT_TQ, half))
    update(pl.ds(half, half), pl.ds(pl.multiple_of(base + half, half), half),
           _causal_mask(half, half))

    def body(j, carry):
        update(pl.ds(0, ATT_TQ), pl.ds(pl.multiple_of(j * ATT_TK, ATT_TK), ATT_TK), None)
        return carry

    lax.fori_loop(0, qi, body, 0)


def _qk(q, k):
    return lax.dot_general(q, k, (((1,), (1,)), ((), ())), preferred_element_type=F32)


def _mla_attn_kernel(q_ref, k_ref, v_ref, o_ref, *state):
    wq, wv = MLA_HEAD_PAD, MLA_V_DIM

    def scores(g, q_rows, k_rows):
        return _qk(q_ref[q_rows, g * wq:(g + 1) * wq], k_ref[k_rows, g * wq:(g + 1) * wq])

    def values(g, k_rows):
        return v_ref[k_rows, g * wv:(g + 1) * wv]

    _attn_tile(pl.program_id(2), MLA_GROUP, scores, values, state)
    for g in range(MLA_GROUP):
        _, l_ref, acc_ref = state[3 * g:3 * g + 3]
        o_ref[:, g * wv:(g + 1) * wv] = (acc_ref[...] / l_ref[...]).astype(o_ref.dtype)


def _mla_attn(q, k, v, batch, seq):
    nq = seq // ATT_TQ
    wq, wv = MLA_GROUP * MLA_HEAD_PAD, MLA_GROUP * MLA_V_DIM
    return pl.pallas_call(
        _mla_attn_kernel,
        grid=(batch, MLA_HEADS // MLA_GROUP, nq),
        in_specs=[
            pl.BlockSpec((ATT_TQ, wq), lambda b, h, i: (b * nq + i, h)),
            pl.BlockSpec((seq, wq), lambda b, h, i: (b, h)),
            pl.BlockSpec((seq, wv), lambda b, h, i: (b, h)),
        ],
        out_specs=pl.BlockSpec((ATT_TQ, wv), lambda b, h, i: (b * nq + i, h)),
        out_shape=jax.ShapeDtypeStruct((batch * seq, MLA_OUT), BF16),
        scratch_shapes=_flash_scratch(MLA_GROUP, MLA_V_DIM),
        compiler_params=pltpu.CompilerParams(
            dimension_semantics=("parallel", "parallel", "parallel"),
            vmem_limit_bytes=_vmem_limit(2 * (seq + ATT_TQ) * (wq + wv) * 2
                                         + 8 * MLA_GROUP * ATT_TQ * ATT_TK * 4)),
        name="mla_attn",
    )(q, k, v)


def _diff_attn_kernel(q_ref, k_ref, v_ref, lq1_ref, lk1_ref, lq2_ref, lk2_ref, g_ref, o_ref,
                      *state):
    d, wv = DIFF_HEAD_DIM, DIFF_V_DIM

    def scores(n, q_rows, k_rows):
        return _qk(q_ref[q_rows, n * d:(n + 1) * d], k_ref[k_rows, n * d:(n + 1) * d])

    def values(n, k_rows):
        return v_ref[k_rows, (n // 2) * wv:(n // 2 + 1) * wv]

    _attn_tile(pl.program_id(2), 2 * DIFF_GROUP, scores, values, state)
    lam = (jnp.exp(jnp.sum(lq1_ref[...] * lk1_ref[...], axis=-1, keepdims=True))
           - jnp.exp(jnp.sum(lq2_ref[...] * lk2_ref[...], axis=-1, keepdims=True))
           + LAMBDA_INIT)
    rep = wv // V7X_LANES
    for h in range(DIFF_GROUP):
        _, l1_ref, a1_ref, _, l2_ref, a2_ref = state[6 * h:6 * h + 6]
        o = (a1_ref[...] / jnp.tile(l1_ref[...], (1, rep))
             - lam * (a2_ref[...] / jnp.tile(l2_ref[...], (1, rep))))
        o_ref[:, h * wv:(h + 1) * wv] = (
            _rms(o, g_ref[...]) * (1.0 - LAMBDA_INIT)).astype(o_ref.dtype)


def _diff_attn(q, k, v, lq1, lk1, lq2, lk2, g, batch, seq):
    nq = seq // ATT_TQ
    w = DIFF_GROUP * DIFF_V_DIM
    vec = pl.BlockSpec((1, DIFF_HEAD_DIM), lambda b, h, i: (0, 0))
    return pl.pallas_call(
        _diff_attn_kernel,
        grid=(batch, DIFF_HEADS // DIFF_GROUP, nq),
        in_specs=[
            pl.BlockSpec((ATT_TQ, w), lambda b, h, i: (b * nq + i, h)),
            pl.BlockSpec((seq, w), lambda b, h, i: (b, h)),
            pl.BlockSpec((seq, w), lambda b, h, i: (b, h)),
            vec, vec, vec, vec,
            pl.BlockSpec((1, DIFF_V_DIM), lambda b, h, i: (0, 0)),
        ],
        out_specs=pl.BlockSpec((ATT_TQ, w), lambda b, h, i: (b * nq + i, h)),
        out_shape=jax.ShapeDtypeStruct((batch * seq, DIFF_OUT), BF16),
        scratch_shapes=_flash_scratch(2 * DIFF_GROUP, DIFF_V_DIM),
        compiler_params=pltpu.CompilerParams(
            dimension_semantics=("parallel", "parallel", "parallel"),
            vmem_limit_bytes=_vmem_limit(2 * (2 * seq + 2 * ATT_TQ) * w * 2
                                         + 16 * DIFF_GROUP * ATT_TQ * ATT_TK * 4)),
        name="diff_attn",
    )(q, k, v, lq1, lk1, lq2, lk2, g)


def _out_kernel(x_ref, oa_ref, ob_ref, wa_ref, wb_ref, g_ref, o_ref):
    o = _dot_bf16(oa_ref[...], wa_ref[...]) + _dot_bf16(ob_ref[...], wb_ref[...])
    o_ref[...] = x_ref[...] + _rms(o, g_ref[...])


def _out_proj(x, oa, ob, w, g):
    t, d = x.shape
    assert MLA_OUT == DIFF_OUT, "the two row blocks of w share one block shape"
    tm = OUT_TM
    const = lambda i: (0, 0)
    row = lambda i: (i, 0)
    once = dict(pipeline_mode=pl.Buffered(1))
    nbytes = (d * d * (w.dtype.itemsize + 2) + 4 * tm * d * 4 + 2 * tm * d * 2 + 2 * tm * d * 4)
    return pl.pallas_call(
        _out_kernel,
        grid=(t // tm,),
        in_specs=[
            pl.BlockSpec((tm, d), row),
            pl.BlockSpec((tm, MLA_OUT), row),
            pl.BlockSpec((tm, DIFF_OUT), row),
            pl.BlockSpec((MLA_OUT, d), const, **once),
            pl.BlockSpec((DIFF_OUT, d), lambda i: (1, 0), **once),
            pl.BlockSpec((1, d), const),
        ],
        out_specs=pl.BlockSpec((tm, d), row),
        out_shape=jax.ShapeDtypeStruct((t, d), F32),
        compiler_params=pltpu.CompilerParams(
            dimension_semantics=("parallel",),
            vmem_limit_bytes=_vmem_limit(nbytes)),
        name="out_proj",
    )(x, oa, ob, w, w, g)


def _rope_tables(seq):
    pos = np.arange(seq, dtype=np.float64)[:, None]

    def angles(d):
        inv_freq = ROPE_THETA ** (-np.arange(0, d, 2, dtype=np.float64) / d)
        return pos * inv_freq[None, :]

    ang = angles(MLA_ROPE_DIM)
    zeros = np.zeros((seq, V7X_LANES - MLA_ROPE_DIM))
    half0 = np.zeros_like(ang)
    cos_m = np.concatenate([np.cos(ang), np.cos(ang), zeros], axis=1)
    sin_up = np.concatenate([half0, np.sin(ang), zeros], axis=1)
    sin_dn = np.concatenate([-np.sin(ang), half0, zeros], axis=1)
    ang = angles(DIFF_HEAD_DIM)
    cos_d = np.concatenate([np.cos(ang), np.cos(ang)], axis=1)
    sin_d = np.concatenate([-np.sin(ang), np.sin(ang)], axis=1)
    return tuple(jnp.asarray(t, F32) for t in (cos_m, sin_up, sin_dn, cos_d, sin_d))


def kernel(x, ffn1_pre_g, ffn1_w_gate, ffn1_w_up, ffn1_w_down, ffn1_post_g, mix_pre_g, w_in, mla_q_norm_g, mla_w_uq, mla_kv_norm_g, mla_w_ukv, diff_lambda_q1, diff_lambda_k1, diff_lambda_q2, diff_lambda_k2, diff_subln_g, w_out, mix_post_g, ffn2_pre_g, ffn2_w_gate, ffn2_w_up, ffn2_w_down, ffn2_post_g):
    batch, seq, d = x.shape
    depth = ffn1_pre_g.shape[0]
    assert depth == 1 and d == D_MODEL
    assert seq % ATT_TQ == 0 and seq % PROJ_TM == 0 and (batch * seq) % FFN_TM == 0
    tables = _rope_tables(seq)
    xt = x.reshape(batch * seq, d)
    for l in range(depth):
        xt, w_in_bf = _ffn(xt, ffn1_pre_g[l][None], ffn1_w_gate[l], ffn1_w_up[l], ffn1_w_down[l],
                           ffn1_post_g[l][None], cast_src=w_in[l])
        q, k, v, dq, dk, dv = _mix_proj(
            xt, mix_pre_g[l][None], w_in_bf, mla_q_norm_g[l][None],
            mla_w_uq[l], mla_kv_norm_g[l][None], mla_w_ukv[l],
            tables, seq)
        o_mla = _mla_attn(q, k, v, batch, seq)
        o_diff = _diff_attn(dq, dk, dv, diff_lambda_q1[l][None], diff_lambda_k1[l][None],
                            diff_lambda_q2[l][None], diff_lambda_k2[l][None],
                            diff_subln_g[l][None], batch, seq)
        xt = _out_proj(xt, o_mla, o_diff, w_out[l], mix_post_g[l][None])
        xt = _ffn(xt, ffn2_pre_g[l][None], ffn2_w_gate[l], ffn2_w_up[l], ffn2_w_down[l],
                  ffn2_post_g[l][None])
    return xt.reshape(batch, seq, d)
```

```python
import functools
import math

import jax
import jax.numpy as jnp
import numpy as np
from jax import lax
from jax.experimental import pallas as pl
from jax.experimental.pallas import tpu as pltpu

D_MODEL = 2048
CHUNK = 64
ROPE_THETA = 10000.0
NORM_EPS = 1e-6
MLA_HEADS = 8
MLA_Q_RANK = 512
MLA_KV_RANK = 512
MLA_NOPE_DIM = 128
MLA_ROPE_DIM = 64
MLA_V_DIM = 128
MLA_QK_DIM = MLA_NOPE_DIM + MLA_ROPE_DIM
DIFF_HEADS = 4
DIFF_HEAD_DIM = 128
DIFF_V_DIM = 2 * DIFF_HEAD_DIM
MLA_OUT = MLA_HEADS * MLA_V_DIM
DIFF_OUT = DIFF_HEADS * DIFF_V_DIM
DIFF_QK_W = DIFF_HEADS * 2 * DIFF_HEAD_DIM
D_FF = 5632
LAMBDA_INIT = 0.8 - 0.6 * math.exp(-0.3 * 0)
LOG2_E = math.log2(math.e)
MLA_SCORE_SCALE = MLA_QK_DIM ** -0.5 * LOG2_E
DIFF_SCORE_SCALE = DIFF_HEAD_DIM ** -0.5 * LOG2_E

V7X_LANES = 128
V7X_VMEM_BYTES = 64 * 1024 * 1024
MLA_HEAD_PAD = 2 * V7X_LANES

FFN_TM = 1024
FFN_TF = 512
FFN_ROWS = 256
FFN_CAST_ROWS = 64
PROJ_TM = 512
ATT_TQ = 512
ATT_TK = 512
MLA_GROUP = 8
DIFF_GROUP = 4
OUT_TM = 512

BF16 = jnp.bfloat16
F32 = jnp.float32


def _vmem_limit(nbytes):
    return int(min(V7X_VMEM_BYTES - 4 * 1024 * 1024, max(32 * 1024 * 1024, nbytes * 3 // 2)))


def _dot_bf16(a, w):
    return lax.dot_general(a, w, (((1,), (0,)), ((), ())), preferred_element_type=F32)


def _dot_nt(a, w_t):
    return lax.dot_general(a, w_t, (((1,), (1,)), ((), ())), preferred_element_type=F32)


def _rms(v, g):
    ms = jnp.mean(v * v, axis=-1, keepdims=True)
    return v * lax.rsqrt(ms + NORM_EPS) * g


def _swiglu_hidden(h, wg, wu):
    g = _dot_bf16(h, wg)
    u = _dot_bf16(h, wu)
    return (g * (1.0 / (1.0 + jnp.exp(-g))) * u).astype(BF16)


def _ffn_kernel(*refs, n_cast_slabs):
    if n_cast_slabs:
        (x_ref, pre_g_ref, wg_ref, wu_ref, wd_ref, post_g_ref, src_ref,
         o_hbm, dst_ref, acc_ref, h_ref, sem) = refs
    else:
        x_ref, pre_g_ref, wg_ref, wu_ref, wd_ref, post_g_ref, o_hbm, acc_ref, h_ref, sem = refs
    i = pl.program_id(0)
    f = pl.program_id(1)
    last = pl.num_programs(1) - 1
    if n_cast_slabs:
        @pl.when(i * pl.num_programs(1) + f < n_cast_slabs)
        def _():
            dst_ref[...] = src_ref[...].astype(BF16)
    n_chunks = FFN_TM // FFN_ROWS
    chunks = [pl.ds(c * FFN_ROWS, FFN_ROWS) for c in range(n_chunks)]

    def out_copy(tile, c):
        dst = o_hbm.at[pl.ds(pl.multiple_of(tile * FFN_TM + c * FFN_ROWS, FFN_ROWS), FFN_ROWS)]
        return pltpu.make_async_copy(acc_ref.at[chunks[c]], dst, sem.at[c])

    @pl.when(jnp.logical_and(f == 0, i > 0))
    def _():
        for c in range(n_chunks):
            out_copy(i - 1, c).wait()

    @pl.when(f == 0)
    def _():
        for rows in chunks:
            h = _rms(x_ref[rows, :], pre_g_ref[...]).astype(BF16)
            h_ref[rows, :] = h
            acc_ref[rows, :] = _dot_bf16(_swiglu_hidden(h, wg_ref[...], wu_ref[...]), wd_ref[...])

    @pl.when(jnp.logical_and(f > 0, f < last))
    def _():
        a = _swiglu_hidden(h_ref[...], wg_ref[...], wu_ref[...])
        acc_ref[...] += _dot_bf16(a, wd_ref[...])

    @pl.when(f == last)
    def _():
        for c, rows in enumerate(chunks):
            a = _swiglu_hidden(h_ref[rows, :], wg_ref[...], wu_ref[...])
            acc = acc_ref[rows, :] + _dot_bf16(a, wd_ref[...])
            acc_ref[rows, :] = x_ref[rows, :] + 0.5 * _rms(acc, post_g_ref[...])
            out_copy(i, c).start()

        @pl.when(i == pl.num_programs(0) - 1)
        def _():
            for c in range(n_chunks):
                out_copy(i, c).wait()


def _ffn(x, pre_g, wg, wu, wd, post_g, cast_src=None):
    t, d = x.shape
    n_f = D_FF // FFN_TF
    assert n_f >= 2, "first and last hidden steps must be distinct"
    nbytes = (3 * FFN_TM * d * 4 + FFN_TM * d * 2 + 2 * 3 * d * FFN_TF * wg.dtype.itemsize
              + 3 * FFN_TM * FFN_TF * 4)
    in_specs = [
        pl.BlockSpec((FFN_TM, d), lambda i, f: (i, 0)),
        pl.BlockSpec((1, d), lambda i, f: (0, 0)),
        pl.BlockSpec((d, FFN_TF), lambda i, f: (0, f)),
        pl.BlockSpec((d, FFN_TF), lambda i, f: (0, f)),
        pl.BlockSpec((FFN_TF, d), lambda i, f: (f, 0)),
        pl.BlockSpec((1, d), lambda i, f: (0, 0)),
    ]
    out_specs = [pl.BlockSpec(memory_space=pl.ANY)]
    out_shape = [jax.ShapeDtypeStruct((t, d), F32)]
    operands = [x, pre_g, wg, wu, wd, post_g]
    n_slabs = 0
    if cast_src is not None:
        rows, cols = cast_src.shape
        n_slabs = rows // FFN_CAST_ROWS
        assert rows % FFN_CAST_ROWS == 0 and n_slabs <= (t // FFN_TM) * n_f
        slab = pl.BlockSpec((FFN_CAST_ROWS, cols),
                            lambda i, f: (jnp.minimum(i * n_f + f, n_slabs - 1), 0))
        in_specs.append(slab)
        out_specs.append(slab)
        out_shape.append(jax.ShapeDtypeStruct((rows, cols), BF16))
        operands.append(cast_src)
        nbytes += 2 * FFN_CAST_ROWS * cols * (4 + 2)
    outs = pl.pallas_call(
        functools.partial(_ffn_kernel, n_cast_slabs=n_slabs),
        grid=(t // FFN_TM, n_f),
        in_specs=in_specs,
        out_specs=out_specs,
        out_shape=out_shape,
        scratch_shapes=[pltpu.VMEM((FFN_TM, d), F32), pltpu.VMEM((FFN_TM, d), BF16),
                        pltpu.SemaphoreType.DMA((FFN_TM // FFN_ROWS,))],
        compiler_params=pltpu.CompilerParams(
            dimension_semantics=("arbitrary", "arbitrary"),
            vmem_limit_bytes=_vmem_limit(nbytes)),
        name="ffn",
    )(*operands)
    return outs[0] if cast_src is None else tuple(outs)


_IN_LATENT = MLA_Q_RANK + MLA_KV_RANK
_IN_W = _IN_LATENT + MLA_ROPE_DIM + 2 * DIFF_QK_W + DIFF_OUT


def _rope_half(v, cos, sin_signed):
    return v * cos + pltpu.roll(v, DIFF_HEAD_DIM // 2, 1) * sin_signed


def _rope_mla(v, cos, sin_up, sin_dn):
    half = MLA_ROPE_DIM // 2
    return (v * cos + pltpu.roll(v, half, 1) * sin_up
            + pltpu.roll(v, V7X_LANES - half, 1) * sin_dn)


def _proj_kernel(x_ref, g_ref, win_ref, qg_ref, wuq_ref, kvg_ref, wukv_ref,
                 cm_ref, sup_ref, sdn_ref, cd_ref, sd_ref,
                 q_ref, k_ref, v_ref, dq_ref, dk_ref, dv_ref):
    h = _rms(x_ref[...], g_ref[...]).astype(BF16)

    pa = _dot_nt(h, win_ref[0:_IN_LATENT, :])
    cq = _rms(pa[:, 0:MLA_Q_RANK], qg_ref[...]).astype(BF16)
    ckv = _rms(pa[:, MLA_Q_RANK:_IN_LATENT], kvg_ref[...]).astype(BF16)
    rest = _dot_nt(h, win_ref[_IN_LATENT:_IN_W, :])
    cm, sup, sdn = cm_ref[...], sup_ref[...], sdn_ref[...]
    kpe = _rope_mla(rest[:, 0:V7X_LANES], cm, sup, sdn).astype(BF16)

    q = _dot_bf16(cq, wuq_ref[...]) * MLA_SCORE_SCALE
    kv = _dot_bf16(ckv, wukv_ref[...])
    for hh in range(MLA_HEADS):
        a = hh * MLA_HEAD_PAD
        b = a + V7X_LANES
        c = b + V7X_LANES
        qa = hh * MLA_QK_DIM
        q_ref[:, a:b] = q[:, qa:qa + MLA_NOPE_DIM].astype(BF16)
        if (qa + MLA_NOPE_DIM) % V7X_LANES == 0:
            pe = q[:, qa + MLA_NOPE_DIM:qa + MLA_NOPE_DIM + V7X_LANES]
        else:
            pe = pltpu.roll(q[:, qa + MLA_ROPE_DIM:qa + MLA_QK_DIM], MLA_ROPE_DIM, 1)
        q_ref[:, b:c] = _rope_mla(pe, cm, sup, sdn).astype(BF16)
        k_ref[:, a:b] = kv[:, a:b].astype(BF16)
        k_ref[:, b:c] = kpe
        v_ref[:, hh * MLA_V_DIM:(hh + 1) * MLA_V_DIM] = kv[:, b:c].astype(BF16)

    cd, sd = cd_ref[...], sd_ref[...]
    off = MLA_ROPE_DIM
    pdq = rest[:, off:off + DIFF_QK_W] * DIFF_SCORE_SCALE
    for c in range(DIFF_QK_W // DIFF_HEAD_DIM):
        cols = slice(c * DIFF_HEAD_DIM, (c + 1) * DIFF_HEAD_DIM)
        dq_ref[:, cols] = _rope_half(pdq[:, cols], cd, sd).astype(BF16)
    pdk = rest[:, off + DIFF_QK_W:off + 2 * DIFF_QK_W]
    for c in range(DIFF_QK_W // DIFF_HEAD_DIM):
        cols = slice(c * DIFF_HEAD_DIM, (c + 1) * DIFF_HEAD_DIM)
        dk_ref[:, cols] = _rope_half(pdk[:, cols], cd, sd).astype(BF16)
    dv_ref[...] = rest[:, off + 2 * DIFF_QK_W:].astype(BF16)


def _mix_proj(x, g, w_in_t, qg, wuq, kvg, wukv, tables, seq):
    t, d = x.shape
    tm = PROJ_TM
    pos_blocks = seq // tm
    const = lambda i: (0, 0)
    row = lambda i: (i, 0)
    pos = lambda i: (i % pos_blocks, 0)
    wq = MLA_HEADS * MLA_HEAD_PAD
    out_w = (wq, wq, MLA_OUT, DIFF_QK_W, DIFF_QK_W, DIFF_OUT)
    once = dict(pipeline_mode=pl.Buffered(1))
    nbytes = (d * _IN_W * 2 + MLA_Q_RANK * wq * (4 + 4) + 2 * tm * d * 4
              + 2 * tm * sum(out_w) * 2 + tm * _IN_W * 4 + 2 * tm * wq * 4)
    return pl.pallas_call(
        _proj_kernel,
        grid=(t // tm,),
        in_specs=[
            pl.BlockSpec((tm, d), row),
            pl.BlockSpec((1, d), const),
            pl.BlockSpec((_IN_W, d), const, **once),
            pl.BlockSpec((1, MLA_Q_RANK), const),
            pl.BlockSpec((MLA_Q_RANK, MLA_HEADS * MLA_QK_DIM), const, **once),
            pl.BlockSpec((1, MLA_KV_RANK), const),
            pl.BlockSpec((MLA_KV_RANK, wq), const, **once),
        ] + [pl.BlockSpec((tm, V7X_LANES), pos)] * 5,
        out_specs=[pl.BlockSpec((tm, w), row) for w in out_w],
        out_shape=[jax.ShapeDtypeStruct((t, w), BF16) for w in out_w],
        compiler_params=pltpu.CompilerParams(
            dimension_semantics=("parallel",),
            vmem_limit_bytes=_vmem_limit(nbytes)),
        name="mix_proj",
    )(x, g, w_in_t, qg, wuq, kvg, wukv, *tables)


def _causal_mask(n_q, n_k):
    r = lax.broadcasted_iota(jnp.int32, (n_q, n_k), 0) // CHUNK
    c = lax.broadcasted_iota(jnp.int32, (n_q, n_k), 1) // CHUNK
    return r >= c


def _flash_scratch(n_maps, v_dim):
    stat = pltpu.VMEM((ATT_TQ, V7X_LANES), F32)
    return [stat, stat, pltpu.VMEM((ATT_TQ, v_dim), F32)] * n_maps


def _flash_init(m_ref, l_ref, acc_ref):
    m_ref[...] = jnp.full(m_ref.shape, -jnp.inf, F32)
    l_ref[...] = jnp.zeros(l_ref.shape, F32)
    acc_ref[...] = jnp.zeros(acc_ref.shape, F32)


def _flash_step(t, v, m_ref, l_ref, acc_ref):
    lanes = m_ref.shape[1]
    m_old = m_ref[...]
    m_new = jnp.maximum(m_old, jnp.max(t, axis=-1, keepdims=True))
    alpha = jnp.exp2(m_old - m_new)
    p = jnp.exp2(t - jnp.tile(m_new, (1, t.shape[1] // lanes)))
    l_ref[...] = alpha * l_ref[...] + jnp.sum(p, axis=-1, keepdims=True)
    pv = jnp.dot(p.astype(BF16), v, preferred_element_type=F32)
    acc_ref[...] = jnp.tile(alpha, (1, acc_ref.shape[1] // lanes)) * acc_ref[...] + pv
    m_ref[...] = m_new


def _attn_tile(qi, n_maps, scores, values, state):
    half = ATT_TQ // 2
    assert ATT_TQ == ATT_TK and half % CHUNK == 0

    def update(q_rows, k_rows, mask):
        for n in range(n_maps):
            t = scores(n, q_rows, k_rows)
            if mask is not None:
                t = jnp.where(mask, t, -jnp.inf)
            _flash_step(t, values(n, k_rows), *[r.at[q_rows] for r in state[3 * n:3 * n + 3]])

    for n in range(n_maps):
        _flash_init(*state[3 * n:3 * n + 3])
    base = pl.multiple_of(qi * ATT_TQ, ATT_TQ)
    update(pl.ds(0, ATT_TQ), pl.ds(base, half), _causal_mask(ATT_TQ, half))
    update(pl.ds(half, half), pl.ds(pl.multiple_of(base + half, half), half),
           _causal_mask(half, half))

    def body(j, carry):
        update(pl.ds(0, ATT_TQ), pl.ds(pl.multiple_of(j * ATT_TK, ATT_TK), ATT_TK), None)
        return carry

    lax.fori_loop(0, qi, body, 0)


def _qk(q, k):
    return lax.dot_general(q, k, (((1,), (1,)), ((), ())), preferred_element_type=F32)


def _mla_attn_kernel(q_ref, k_ref, v_ref, o_ref, *state):
    wq, wv = MLA_HEAD_PAD, MLA_V_DIM

    def scores(g, q_rows, k_rows):
        return _qk(q_ref[q_rows, g * wq:(g + 1) * wq], k_ref[k_rows, g * wq:(g + 1) * wq])

    def values(g, k_rows):
        return v_ref[k_rows, g * wv:(g + 1) * wv]

    _attn_tile(pl.program_id(2), MLA_GROUP, scores, values, state)
    for g in range(MLA_GROUP):
        _, l_ref, acc_ref = state[3 * g:3 * g + 3]
        o_ref[:, g * wv:(g + 1) * wv] = (acc_ref[...] / l_ref[...]).astype(o_ref.dtype)


def _mla_attn(q, k, v, batch, seq):
    nq = seq // ATT_TQ
    wq, wv = MLA_GROUP * MLA_HEAD_PAD, MLA_GROUP * MLA_V_DIM
    return pl.pallas_call(
        _mla_attn_kernel,
        grid=(batch, MLA_HEADS // MLA_GROUP, nq),
        in_specs=[
            pl.BlockSpec((ATT_TQ, wq), lambda b, h, i: (b * nq + i, h)),
            pl.BlockSpec((seq, wq), lambda b, h, i: (b, h)),
            pl.BlockSpec((seq, wv), lambda b, h, i: (b, h)),
        ],
        out_specs=pl.BlockSpec((ATT_TQ, wv), lambda b, h, i: (b * nq + i, h)),
        out_shape=jax.ShapeDtypeStruct((batch * seq, MLA_OUT), BF16),
        scratch_shapes=_flash_scratch(MLA_GROUP, MLA_V_DIM),
        compiler_params=pltpu.CompilerParams(
            dimension_semantics=("parallel", "parallel", "parallel"),
            vmem_limit_bytes=_vmem_limit(2 * (seq + ATT_TQ) * (wq + wv) * 2
                                         + 8 * MLA_GROUP * ATT_TQ * ATT_TK * 4)),
        name="mla_attn",
    )(q, k, v)


def _diff_attn_kernel(q_ref, k_ref, v_ref, lq1_ref, lk1_ref, lq2_ref, lk2_ref, g_ref, o_ref,
                      *state):
    d, wv = DIFF_HEAD_DIM, DIFF_V_DIM

    def scores(n, q_rows, k_rows):
        return _qk(q_ref[q_rows, n * d:(n + 1) * d], k_ref[k_rows, n * d:(n + 1) * d])

    def values(n, k_rows):
        return v_ref[k_rows, (n // 2) * wv:(n // 2 + 1) * wv]

    _attn_tile(pl.program_id(2), 2 * DIFF_GROUP, scores, values, state)
    lam = (jnp.exp(jnp.sum(lq1_ref[...] * lk1_ref[...], axis=-1, keepdims=True))
           - jnp.exp(jnp.sum(lq2_ref[...] * lk2_ref[...], axis=-1, keepdims=True))
           + LAMBDA_INIT)
    rep = wv // V7X_LANES
    for h in range(DIFF_GROUP):
        _, l1_ref, a1_ref, _, l2_ref, a2_ref = state[6 * h:6 * h + 6]
        o = (a1_ref[...] / jnp.tile(l1_ref[...], (1, rep))
             - lam * (a2_ref[...] / jnp.tile(l2_ref[...], (1, rep))))
        o_ref[:, h * wv:(h + 1) * wv] = (
            _rms(o, g_ref[...]) * (1.0 - LAMBDA_INIT)).astype(o_ref.dtype)


def _diff_attn(q, k, v, lq1, lk1, lq2, lk2, g, batch, seq):
    nq = seq // ATT_TQ
    w = DIFF_GROUP * DIFF_V_DIM
    vec = pl.BlockSpec((1, DIFF_HEAD_DIM), lambda b, h, i: (0, 0))
    return pl.pallas_call(
        _diff_attn_kernel,
        grid=(batch, DIFF_HEADS // DIFF_GROUP, nq),
        in_specs=[
            pl.BlockSpec((ATT_TQ, w), lambda b, h, i: (b * nq + i, h)),
            pl.BlockSpec((seq, w), lambda b, h, i: (b, h)),
            pl.BlockSpec((seq, w), lambda b, h, i: (b, h)),
            vec, vec, vec, vec,
            pl.BlockSpec((1, DIFF_V_DIM), lambda b, h, i: (0, 0)),
        ],
        out_specs=pl.BlockSpec((ATT_TQ, w), lambda b, h, i: (b * nq + i, h)),
        out_shape=jax.ShapeDtypeStruct((batch * seq, DIFF_OUT), BF16),
        scratch_shapes=_flash_scratch(2 * DIFF_GROUP, DIFF_V_DIM),
        compiler_params=pltpu.CompilerParams(
            dimension_semantics=("parallel", "parallel", "parallel"),
            vmem_limit_bytes=_vmem_limit(2 * (2 * seq + 2 * ATT_TQ) * w * 2
                                         + 16 * DIFF_GROUP * ATT_TQ * ATT_TK * 4)),
        name="diff_attn",
    )(q, k, v, lq1, lk1, lq2, lk2, g)


def _out_kernel(x_ref, oa_ref, ob_ref, wa_ref, wb_ref, g_ref, o_ref):
    o = _dot_bf16(oa_ref[...], wa_ref[...]) + _dot_bf16(ob_ref[...], wb_ref[...])
    o_ref[...] = x_ref[...] + _rms(o, g_ref[...])


def _out_proj(x, oa, ob, w, g):
    t, d = x.shape
    assert MLA_OUT == DIFF_OUT, "the two row blocks of w share one block shape"
    tm = OUT_TM
    const = lambda i: (0, 0)
    row = lambda i: (i, 0)
    once = dict(pipeline_mode=pl.Buffered(1))
    nbytes = (d * d * (w.dtype.itemsize + 2) + 4 * tm * d * 4 + 2 * tm * d * 2 + 2 * tm * d * 4)
    return pl.pallas_call(
        _out_kernel,
        grid=(t // tm,),
        in_specs=[
            pl.BlockSpec((tm, d), row),
            pl.BlockSpec((tm, MLA_OUT), row),
            pl.BlockSpec((tm, DIFF_OUT), row),
            pl.BlockSpec((MLA_OUT, d), const, **once),
            pl.BlockSpec((DIFF_OUT, d), lambda i: (1, 0), **once),
            pl.BlockSpec((1, d), const),
        ],
        out_specs=pl.BlockSpec((tm, d), row),
        out_shape=jax.ShapeDtypeStruct((t, d), F32),
        compiler_params=pltpu.CompilerParams(
            dimension_semantics=("parallel",),
            vmem_limit_bytes=_vmem_limit(nbytes)),
        name="out_proj",
    )(x, oa, ob, w, w, g)


def _rope_tables(seq):
    pos = np.arange(seq, dtype=np.float64)[:, None]

    def angles(d):
        inv_freq = ROPE_THETA ** (-np.arange(0, d, 2, dtype=np.float64) / d)
        return pos * inv_freq[None, :]

    ang = angles(MLA_ROPE_DIM)
    zeros = np.zeros((seq, V7X_LANES - MLA_ROPE_DIM))
    half0 = np.zeros_like(ang)
    cos_m = np.concatenate([np.cos(ang), np.cos(ang), zeros], axis=1)
    sin_up = np.concatenate([half0, np.sin(ang), zeros], axis=1)
    sin_dn = np.concatenate([-np.sin(ang), half0, zeros], axis=1)
    ang = angles(DIFF_HEAD_DIM)
    cos_d = np.concatenate([np.cos(ang), np.cos(ang)], axis=1)
    sin_d = np.concatenate([-np.sin(ang), np.sin(ang)], axis=1)
    return tuple(jnp.asarray(t, F32) for t in (cos_m, sin_up, sin_dn, cos_d, sin_d))


def kernel(x, ffn1_pre_g, ffn1_w_gate, ffn1_w_up, ffn1_w_down, ffn1_post_g, mix_pre_g, w_in, mla_q_norm_g, mla_w_uq, mla_kv_norm_g, mla_w_ukv, diff_lambda_q1, diff_lambda_k1, diff_lambda_q2, diff_lambda_k2, diff_subln_g, w_out, mix_post_g, ffn2_pre_g, ffn2_w_gate, ffn2_w_up, ffn2_w_down, ffn2_post_g):
    batch, seq, d = x.shape
    depth = ffn1_pre_g.shape[0]
    assert depth == 1 and d == D_MODEL
    assert seq % ATT_TQ == 0 and seq % PROJ_TM == 0 and (batch * seq) % FFN_TM == 0
    tables = _rope_tables(seq)
    xt = x.reshape(batch * seq, d)
    for l in range(depth):
        xt, w_in_bf = _ffn(xt, ffn1_pre_g[l][None], ffn1_w_gate[l], ffn1_w_up[l], ffn1_w_down[l],
                           ffn1_post_g[l][None], cast_src=w_in[l].T)
        q, k, v, dq, dk, dv = _mix_proj(
            xt, mix_pre_g[l][None], w_in_bf, mla_q_norm_g[l][None],
            mla_w_uq[l], mla_kv_norm_g[l][None], mla_w_ukv[l],
            tables, seq)
        o_mla = _mla_attn(q, k, v, batch, seq)
        o_diff = _diff_attn(dq, dk, dv, diff_lambda_q1[l][None], diff_lambda_k1[l][None],
                            diff_lambda_q2[l][None], diff_lambda_k2[l][None],
                            diff_subln_g[l][None], batch, seq)
        xt = _out_proj(xt, o_mla, o_diff, w_out[l], mix_post_g[l][None])
        xt = _ffn(xt, ffn2_pre_g[l][None], ffn2_w_gate[l], ffn2_w_up[l], ffn2_w_down[l],
                  ffn2_post_g[l][None])
    return xt.reshape(batch, seq, d)
```

```python
import functools
import math

import jax
import jax.numpy as jnp
import numpy as np
from jax import lax
from jax.experimental import pallas as pl
from jax.experimental.pallas import tpu as pltpu

D_MODEL = 2048
CHUNK = 64
ROPE_THETA = 10000.0
NORM_EPS = 1e-6
MLA_HEADS = 8
MLA_Q_RANK = 512
MLA_KV_RANK = 512
MLA_NOPE_DIM = 128
MLA_ROPE_DIM = 64
MLA_V_DIM = 128
MLA_QK_DIM = MLA_NOPE_DIM + MLA_ROPE_DIM
DIFF_HEADS = 4
DIFF_HEAD_DIM = 128
DIFF_V_DIM = 2 * DIFF_HEAD_DIM
MLA_OUT = MLA_HEADS * MLA_V_DIM
DIFF_OUT = DIFF_HEADS * DIFF_V_DIM
DIFF_QK_W = DIFF_HEADS * 2 * DIFF_HEAD_DIM
D_FF = 5632
LAMBDA_INIT = 0.8 - 0.6 * math.exp(-0.3 * 0)
LOG2_E = math.log2(math.e)
MLA_SCORE_SCALE = MLA_QK_DIM ** -0.5 * LOG2_E
DIFF_SCORE_SCALE = DIFF_HEAD_DIM ** -0.5 * LOG2_E

V7X_LANES = 128
V7X_VMEM_BYTES = 64 * 1024 * 1024
MLA_HEAD_PAD = 2 * V7X_LANES

FFN_TM = 1024
FFN_TF = 512
FFN_ROWS = 256
FFN_CAST_ROWS = 64
PROJ_TM = 512
ATT_TQ = 512
ATT_TK = 512
MLA_GROUP = 8
DIFF_GROUP = 4
OUT_TM = 512

BF16 = jnp.bfloat16
F32 = jnp.float32


def _vmem_limit(nbytes):
    return int(min(V7X_VMEM_BYTES - 4 * 1024 * 1024, max(32 * 1024 * 1024, nbytes * 3 // 2)))


def _dot_bf16(a, w):
    return lax.dot_general(a, w, (((1,), (0,)), ((), ())), preferred_element_type=F32)


def _dot_nt(a, w_t):
    return lax.dot_general(a, w_t, (((1,), (1,)), ((), ())), preferred_element_type=F32)


def _rms(v, g):
    ms = jnp.mean(v * v, axis=-1, keepdims=True)
    return v * lax.rsqrt(ms + NORM_EPS) * g


def _swiglu_hidden(h, wg, wu):
    g = _dot_bf16(h, wg)
    u = _dot_bf16(h, wu)
    return (g * (1.0 / (1.0 + jnp.exp(-g))) * u).astype(BF16)


def _ffn_kernel(*refs, n_cast_slabs):
    if n_cast_slabs:
        (x_ref, pre_g_ref, wg_ref, wu_ref, wd_ref, post_g_ref, src_ref,
         o_hbm, dst_ref, acc_ref, h_ref, sem) = refs
    else:
        x_ref, pre_g_ref, wg_ref, wu_ref, wd_ref, post_g_ref, o_hbm, acc_ref, h_ref, sem = refs
    i = pl.program_id(0)
    f = pl.program_id(1)
    last = pl.num_programs(1) - 1

    def side_job():
        if n_cast_slabs:
            dst_ref[...] = src_ref[...].astype(BF16)

    n_chunks = FFN_TM // FFN_ROWS
    chunks = [pl.ds(c * FFN_ROWS, FFN_ROWS) for c in range(n_chunks)]

    def out_copy(tile, c):
        dst = o_hbm.at[pl.ds(pl.multiple_of(tile * FFN_TM + c * FFN_ROWS, FFN_ROWS), FFN_ROWS)]
        return pltpu.make_async_copy(acc_ref.at[chunks[c]], dst, sem.at[c])

    @pl.when(jnp.logical_and(f == 0, i > 0))
    def _():
        for c in range(n_chunks):
            out_copy(i - 1, c).wait()

    @pl.when(f == 0)
    def _():
        side_job()
        for rows in chunks:
            h = _rms(x_ref[rows, :], pre_g_ref[...]).astype(BF16)
            h_ref[rows, :] = h
            acc_ref[rows, :] = _dot_bf16(_swiglu_hidden(h, wg_ref[...], wu_ref[...]), wd_ref[...])

    @pl.when(jnp.logical_and(f > 0, f < last))
    def _():
        side_job()
        a = _swiglu_hidden(h_ref[...], wg_ref[...], wu_ref[...])
        acc_ref[...] += _dot_bf16(a, wd_ref[...])

    @pl.when(f == last)
    def _():
        side_job()
        for c, rows in enumerate(chunks):
            a = _swiglu_hidden(h_ref[rows, :], wg_ref[...], wu_ref[...])
            acc = acc_ref[rows, :] + _dot_bf16(a, wd_ref[...])
            acc_ref[rows, :] = x_ref[rows, :] + 0.5 * _rms(acc, post_g_ref[...])
            out_copy(i, c).start()

        @pl.when(i == pl.num_programs(0) - 1)
        def _():
            for c in range(n_chunks):
                out_copy(i, c).wait()


def _ffn(x, pre_g, wg, wu, wd, post_g, cast_src=None):
    t, d = x.shape
    n_f = D_FF // FFN_TF
    assert n_f >= 2, "first and last hidden steps must be distinct"
    nbytes = (3 * FFN_TM * d * 4 + FFN_TM * d * 2 + 2 * 3 * d * FFN_TF * wg.dtype.itemsize
              + 3 * FFN_TM * FFN_TF * 4)
    in_specs = [
        pl.BlockSpec((FFN_TM, d), lambda i, f: (i, 0)),
        pl.BlockSpec((1, d), lambda i, f: (0, 0)),
        pl.BlockSpec((d, FFN_TF), lambda i, f: (0, f)),
        pl.BlockSpec((d, FFN_TF), lambda i, f: (0, f)),
        pl.BlockSpec((FFN_TF, d), lambda i, f: (f, 0)),
        pl.BlockSpec((1, d), lambda i, f: (0, 0)),
    ]
    out_specs = [pl.BlockSpec(memory_space=pl.ANY)]
    out_shape = [jax.ShapeDtypeStruct((t, d), F32)]
    operands = [x, pre_g, wg, wu, wd, post_g]
    n_slabs = 0
    if cast_src is not None:
        rows, cols = cast_src.shape
        n_slabs = rows // FFN_CAST_ROWS
        assert rows % FFN_CAST_ROWS == 0 and n_slabs <= (t // FFN_TM) * n_f
        slab = pl.BlockSpec((FFN_CAST_ROWS, cols),
                            lambda i, f: (jnp.minimum(i * n_f + f, n_slabs - 1), 0))
        in_specs.append(slab)
        out_specs.append(slab)
        out_shape.append(jax.ShapeDtypeStruct((rows, cols), BF16))
        operands.append(cast_src)
        nbytes += 2 * FFN_CAST_ROWS * cols * (4 + 2)
    outs = pl.pallas_call(
        functools.partial(_ffn_kernel, n_cast_slabs=n_slabs),
        grid=(t // FFN_TM, n_f),
        in_specs=in_specs,
        out_specs=out_specs,
        out_shape=out_shape,
        scratch_shapes=[pltpu.VMEM((FFN_TM, d), F32), pltpu.VMEM((FFN_TM, d), BF16),
                        pltpu.SemaphoreType.DMA((FFN_TM // FFN_ROWS,))],
        compiler_params=pltpu.CompilerParams(
            dimension_semantics=("arbitrary", "arbitrary"),
            vmem_limit_bytes=_vmem_limit(nbytes)),
        name="ffn",
    )(*operands)
    return outs[0] if cast_src is None else tuple(outs)


_IN_LATENT = MLA_Q_RANK + MLA_KV_RANK
_IN_W = _IN_LATENT + MLA_ROPE_DIM + 2 * DIFF_QK_W + DIFF_OUT


def _rope_half(v, cos, sin_signed):
    return v * cos + pltpu.roll(v, DIFF_HEAD_DIM // 2, 1) * sin_signed


def _rope_mla(v, cos, sin_up, sin_dn):
    half = MLA_ROPE_DIM // 2
    return (v * cos + pltpu.roll(v, half, 1) * sin_up
            + pltpu.roll(v, V7X_LANES - half, 1) * sin_dn)


def _proj_kernel(x_ref, g_ref, win_ref, qg_ref, wuq_ref, kvg_ref, wukv_ref,
                 cm_ref, sup_ref, sdn_ref, cd_ref, sd_ref,
                 q_ref, k_ref, v_ref, dq_ref, dk_ref, dv_ref):
    h = _rms(x_ref[...], g_ref[...]).astype(BF16)

    pa = _dot_nt(h, win_ref[0:_IN_LATENT, :])
    cq = _rms(pa[:, 0:MLA_Q_RANK], qg_ref[...]).astype(BF16)
    ckv = _rms(pa[:, MLA_Q_RANK:_IN_LATENT], kvg_ref[...]).astype(BF16)
    rest = _dot_nt(h, win_ref[_IN_LATENT:_IN_W, :])
    cm, sup, sdn = cm_ref[...], sup_ref[...], sdn_ref[...]
    kpe = _rope_mla(rest[:, 0:V7X_LANES], cm, sup, sdn).astype(BF16)

    q = _dot_bf16(cq, wuq_ref[...]) * MLA_SCORE_SCALE
    kv = _dot_bf16(ckv, wukv_ref[...])
    for hh in range(MLA_HEADS):
        a = hh * MLA_HEAD_PAD
        b = a + V7X_LANES
        c = b + V7X_LANES
        qa = hh * MLA_QK_DIM
        q_ref[:, a:b] = q[:, qa:qa + MLA_NOPE_DIM].astype(BF16)
        if (qa + MLA_NOPE_DIM) % V7X_LANES == 0:
            pe = q[:, qa + MLA_NOPE_DIM:qa + MLA_NOPE_DIM + V7X_LANES]
        else:
            pe = pltpu.roll(q[:, qa + MLA_ROPE_DIM:qa + MLA_QK_DIM], MLA_ROPE_DIM, 1)
        q_ref[:, b:c] = _rope_mla(pe, cm, sup, sdn).astype(BF16)
        k_ref[:, a:b] = kv[:, a:b].astype(BF16)
        k_ref[:, b:c] = kpe
        v_ref[:, hh * MLA_V_DIM:(hh + 1) * MLA_V_DIM] = kv[:, b:c].astype(BF16)

    cd, sd = cd_ref[...], sd_ref[...]
    off = MLA_ROPE_DIM
    pdq = rest[:, off:off + DIFF_QK_W] * DIFF_SCORE_SCALE
    for c in range(DIFF_QK_W // DIFF_HEAD_DIM):
        cols = slice(c * DIFF_HEAD_DIM, (c + 1) * DIFF_HEAD_DIM)
        dq_ref[:, cols] = _rope_half(pdq[:, cols], cd, sd).astype(BF16)
    pdk = rest[:, off + DIFF_QK_W:off + 2 * DIFF_QK_W]
    for c in range(DIFF_QK_W // DIFF_HEAD_DIM):
        cols = slice(c * DIFF_HEAD_DIM, (c + 1) * DIFF_HEAD_DIM)
        dk_ref[:, cols] = _rope_half(pdk[:, cols], cd, sd).astype(BF16)
    dv_ref[...] = rest[:, off + 2 * DIFF_QK_W:].astype(BF16)


def _mix_proj(x, g, w_in_t, qg, wuq, kvg, wukv, tables, seq):
    t, d = x.shape
    tm = PROJ_TM
    pos_blocks = seq // tm
    const = lambda i: (0, 0)
    row = lambda i: (i, 0)
    pos = lambda i: (i % pos_blocks, 0)
    wq = MLA_HEADS * MLA_HEAD_PAD
    out_w = (wq, wq, MLA_OUT, DIFF_QK_W, DIFF_QK_W, DIFF_OUT)
    once = dict(pipeline_mode=pl.Buffered(1))
    nbytes = (d * _IN_W * 2 + MLA_Q_RANK * wq * (4 + 4) + 2 * tm * d * 4
              + 2 * tm * sum(out_w) * 2 + tm * _IN_W * 4 + 2 * tm * wq * 4)
    return pl.pallas_call(
        _proj_kernel,
        grid=(t // tm,),
        in_specs=[
            pl.BlockSpec((tm, d), row),
            pl.BlockSpec((1, d), const),
            pl.BlockSpec((_IN_W, d), const, **once),
            pl.BlockSpec((1, MLA_Q_RANK), const),
            pl.BlockSpec((MLA_Q_RANK, MLA_HEADS * MLA_QK_DIM), const, **once),
            pl.BlockSpec((1, MLA_KV_RANK), const),
            pl.BlockSpec((MLA_KV_RANK, wq), const, **once),
        ] + [pl.BlockSpec((tm, V7X_LANES), pos)] * 5,
        out_specs=[pl.BlockSpec((tm, w), row) for w in out_w],
        out_shape=[jax.ShapeDtypeStruct((t, w), BF16) for w in out_w],
        compiler_params=pltpu.CompilerParams(
            dimension_semantics=("parallel",),
            vmem_limit_bytes=_vmem_limit(nbytes)),
        name="mix_proj",
    )(x, g, w_in_t, qg, wuq, kvg, wukv, *tables)


def _causal_mask(n_q, n_k):
    r = lax.broadcasted_iota(jnp.int32, (n_q, n_k), 0) // CHUNK
    c = lax.broadcasted_iota(jnp.int32, (n_q, n_k), 1) // CHUNK
    return r >= c


def _flash_scratch(n_maps, v_dim):
    stat = pltpu.VMEM((ATT_TQ, V7X_LANES), F32)
    return [stat, stat, pltpu.VMEM((ATT_TQ, v_dim), F32)] * n_maps


def _flash_init(m_ref, l_ref, acc_ref):
    m_ref[...] = jnp.full(m_ref.shape, -jnp.inf, F32)
    l_ref[...] = jnp.zeros(l_ref.shape, F32)
    acc_ref[...] = jnp.zeros(acc_ref.shape, F32)


def _flash_step(t, v, m_ref, l_ref, acc_ref):
    lanes = m_ref.shape[1]
    m_old = m_ref[...]
    m_new = jnp.maximum(m_old, jnp.max(t, axis=-1, keepdims=True))
    alpha = jnp.exp2(m_old - m_new)
    p = jnp.exp2(t - jnp.tile(m_new, (1, t.shape[1] // lanes)))
    l_ref[...] = alpha * l_ref[...] + jnp.sum(p, axis=-1, keepdims=True)
    pv = jnp.dot(p.astype(BF16), v, preferred_element_type=F32)
    acc_ref[...] = jnp.tile(alpha, (1, acc_ref.shape[1] // lanes)) * acc_ref[...] + pv
    m_ref[...] = m_new


def _attn_tile(qi, n_maps, scores, values, state):
    half = ATT_TQ // 2
    assert ATT_TQ == ATT_TK and half % CHUNK == 0

    def update(q_rows, k_rows, mask):
        for n in range(n_maps):
            t = scores(n, q_rows, k_rows)
            if mask is not None:
                t = jnp.where(mask, t, -jnp.inf)
            _flash_step(t, values(n, k_rows), *[r.at[q_rows] for r in state[3 * n:3 * n + 3]])

    for n in range(n_maps):
        _flash_init(*state[3 * n:3 * n + 3])
    base = pl.multiple_of(qi * ATT_TQ, ATT_TQ)
    update(pl.ds(0, ATT_TQ), pl.ds(base, half), _causal_mask(ATT_TQ, half))
    update(pl.ds(half, half), pl.ds(pl.multiple_of(base + half, half), half),
           _causal_mask(half, half))

    def body(j, carry):
        update(pl.ds(0, ATT_TQ), pl.ds(pl.multiple_of(j * 2 * ATT_TK, 2 * ATT_TK), 2 * ATT_TK), None)
        return carry

    lax.fori_loop(0, qi // 2, body, 0)

    @pl.when(qi % 2 == 1)
    def _():
        update(pl.ds(0, ATT_TQ), pl.ds(pl.multiple_of((qi - 1) * ATT_TK, ATT_TK), ATT_TK), None)


def _qk(q, k):
    return lax.dot_general(q, k, (((1,), (1,)), ((), ())), preferred_element_type=F32)


def _mla_attn_kernel(q_ref, k_ref, v_ref, o_ref, *state):
    wq, wv = MLA_HEAD_PAD, MLA_V_DIM

    def scores(g, q_rows, k_rows):
        return _qk(q_ref[q_rows, g * wq:(g + 1) * wq], k_ref[k_rows, g * wq:(g + 1) * wq])

    def values(g, k_rows):
        return v_ref[k_rows, g * wv:(g + 1) * wv]

    _attn_tile(pl.program_id(2), MLA_GROUP, scores, values, state)
    for g in range(MLA_GROUP):
        _, l_ref, acc_ref = state[3 * g:3 * g + 3]
        o_ref[:, g * wv:(g + 1) * wv] = (acc_ref[...] / l_ref[...]).astype(o_ref.dtype)


def _mla_attn(q, k, v, batch, seq):
    nq = seq // ATT_TQ
    wq, wv = MLA_GROUP * MLA_HEAD_PAD, MLA_GROUP * MLA_V_DIM
    return pl.pallas_call(
        _mla_attn_kernel,
        grid=(batch, MLA_HEADS // MLA_GROUP, nq),
        in_specs=[
            pl.BlockSpec((ATT_TQ, wq), lambda b, h, i: (b * nq + i, h)),
            pl.BlockSpec((seq, wq), lambda b, h, i: (b, h)),
            pl.BlockSpec((seq, wv), lambda b, h, i: (b, h)),
        ],
        out_specs=pl.BlockSpec((ATT_TQ, wv), lambda b, h, i: (b * nq + i, h)),
        out_shape=jax.ShapeDtypeStruct((batch * seq, MLA_OUT), BF16),
        scratch_shapes=_flash_scratch(MLA_GROUP, MLA_V_DIM),
        compiler_params=pltpu.CompilerParams(
            dimension_semantics=("parallel", "parallel", "parallel"),
            vmem_limit_bytes=_vmem_limit(2 * (seq + ATT_TQ) * (wq + wv) * 2
                                         + 8 * MLA_GROUP * ATT_TQ * ATT_TK * 4)),
        name="mla_attn",
    )(q, k, v)


def _diff_attn_kernel(q_ref, k_ref, v_ref, lq1_ref, lk1_ref, lq2_ref, lk2_ref, g_ref, o_ref,
                      *state):
    d, wv = DIFF_HEAD_DIM, DIFF_V_DIM

    def scores(n, q_rows, k_rows):
        return _qk(q_ref[q_rows, n * d:(n + 1) * d], k_ref[k_rows, n * d:(n + 1) * d])

    def values(n, k_rows):
        return v_ref[k_rows, (n // 2) * wv:(n // 2 + 1) * wv]

    _attn_tile(pl.program_id(2), 2 * DIFF_GROUP, scores, values, state)
    lam = (jnp.exp(jnp.sum(lq1_ref[...] * lk1_ref[...], axis=-1, keepdims=True))
           - jnp.exp(jnp.sum(lq2_ref[...] * lk2_ref[...], axis=-1, keepdims=True))
           + LAMBDA_INIT)
    rep = wv // V7X_LANES
    for h in range(DIFF_GROUP):
        _, l1_ref, a1_ref, _, l2_ref, a2_ref = state[6 * h:6 * h + 6]
        o = (a1_ref[...] / jnp.tile(l1_ref[...], (1, rep))
             - lam * (a2_ref[...] / jnp.tile(l2_ref[...], (1, rep))))
        o_ref[:, h * wv:(h + 1) * wv] = (
            _rms(o, g_ref[...]) * (1.0 - LAMBDA_INIT)).astype(o_ref.dtype)


def _diff_attn(q, k, v, lq1, lk1, lq2, lk2, g, batch, seq):
    nq = seq // ATT_TQ
    w = DIFF_GROUP * DIFF_V_DIM
    vec = pl.BlockSpec((1, DIFF_HEAD_DIM), lambda b, h, i: (0, 0))
    return pl.pallas_call(
        _diff_attn_kernel,
        grid=(batch, DIFF_HEADS // DIFF_GROUP, nq),
        in_specs=[
            pl.BlockSpec((ATT_TQ, w), lambda b, h, i: (b * nq + i, h)),
            pl.BlockSpec((seq, w), lambda b, h, i: (b, h)),
            pl.BlockSpec((seq, w), lambda b, h, i: (b, h)),
            vec, vec, vec, vec,
            pl.BlockSpec((1, DIFF_V_DIM), lambda b, h, i: (0, 0)),
        ],
        out_specs=pl.BlockSpec((ATT_TQ, w), lambda b, h, i: (b * nq + i, h)),
        out_shape=jax.ShapeDtypeStruct((batch * seq, DIFF_OUT), BF16),
        scratch_shapes=_flash_scratch(2 * DIFF_GROUP, DIFF_V_DIM),
        compiler_params=pltpu.CompilerParams(
            dimension_semantics=("parallel", "parallel", "parallel"),
            vmem_limit_bytes=_vmem_limit(2 * (2 * seq + 2 * ATT_TQ) * w * 2
                                         + 16 * DIFF_GROUP * ATT_TQ * ATT_TK * 4)),
        name="diff_attn",
    )(q, k, v, lq1, lk1, lq2, lk2, g)


def _out_kernel(x_ref, oa_ref, ob_ref, wa_ref, wb_ref, g_ref, o_ref):
    o = _dot_bf16(oa_ref[...], wa_ref[...]) + _dot_bf16(ob_ref[...], wb_ref[...])
    o_ref[...] = x_ref[...] + _rms(o, g_ref[...])


def _out_proj(x, oa, ob, w, g):
    t, d = x.shape
    assert MLA_OUT == DIFF_OUT, "the two row blocks of w share one block shape"
    tm = OUT_TM
    const = lambda i: (0, 0)
    row = lambda i: (i, 0)
    once = dict(pipeline_mode=pl.Buffered(1))
    nbytes = (d * d * (w.dtype.itemsize + 2) + 4 * tm * d * 4 + 2 * tm * d * 2 + 2 * tm * d * 4)
    return pl.pallas_call(
        _out_kernel,
        grid=(t // tm,),
        in_specs=[
            pl.BlockSpec((tm, d), row),
            pl.BlockSpec((tm, MLA_OUT), row),
            pl.BlockSpec((tm, DIFF_OUT), row),
            pl.BlockSpec((MLA_OUT, d), const, **once),
            pl.BlockSpec((DIFF_OUT, d), lambda i: (1, 0), **once),
            pl.BlockSpec((1, d), const),
        ],
        out_specs=pl.BlockSpec((tm, d), row),
        out_shape=jax.ShapeDtypeStruct((t, d), F32),
        compiler_params=pltpu.CompilerParams(
            dimension_semantics=("parallel",),
            vmem_limit_bytes=_vmem_limit(nbytes)),
        name="out_proj",
    )(x, oa, ob, w, w, g)


def _rope_tables(seq):
    pos = np.arange(seq, dtype=np.float64)[:, None]

    def angles(d):
        inv_freq = ROPE_THETA ** (-np.arange(0, d, 2, dtype=np.float64) / d)
        return pos * inv_freq[None, :]

    ang = angles(MLA_ROPE_DIM)
    zeros = np.zeros((seq, V7X_LANES - MLA_ROPE_DIM))
    half0 = np.zeros_like(ang)
    cos_m = np.concatenate([np.cos(ang), np.cos(ang), zeros], axis=1)
    sin_up = np.concatenate([half0, np.sin(ang), zeros], axis=1)
    sin_dn = np.concatenate([-np.sin(ang), half0, zeros], axis=1)
    ang = angles(DIFF_HEAD_DIM)
    cos_d = np.concatenate([np.cos(ang), np.cos(ang)], axis=1)
    sin_d = np.concatenate([-np.sin(ang), np.sin(ang)], axis=1)
    return tuple(jnp.asarray(t, F32) for t in (cos_m, sin_up, sin_dn, cos_d, sin_d))


def kernel(x, ffn1_pre_g, ffn1_w_gate, ffn1_w_up, ffn1_w_down, ffn1_post_g, mix_pre_g, w_in, mla_q_norm_g, mla_w_uq, mla_kv_norm_g, mla_w_ukv, diff_lambda_q1, diff_lambda_k1, diff_lambda_q2, diff_lambda_k2, diff_subln_g, w_out, mix_post_g, ffn2_pre_g, ffn2_w_gate, ffn2_w_up, ffn2_w_down, ffn2_post_g):
    batch, seq, d = x.shape
    depth = ffn1_pre_g.shape[0]
    assert depth == 1 and d == D_MODEL
    assert seq % ATT_TQ == 0 and seq % PROJ_TM == 0 and (batch * seq) % FFN_TM == 0
    tables = _rope_tables(seq)
    xt = x.reshape(batch * seq, d)
    for l in range(depth):
        xt, w_in_bf = _ffn(xt, ffn1_pre_g[l][None], ffn1_w_gate[l], ffn1_w_up[l], ffn1_w_down[l],
                           ffn1_post_g[l][None], cast_src=w_in[l].T)
        q, k, v, dq, dk, dv = _mix_proj(
            xt, mix_pre_g[l][None], w_in_bf, mla_q_norm_g[l][None],
            mla_w_uq[l], mla_kv_norm_g[l][None], mla_w_ukv[l],
            tables, seq)
        o_mla = _mla_attn(q, k, v, batch, seq)
        o_diff = _diff_attn(dq, dk, dv, diff_lambda_q1[l][None], diff_lambda_k1[l][None],
                            diff_lambda_q2[l][None], diff_lambda_k2[l][None],
                            diff_subln_g[l][None], batch, seq)
        xt = _out_proj(xt, o_mla, o_diff, w_out[l], mix_post_g[l][None])
        xt = _ffn(xt, ffn2_pre_g[l][None], ffn2_w_gate[l], ffn2_w_up[l], ffn2_w_down[l],
                  ffn2_post_g[l][None])
    return xt.reshape(batch, seq, d)
```

```python
import functools
import math

import jax
import jax.numpy as jnp
import numpy as np
from jax import lax
from jax.experimental import pallas as pl
from jax.experimental.pallas import tpu as pltpu

D_MODEL = 2048
CHUNK = 64
ROPE_THETA = 10000.0
NORM_EPS = 1e-6
MLA_HEADS = 8
MLA_Q_RANK = 512
MLA_KV_RANK = 512
MLA_NOPE_DIM = 128
MLA_ROPE_DIM = 64
MLA_V_DIM = 128
MLA_QK_DIM = MLA_NOPE_DIM + MLA_ROPE_DIM
DIFF_HEADS = 4
DIFF_HEAD_DIM = 128
DIFF_V_DIM = 2 * DIFF_HEAD_DIM
MLA_OUT = MLA_HEADS * MLA_V_DIM
DIFF_OUT = DIFF_HEADS * DIFF_V_DIM
DIFF_QK_W = DIFF_HEADS * 2 * DIFF_HEAD_DIM
D_FF = 5632
LAMBDA_INIT = 0.8 - 0.6 * math.exp(-0.3 * 0)
LOG2_E = math.log2(math.e)
MLA_SCORE_SCALE = MLA_QK_DIM ** -0.5 * LOG2_E
DIFF_SCORE_SCALE = DIFF_HEAD_DIM ** -0.5 * LOG2_E

V7X_LANES = 128
V7X_VMEM_BYTES = 64 * 1024 * 1024
V7X_DEFAULT_SCOPED_VMEM = 32 * 1024 * 1024
V7X_VMEM_RESERVE = 4 * 1024 * 1024
MLA_HEAD_PAD = 2 * V7X_LANES

FFN_TM = 1024
FFN_TF = 512
FFN_ROWS = 256
FFN_CAST_ROWS = 64
PROJ_TM = 512
ATT_TQ = 512
ATT_TK = 512
ATT_KGROUP = 3
MLA_GROUP = 8
DIFF_GROUP = 4
OUT_TM = 512

BF16 = jnp.bfloat16
F32 = jnp.float32


def _vmem_limit(nbytes):
    return int(min(V7X_VMEM_BYTES - V7X_VMEM_RESERVE, max(V7X_DEFAULT_SCOPED_VMEM, nbytes * 3 // 2)))


def _dot_bf16(a, w):
    return lax.dot_general(a, w, (((1,), (0,)), ((), ())), preferred_element_type=F32)


def _dot_nt(a, w_t):
    return lax.dot_general(a, w_t, (((1,), (1,)), ((), ())), preferred_element_type=F32)


def _rms(v, g):
    ms = jnp.mean(v * v, axis=-1, keepdims=True)
    return v * lax.rsqrt(ms + NORM_EPS) * g


def _swiglu_hidden(h, wg, wu):
    g = _dot_bf16(h, wg)
    u = _dot_bf16(h, wu)
    return (g * (1.0 / (1.0 + jnp.exp(-g))) * u).astype(BF16)


def _ffn_kernel(*refs, n_cast_slabs):
    if n_cast_slabs:
        (x_ref, pre_g_ref, wg_ref, wu_ref, wd_ref, post_g_ref, src_ref,
         o_hbm, dst_ref, acc_ref, h_ref, sem) = refs
    else:
        x_ref, pre_g_ref, wg_ref, wu_ref, wd_ref, post_g_ref, o_hbm, acc_ref, h_ref, sem = refs
    i = pl.program_id(0)
    f = pl.program_id(1)
    last = pl.num_programs(1) - 1
    if n_cast_slabs:
        @pl.when(i * pl.num_programs(1) + f < n_cast_slabs)
        def _():
            dst_ref[...] = src_ref[...].astype(BF16)
    n_chunks = FFN_TM // FFN_ROWS
    chunks = [pl.ds(c * FFN_ROWS, FFN_ROWS) for c in range(n_chunks)]

    def out_copy(tile, c):
        dst = o_hbm.at[pl.ds(pl.multiple_of(tile * FFN_TM + c * FFN_ROWS, FFN_ROWS), FFN_ROWS)]
        return pltpu.make_async_copy(acc_ref.at[chunks[c]], dst, sem.at[c])

    @pl.when(jnp.logical_and(f == 0, i > 0))
    def _():
        for c in range(n_chunks):
            out_copy(i - 1, c).wait()

    @pl.when(f == 0)
    def _():
        for rows in chunks:
            h = _rms(x_ref[rows, :], pre_g_ref[...]).astype(BF16)
            h_ref[rows, :] = h
            acc_ref[rows, :] = _dot_bf16(_swiglu_hidden(h, wg_ref[...], wu_ref[...]), wd_ref[...])

    @pl.when(jnp.logical_and(f > 0, f < last))
    def _():
        a = _swiglu_hidden(h_ref[...], wg_ref[...], wu_ref[...])
        acc_ref[...] += _dot_bf16(a, wd_ref[...])

    @pl.when(f == last)
    def _():
        for c, rows in enumerate(chunks):
            a = _swiglu_hidden(h_ref[rows, :], wg_ref[...], wu_ref[...])
            acc = acc_ref[rows, :] + _dot_bf16(a, wd_ref[...])
            acc_ref[rows, :] = x_ref[rows, :] + 0.5 * _rms(acc, post_g_ref[...])
            out_copy(i, c).start()

        @pl.when(i == pl.num_programs(0) - 1)
        def _():
            for c in range(n_chunks):
                out_copy(i, c).wait()


def _ffn(x, pre_g, wg, wu, wd, post_g, cast_src=None):
    t, d = x.shape
    n_f = D_FF // FFN_TF
    assert n_f >= 2, "first and last hidden steps must be distinct"
    nbytes = (3 * FFN_TM * d * 4 + FFN_TM * d * 2 + 2 * 3 * d * FFN_TF * wg.dtype.itemsize
              + 3 * FFN_TM * FFN_TF * 4)
    in_specs = [
        pl.BlockSpec((FFN_TM, d), lambda i, f: (i, 0)),
        pl.BlockSpec((1, d), lambda i, f: (0, 0)),
        pl.BlockSpec((d, FFN_TF), lambda i, f: (0, f)),
        pl.BlockSpec((d, FFN_TF), lambda i, f: (0, f)),
        pl.BlockSpec((FFN_TF, d), lambda i, f: (f, 0)),
        pl.BlockSpec((1, d), lambda i, f: (0, 0)),
    ]
    out_specs = [pl.BlockSpec(memory_space=pl.ANY)]
    out_shape = [jax.ShapeDtypeStruct((t, d), F32)]
    operands = [x, pre_g, wg, wu, wd, post_g]
    n_slabs = 0
    if cast_src is not None:
        rows, cols = cast_src.shape
        n_slabs = rows // FFN_CAST_ROWS
        assert rows % FFN_CAST_ROWS == 0 and n_slabs <= (t // FFN_TM) * n_f
        slab = pl.BlockSpec((FFN_CAST_ROWS, cols),
                            lambda i, f: (jnp.minimum(i * n_f + f, n_slabs - 1), 0))
        in_specs.append(slab)
        out_specs.append(slab)
        out_shape.append(jax.ShapeDtypeStruct((rows, cols), BF16))
        operands.append(cast_src)
        nbytes += 2 * FFN_CAST_ROWS * cols * (4 + 2)
    outs = pl.pallas_call(
        functools.partial(_ffn_kernel, n_cast_slabs=n_slabs),
        grid=(t // FFN_TM, n_f),
        in_specs=in_specs,
        out_specs=out_specs,
        out_shape=out_shape,
        scratch_shapes=[pltpu.VMEM((FFN_TM, d), F32), pltpu.VMEM((FFN_TM, d), BF16),
                        pltpu.SemaphoreType.DMA((FFN_TM // FFN_ROWS,))],
        compiler_params=pltpu.CompilerParams(
            dimension_semantics=("arbitrary", "arbitrary"),
            vmem_limit_bytes=_vmem_limit(nbytes)),
        name="ffn",
    )(*operands)
    return outs[0] if cast_src is None else tuple(outs)


_IN_LATENT = MLA_Q_RANK + MLA_KV_RANK
_IN_W = _IN_LATENT + MLA_ROPE_DIM + 2 * DIFF_QK_W + DIFF_OUT


def _rope_half(v, cos, sin_signed):
    return v * cos + pltpu.roll(v, DIFF_HEAD_DIM // 2, 1) * sin_signed


def _rope_mla(v, cos, sin_up, sin_dn):
    half = MLA_ROPE_DIM // 2
    return (v * cos + pltpu.roll(v, half, 1) * sin_up
            + pltpu.roll(v, V7X_LANES - half, 1) * sin_dn)


def _proj_kernel(x_ref, g_ref, win_ref, qg_ref, wuq_ref, kvg_ref, wukv_ref,
                 cm_ref, sup_ref, sdn_ref, cd_ref, sd_ref,
                 q_ref, k_ref, v_ref, dq_ref, dk_ref, dv_ref):
    h = _rms(x_ref[...], g_ref[...]).astype(BF16)

    pa = _dot_nt(h, win_ref[0:_IN_LATENT, :])
    cq = _rms(pa[:, 0:MLA_Q_RANK], qg_ref[...]).astype(BF16)
    ckv = _rms(pa[:, MLA_Q_RANK:_IN_LATENT], kvg_ref[...]).astype(BF16)
    rest = _dot_nt(h, win_ref[_IN_LATENT:_IN_W, :])
    cm, sup, sdn = cm_ref[...], sup_ref[...], sdn_ref[...]
    kpe = _rope_mla(rest[:, 0:V7X_LANES], cm, sup, sdn).astype(BF16)

    q = _dot_bf16(cq, wuq_ref[...]) * MLA_SCORE_SCALE
    kv = _dot_bf16(ckv, wukv_ref[...])
    for hh in range(MLA_HEADS):
        a = hh * MLA_HEAD_PAD
        b = a + V7X_LANES
        c = b + V7X_LANES
        qa = hh * MLA_QK_DIM
        q_ref[:, a:b] = q[:, qa:qa + MLA_NOPE_DIM].astype(BF16)
        if (qa + MLA_NOPE_DIM) % V7X_LANES == 0:
            pe = q[:, qa + MLA_NOPE_DIM:qa + MLA_NOPE_DIM + V7X_LANES]
        else:
            pe = pltpu.roll(q[:, qa + MLA_ROPE_DIM:qa + MLA_QK_DIM], MLA_ROPE_DIM, 1)
        q_ref[:, b:c] = _rope_mla(pe, cm, sup, sdn).astype(BF16)
        k_ref[:, a:b] = kv[:, a:b].astype(BF16)
        k_ref[:, b:c] = kpe
        v_ref[:, hh * MLA_V_DIM:(hh + 1) * MLA_V_DIM] = kv[:, b:c].astype(BF16)

    cd, sd = cd_ref[...], sd_ref[...]
    off = MLA_ROPE_DIM
    pdq = rest[:, off:off + DIFF_QK_W] * DIFF_SCORE_SCALE
    for c in range(DIFF_QK_W // DIFF_HEAD_DIM):
        cols = slice(c * DIFF_HEAD_DIM, (c + 1) * DIFF_HEAD_DIM)
        dq_ref[:, cols] = _rope_half(pdq[:, cols], cd, sd).astype(BF16)
    pdk = rest[:, off + DIFF_QK_W:off + 2 * DIFF_QK_W]
    for c in range(DIFF_QK_W // DIFF_HEAD_DIM):
        cols = slice(c * DIFF_HEAD_DIM, (c + 1) * DIFF_HEAD_DIM)
        dk_ref[:, cols] = _rope_half(pdk[:, cols], cd, sd).astype(BF16)
    dv_ref[...] = rest[:, off + 2 * DIFF_QK_W:].astype(BF16)


def _mix_proj(x, g, w_in_t, qg, wuq, kvg, wukv, tables, seq):
    t, d = x.shape
    tm = PROJ_TM
    pos_blocks = seq // tm
    const = lambda i: (0, 0)
    row = lambda i: (i, 0)
    pos = lambda i: (i % pos_blocks, 0)
    wq = MLA_HEADS * MLA_HEAD_PAD
    out_w = (wq, wq, MLA_OUT, DIFF_QK_W, DIFF_QK_W, DIFF_OUT)
    once = dict(pipeline_mode=pl.Buffered(1))
    nbytes = (d * _IN_W * 2 + MLA_Q_RANK * wq * (4 + 4) + 2 * tm * d * 4
              + 2 * tm * sum(out_w) * 2 + tm * _IN_W * 4 + 2 * tm * wq * 4)
    return pl.pallas_call(
        _proj_kernel,
        grid=(t // tm,),
        in_specs=[
            pl.BlockSpec((tm, d), row),
            pl.BlockSpec((1, d), const),
            pl.BlockSpec((_IN_W, d), const, **once),
            pl.BlockSpec((1, MLA_Q_RANK), const),
            pl.BlockSpec((MLA_Q_RANK, MLA_HEADS * MLA_QK_DIM), const, **once),
            pl.BlockSpec((1, MLA_KV_RANK), const),
            pl.BlockSpec((MLA_KV_RANK, wq), const, **once),
        ] + [pl.BlockSpec((tm, V7X_LANES), pos)] * 5,
        out_specs=[pl.BlockSpec((tm, w), row) for w in out_w],
        out_shape=[jax.ShapeDtypeStruct((t, w), BF16) for w in out_w],
        compiler_params=pltpu.CompilerParams(
            dimension_semantics=("parallel",),
            vmem_limit_bytes=_vmem_limit(nbytes)),
        name="mix_proj",
    )(x, g, w_in_t, qg, wuq, kvg, wukv, *tables)


def _causal_mask(n_q, n_k):
    r = lax.broadcasted_iota(jnp.int32, (n_q, n_k), 0) // CHUNK
    c = lax.broadcasted_iota(jnp.int32, (n_q, n_k), 1) // CHUNK
    return r >= c


def _flash_scratch(n_maps, v_dim):
    stat = pltpu.VMEM((ATT_TQ, V7X_LANES), F32)
    return [stat, stat, pltpu.VMEM((ATT_TQ, v_dim), F32)] * n_maps


def _flash_init(m_ref, l_ref, acc_ref):
    m_ref[...] = jnp.full(m_ref.shape, -jnp.inf, F32)
    l_ref[...] = jnp.zeros(l_ref.shape, F32)
    acc_ref[...] = jnp.zeros(acc_ref.shape, F32)


def _flash_step(t, v, m_ref, l_ref, acc_ref):
    lanes = m_ref.shape[1]
    m_old = m_ref[...]
    m_new = jnp.maximum(m_old, jnp.max(t, axis=-1, keepdims=True))
    alpha = jnp.exp2(m_old - m_new)
    p = jnp.exp2(t - jnp.tile(m_new, (1, t.shape[1] // lanes)))
    l_ref[...] = alpha * l_ref[...] + jnp.sum(p, axis=-1, keepdims=True)
    pv = jnp.dot(p.astype(BF16), v, preferred_element_type=F32)
    acc_ref[...] = jnp.tile(alpha, (1, acc_ref.shape[1] // lanes)) * acc_ref[...] + pv
    m_ref[...] = m_new


def _attn_tile(qi, n_maps, scores, values, state):
    half = ATT_TQ // 2
    assert ATT_TQ == ATT_TK and half % CHUNK == 0

    def update(q_rows, k_rows, mask):
        for n in range(n_maps):
            t = scores(n, q_rows, k_rows)
            if mask is not None:
                t = jnp.where(mask, t, -jnp.inf)
            _flash_step(t, values(n, k_rows), *[r.at[q_rows] for r in state[3 * n:3 * n + 3]])

    for n in range(n_maps):
        _flash_init(*state[3 * n:3 * n + 3])
    base = pl.multiple_of(qi * ATT_TQ, ATT_TQ)
    update(pl.ds(0, ATT_TQ), pl.ds(base, half), _causal_mask(ATT_TQ, half))
    update(pl.ds(half, half), pl.ds(pl.multiple_of(base + half, half), half),
           _causal_mask(half, half))

    grp = ATT_KGROUP

    def body(j, carry):
        update(pl.ds(0, ATT_TQ),
               pl.ds(pl.multiple_of(j * grp * ATT_TK, grp * ATT_TK), grp * ATT_TK), None)
        return carry

    lax.fori_loop(0, qi // grp, body, 0)
    for rem in range(1, grp):
        @pl.when(qi % grp == rem)
        def _():
            start = pl.multiple_of((qi - rem) * ATT_TK, ATT_TK)
            update(pl.ds(0, ATT_TQ), pl.ds(start, rem * ATT_TK), None)


def _qk(q, k):
    return lax.dot_general(q, k, (((1,), (1,)), ((), ())), preferred_element_type=F32)


def _mla_attn_kernel(q_ref, k_ref, v_ref, o_ref, *state):
    wq, wv = MLA_HEAD_PAD, MLA_V_DIM

    def scores(g, q_rows, k_rows):
        return _qk(q_ref[q_rows, g * wq:(g + 1) * wq], k_ref[k_rows, g * wq:(g + 1) * wq])

    def values(g, k_rows):
        return v_ref[k_rows, g * wv:(g + 1) * wv]

    _attn_tile(pl.program_id(2), MLA_GROUP, scores, values, state)
    for g in range(MLA_GROUP):
        _, l_ref, acc_ref = state[3 * g:3 * g + 3]
        o_ref[:, g * wv:(g + 1) * wv] = (acc_ref[...] / l_ref[...]).astype(o_ref.dtype)


def _mla_attn(q, k, v, batch, seq):
    nq = seq // ATT_TQ
    wq, wv = MLA_GROUP * MLA_HEAD_PAD, MLA_GROUP * MLA_V_DIM
    return pl.pallas_call(
        _mla_attn_kernel,
        grid=(batch, MLA_HEADS // MLA_GROUP, nq),
        in_specs=[
            pl.BlockSpec((ATT_TQ, wq), lambda b, h, i: (b * nq + i, h)),
            pl.BlockSpec((seq, wq), lambda b, h, i: (b, h)),
            pl.BlockSpec((seq, wv), lambda b, h, i: (b, h)),
        ],
        out_specs=pl.BlockSpec((ATT_TQ, wv), lambda b, h, i: (b * nq + i, h)),
        out_shape=jax.ShapeDtypeStruct((batch * seq, MLA_OUT), BF16),
        scratch_shapes=_flash_scratch(MLA_GROUP, MLA_V_DIM),
        compiler_params=pltpu.CompilerParams(
            dimension_semantics=("parallel", "parallel", "parallel"),
            vmem_limit_bytes=_vmem_limit(2 * (seq + ATT_TQ) * (wq + wv) * 2
                                         + 8 * MLA_GROUP * ATT_TQ * ATT_TK * 4)),
        name="mla_attn",
    )(q, k, v)


def _diff_attn_kernel(q_ref, k_ref, v_ref, lq1_ref, lk1_ref, lq2_ref, lk2_ref, g_ref, o_ref,
                      *state):
    d, wv = DIFF_HEAD_DIM, DIFF_V_DIM

    def scores(n, q_rows, k_rows):
        return _qk(q_ref[q_rows, n * d:(n + 1) * d], k_ref[k_rows, n * d:(n + 1) * d])

    def values(n, k_rows):
        return v_ref[k_rows, (n // 2) * wv:(n // 2 + 1) * wv]

    _attn_tile(pl.program_id(2), 2 * DIFF_GROUP, scores, values, state)
    lam = (jnp.exp(jnp.sum(lq1_ref[...] * lk1_ref[...], axis=-1, keepdims=True))
           - jnp.exp(jnp.sum(lq2_ref[...] * lk2_ref[...], axis=-1, keepdims=True))
           + LAMBDA_INIT)
    rep = wv // V7X_LANES
    for h in range(DIFF_GROUP):
        _, l1_ref, a1_ref, _, l2_ref, a2_ref = state[6 * h:6 * h + 6]
        o = (a1_ref[...] / jnp.tile(l1_ref[...], (1, rep))
             - lam * (a2_ref[...] / jnp.tile(l2_ref[...], (1, rep))))
        o_ref[:, h * wv:(h + 1) * wv] = (
            _rms(o, g_ref[...]) * (1.0 - LAMBDA_INIT)).astype(o_ref.dtype)


def _diff_attn(q, k, v, lq1, lk1, lq2, lk2, g, batch, seq):
    nq = seq // ATT_TQ
    w = DIFF_GROUP * DIFF_V_DIM
    vec = pl.BlockSpec((1, DIFF_HEAD_DIM), lambda b, h, i: (0, 0))
    return pl.pallas_call(
        _diff_attn_kernel,
        grid=(batch, DIFF_HEADS // DIFF_GROUP, nq),
        in_specs=[
            pl.BlockSpec((ATT_TQ, w), lambda b, h, i: (b * nq + i, h)),
            pl.BlockSpec((seq, w), lambda b, h, i: (b, h)),
            pl.BlockSpec((seq, w), lambda b, h, i: (b, h)),
            vec, vec, vec, vec,
            pl.BlockSpec((1, DIFF_V_DIM), lambda b, h, i: (0, 0)),
        ],
        out_specs=pl.BlockSpec((ATT_TQ, w), lambda b, h, i: (b * nq + i, h)),
        out_shape=jax.ShapeDtypeStruct((batch * seq, DIFF_OUT), BF16),
        scratch_shapes=_flash_scratch(2 * DIFF_GROUP, DIFF_V_DIM),
        compiler_params=pltpu.CompilerParams(
            dimension_semantics=("parallel", "parallel", "parallel"),
            vmem_limit_bytes=_vmem_limit(2 * (2 * seq + 2 * ATT_TQ) * w * 2
                                         + 16 * DIFF_GROUP * ATT_TQ * ATT_TK * 4)),
        name="diff_attn",
    )(q, k, v, lq1, lk1, lq2, lk2, g)


def _out_kernel(x_ref, oa_ref, ob_ref, wa_ref, wb_ref, g_ref, o_ref):
    o = _dot_bf16(oa_ref[...], wa_ref[...]) + _dot_bf16(ob_ref[...], wb_ref[...])
    o_ref[...] = x_ref[...] + _rms(o, g_ref[...])


def _out_proj(x, oa, ob, w, g):
    t, d = x.shape
    assert MLA_OUT == DIFF_OUT, "the two row blocks of w share one block shape"
    tm = OUT_TM
    const = lambda i: (0, 0)
    row = lambda i: (i, 0)
    once = dict(pipeline_mode=pl.Buffered(1))
    nbytes = (d * d * (w.dtype.itemsize + 2) + 4 * tm * d * 4 + 2 * tm * d * 2 + 2 * tm * d * 4)
    return pl.pallas_call(
        _out_kernel,
        grid=(t // tm,),
        in_specs=[
            pl.BlockSpec((tm, d), row),
            pl.BlockSpec((tm, MLA_OUT), row),
            pl.BlockSpec((tm, DIFF_OUT), row),
            pl.BlockSpec((MLA_OUT, d), const, **once),
            pl.BlockSpec((DIFF_OUT, d), lambda i: (1, 0), **once),
            pl.BlockSpec((1, d), const),
        ],
        out_specs=pl.BlockSpec((tm, d), row),
        out_shape=jax.ShapeDtypeStruct((t, d), F32),
        compiler_params=pltpu.CompilerParams(
            dimension_semantics=("parallel",),
            vmem_limit_bytes=_vmem_limit(nbytes)),
        name="out_proj",
    )(x, oa, ob, w, w, g)


def _rope_tables(seq):
    pos = np.arange(seq, dtype=np.float64)[:, None]

    def angles(d):
        inv_freq = ROPE_THETA ** (-np.arange(0, d, 2, dtype=np.float64) / d)
        return pos * inv_freq[None, :]

    ang = angles(MLA_ROPE_DIM)
    zeros = np.zeros((seq, V7X_LANES - MLA_ROPE_DIM))
    half0 = np.zeros_like(ang)
    cos_m = np.concatenate([np.cos(ang), np.cos(ang), zeros], axis=1)
    sin_up = np.concatenate([half0, np.sin(ang), zeros], axis=1)
    sin_dn = np.concatenate([-np.sin(ang), half0, zeros], axis=1)
    ang = angles(DIFF_HEAD_DIM)
    cos_d = np.concatenate([np.cos(ang), np.cos(ang)], axis=1)
    sin_d = np.concatenate([-np.sin(ang), np.sin(ang)], axis=1)
    return tuple(jnp.asarray(t, F32) for t in (cos_m, sin_up, sin_dn, cos_d, sin_d))


def kernel(x, ffn1_pre_g, ffn1_w_gate, ffn1_w_up, ffn1_w_down, ffn1_post_g, mix_pre_g, w_in, mla_q_norm_g, mla_w_uq, mla_kv_norm_g, mla_w_ukv, diff_lambda_q1, diff_lambda_k1, diff_lambda_q2, diff_lambda_k2, diff_subln_g, w_out, mix_post_g, ffn2_pre_g, ffn2_w_gate, ffn2_w_up, ffn2_w_down, ffn2_post_g):
    batch, seq, d = x.shape
    depth = ffn1_pre_g.shape[0]
    assert depth == 1 and d == D_MODEL
    assert seq % ATT_TQ == 0 and seq % PROJ_TM == 0 and (batch * seq) % FFN_TM == 0
    tables = _rope_tables(seq)
    xt = x.reshape(batch * seq, d)
    for l in range(depth):
        xt, w_in_bf = _ffn(xt, ffn1_pre_g[l][None], ffn1_w_gate[l], ffn1_w_up[l], ffn1_w_down[l],
                           ffn1_post_g[l][None], cast_src=w_in[l].T)
        q, k, v, dq, dk, dv = _mix_proj(
            xt, mix_pre_g[l][None], w_in_bf, mla_q_norm_g[l][None],
            mla_w_uq[l], mla_kv_norm_g[l][None], mla_w_ukv[l],
            tables, seq)
        o_mla = _mla_attn(q, k, v, batch, seq)
        o_diff = _diff_attn(dq, dk, dv, diff_lambda_q1[l][None], diff_lambda_k1[l][None],
                            diff_lambda_q2[l][None], diff_lambda_k2[l][None],
                            diff_subln_g[l][None], batch, seq)
        xt = _out_proj(xt, o_mla, o_diff, w_out[l], mix_post_g[l][None])
        xt = _ffn(xt, ffn2_pre_g[l][None], ffn2_w_gate[l], ffn2_w_up[l], ffn2_w_down[l],
                  ffn2_post_g[l][None])
    return xt.reshape(batch, seq, d)
```

```python
import functools
import math

import jax
import jax.numpy as jnp
import numpy as np
from jax import lax
from jax.experimental import pallas as pl
from jax.experimental.pallas import tpu as pltpu

D_MODEL = 2048
CHUNK = 64
ROPE_THETA = 10000.0
NORM_EPS = 1e-6
MLA_HEADS = 8
MLA_Q_RANK = 512
MLA_KV_RANK = 512
MLA_NOPE_DIM = 128
MLA_ROPE_DIM = 64
MLA_V_DIM = 128
MLA_QK_DIM = MLA_NOPE_DIM + MLA_ROPE_DIM
DIFF_HEADS = 4
DIFF_HEAD_DIM = 128
DIFF_V_DIM = 2 * DIFF_HEAD_DIM
MLA_OUT = MLA_HEADS * MLA_V_DIM
DIFF_OUT = DIFF_HEADS * DIFF_V_DIM
DIFF_QK_W = DIFF_HEADS * 2 * DIFF_HEAD_DIM
D_FF = 5632
LAMBDA_INIT = 0.8 - 0.6 * math.exp(-0.3 * 0)
LOG2_E = math.log2(math.e)
MLA_SCORE_SCALE = MLA_QK_DIM ** -0.5 * LOG2_E
DIFF_SCORE_SCALE = DIFF_HEAD_DIM ** -0.5 * LOG2_E

V7X_LANES = 128
V7X_VMEM_BYTES = 64 * 1024 * 1024
V7X_DEFAULT_SCOPED_VMEM = 32 * 1024 * 1024
V7X_VMEM_RESERVE = 4 * 1024 * 1024
MLA_HEAD_PAD = 2 * V7X_LANES

FFN_TM = 1024
FFN_TF = 512
FFN_ROWS = 256
FFN_CAST_ROWS = 64
PROJ_TM = 512
ATT_TQ = 512
ATT_TK = 512
ATT_KGROUP = 2
MLA_GROUP = 8
DIFF_GROUP = 4
OUT_TM = 512

BF16 = jnp.bfloat16
F32 = jnp.float32


def _vmem_limit(nbytes):
    return int(min(V7X_VMEM_BYTES - V7X_VMEM_RESERVE, max(V7X_DEFAULT_SCOPED_VMEM, nbytes * 3 // 2)))


def _dot_bf16(a, w):
    return lax.dot_general(a, w, (((1,), (0,)), ((), ())), preferred_element_type=F32)


def _dot_nt(a, w_t):
    return lax.dot_general(a, w_t, (((1,), (1,)), ((), ())), preferred_element_type=F32)


def _rms(v, g):
    ms = jnp.mean(v * v, axis=-1, keepdims=True)
    return v * lax.rsqrt(ms + NORM_EPS) * g


def _swiglu_hidden(h, wg, wu):
    g = _dot_bf16(h, wg)
    u = _dot_bf16(h, wu)
    return (g * (1.0 / (1.0 + jnp.exp(-g))) * u).astype(BF16)


def _ffn_kernel(*refs, n_cast_slabs):
    if n_cast_slabs:
        (x_ref, pre_g_ref, wg_ref, wu_ref, wd_ref, post_g_ref, src_ref,
         o_hbm, dst_ref, acc_ref, h_ref, sem) = refs
    else:
        x_ref, pre_g_ref, wg_ref, wu_ref, wd_ref, post_g_ref, o_hbm, acc_ref, h_ref, sem = refs
    i = pl.program_id(0)
    f = pl.program_id(1)
    last = pl.num_programs(1) - 1
    if n_cast_slabs:
        @pl.when(i * pl.num_programs(1) + f < n_cast_slabs)
        def _():
            dst_ref[...] = src_ref[...].astype(BF16)
    n_chunks = FFN_TM // FFN_ROWS
    chunks = [pl.ds(c * FFN_ROWS, FFN_ROWS) for c in range(n_chunks)]

    def out_copy(tile, c):
        dst = o_hbm.at[pl.ds(pl.multiple_of(tile * FFN_TM + c * FFN_ROWS, FFN_ROWS), FFN_ROWS)]
        return pltpu.make_async_copy(acc_ref.at[chunks[c]], dst, sem.at[c])

    @pl.when(jnp.logical_and(f == 0, i > 0))
    def _():
        for c in range(n_chunks):
            out_copy(i - 1, c).wait()

    @pl.when(f == 0)
    def _():
        for rows in chunks:
            h = _rms(x_ref[rows, :], pre_g_ref[...]).astype(BF16)
            h_ref[rows, :] = h
            acc_ref[rows, :] = _dot_bf16(_swiglu_hidden(h, wg_ref[...], wu_ref[...]), wd_ref[...])

    @pl.when(jnp.logical_and(f > 0, f < last))
    def _():
        a = _swiglu_hidden(h_ref[...], wg_ref[...], wu_ref[...])
        acc_ref[...] += _dot_bf16(a, wd_ref[...])

    @pl.when(f == last)
    def _():
        for c, rows in enumerate(chunks):
            a = _swiglu_hidden(h_ref[rows, :], wg_ref[...], wu_ref[...])
            acc = acc_ref[rows, :] + _dot_bf16(a, wd_ref[...])
            acc_ref[rows, :] = x_ref[rows, :] + 0.5 * _rms(acc, post_g_ref[...])
            out_copy(i, c).start()

        @pl.when(i == pl.num_programs(0) - 1)
        def _():
            for c in range(n_chunks):
                out_copy(i, c).wait()


def _ffn(x, pre_g, wg, wu, wd, post_g, cast_src=None):
    t, d = x.shape
    n_f = D_FF // FFN_TF
    assert n_f >= 2, "first and last hidden steps must be distinct"
    nbytes = (3 * FFN_TM * d * 4 + FFN_TM * d * 2 + 2 * 3 * d * FFN_TF * wg.dtype.itemsize
              + 3 * FFN_TM * FFN_TF * 4)
    in_specs = [
        pl.BlockSpec((FFN_TM, d), lambda i, f: (i, 0)),
        pl.BlockSpec((1, d), lambda i, f: (0, 0)),
        pl.BlockSpec((d, FFN_TF), lambda i, f: (0, f)),
        pl.BlockSpec((d, FFN_TF), lambda i, f: (0, f)),
        pl.BlockSpec((FFN_TF, d), lambda i, f: (f, 0)),
        pl.BlockSpec((1, d), lambda i, f: (0, 0)),
    ]
    out_specs = [pl.BlockSpec(memory_space=pl.ANY)]
    out_shape = [jax.ShapeDtypeStruct((t, d), F32)]
    operands = [x, pre_g, wg, wu, wd, post_g]
    n_slabs = 0
    if cast_src is not None:
        rows, cols = cast_src.shape
        n_slabs = rows // FFN_CAST_ROWS
        assert rows % FFN_CAST_ROWS == 0 and n_slabs <= (t // FFN_TM) * n_f
        slab = pl.BlockSpec((FFN_CAST_ROWS, cols),
                            lambda i, f: (jnp.minimum(i * n_f + f, n_slabs - 1), 0))
        in_specs.append(slab)
        out_specs.append(slab)
        out_shape.append(jax.ShapeDtypeStruct((rows, cols), BF16))
        operands.append(cast_src)
        nbytes += 2 * FFN_CAST_ROWS * cols * (4 + 2)
    outs = pl.pallas_call(
        functools.partial(_ffn_kernel, n_cast_slabs=n_slabs),
        grid=(t // FFN_TM, n_f),
        in_specs=in_specs,
        out_specs=out_specs,
        out_shape=out_shape,
        scratch_shapes=[pltpu.VMEM((FFN_TM, d), F32), pltpu.VMEM((FFN_TM, d), BF16),
                        pltpu.SemaphoreType.DMA((FFN_TM // FFN_ROWS,))],
        compiler_params=pltpu.CompilerParams(
            dimension_semantics=("arbitrary", "arbitrary"),
            vmem_limit_bytes=_vmem_limit(nbytes)),
        name="ffn",
    )(*operands)
    return outs[0] if cast_src is None else tuple(outs)


_IN_LATENT = MLA_Q_RANK + MLA_KV_RANK
_IN_W = _IN_LATENT + MLA_ROPE_DIM + 2 * DIFF_QK_W + DIFF_OUT


def _rope_half(v, cos, sin_signed):
    return v * cos + pltpu.roll(v, DIFF_HEAD_DIM // 2, 1) * sin_signed


def _rope_mla(v, cos, sin_up, sin_dn):
    half = MLA_ROPE_DIM // 2
    return (v * cos + pltpu.roll(v, half, 1) * sin_up
            + pltpu.roll(v, V7X_LANES - half, 1) * sin_dn)


def _proj_kernel(x_ref, g_ref, win_ref, qg_ref, wuq_ref, kvg_ref, wukv_ref,
                 cm_ref, sup_ref, sdn_ref, cd_ref, sd_ref,
                 q_ref, k_ref, v_ref, dq_ref, dk_ref, dv_ref):
    h = _rms(x_ref[...], g_ref[...]).astype(BF16)

    pa = _dot_nt(h, win_ref[0:_IN_LATENT, :])
    cq = _rms(pa[:, 0:MLA_Q_RANK], qg_ref[...]).astype(BF16)
    ckv = _rms(pa[:, MLA_Q_RANK:_IN_LATENT], kvg_ref[...]).astype(BF16)
    rest = _dot_nt(h, win_ref[_IN_LATENT:_IN_W, :])
    cm, sup, sdn = cm_ref[...], sup_ref[...], sdn_ref[...]
    kpe = _rope_mla(rest[:, 0:V7X_LANES], cm, sup, sdn).astype(BF16)

    q = _dot_bf16(cq, wuq_ref[...]) * MLA_SCORE_SCALE
    kv = _dot_bf16(ckv, wukv_ref[...])
    for hh in range(MLA_HEADS):
        a = hh * MLA_HEAD_PAD
        b = a + V7X_LANES
        c = b + V7X_LANES
        qa = hh * MLA_QK_DIM
        q_ref[:, a:b] = q[:, qa:qa + MLA_NOPE_DIM].astype(BF16)
        if (qa + MLA_NOPE_DIM) % V7X_LANES == 0:
            pe = q[:, qa + MLA_NOPE_DIM:qa + MLA_NOPE_DIM + V7X_LANES]
        else:
            pe = pltpu.roll(q[:, qa + MLA_ROPE_DIM:qa + MLA_QK_DIM], MLA_ROPE_DIM, 1)
        q_ref[:, b:c] = _rope_mla(pe, cm, sup, sdn).astype(BF16)
        k_ref[:, a:b] = kv[:, a:b].astype(BF16)
        k_ref[:, b:c] = kpe
        v_ref[:, hh * MLA_V_DIM:(hh + 1) * MLA_V_DIM] = kv[:, b:c].astype(BF16)

    cd, sd = cd_ref[...], sd_ref[...]
    off = MLA_ROPE_DIM
    pdq = rest[:, off:off + DIFF_QK_W] * DIFF_SCORE_SCALE
    for c in range(DIFF_QK_W // DIFF_HEAD_DIM):
        cols = slice(c * DIFF_HEAD_DIM, (c + 1) * DIFF_HEAD_DIM)
        dq_ref[:, cols] = _rope_half(pdq[:, cols], cd, sd).astype(BF16)
    pdk = rest[:, off + DIFF_QK_W:off + 2 * DIFF_QK_W]
    for c in range(DIFF_QK_W // DIFF_HEAD_DIM):
        cols = slice(c * DIFF_HEAD_DIM, (c + 1) * DIFF_HEAD_DIM)
        dk_ref[:, cols] = _rope_half(pdk[:, cols], cd, sd).astype(BF16)
    dv_ref[...] = rest[:, off + 2 * DIFF_QK_W:].astype(BF16)


def _mix_proj(x, g, w_in_t, qg, wuq, kvg, wukv, tables, seq):
    t, d = x.shape
    tm = PROJ_TM
    pos_blocks = seq // tm
    const = lambda i: (0, 0)
    row = lambda i: (i, 0)
    pos = lambda i: (i % pos_blocks, 0)
    wq = MLA_HEADS * MLA_HEAD_PAD
    out_w = (wq, wq, MLA_OUT, DIFF_QK_W, DIFF_QK_W, DIFF_OUT)
    once = dict(pipeline_mode=pl.Buffered(1))
    nbytes = (d * _IN_W * 2 + MLA_Q_RANK * wq * (4 + 4) + 2 * tm * d * 4
              + 2 * tm * sum(out_w) * 2 + tm * _IN_W * 4 + 2 * tm * wq * 4)
    return pl.pallas_call(
        _proj_kernel,
        grid=(t // tm,),
        in_specs=[
            pl.BlockSpec((tm, d), row),
            pl.BlockSpec((1, d), const),
            pl.BlockSpec((_IN_W, d), const, **once),
            pl.BlockSpec((1, MLA_Q_RANK), const),
            pl.BlockSpec((MLA_Q_RANK, MLA_HEADS * MLA_QK_DIM), const, **once),
            pl.BlockSpec((1, MLA_KV_RANK), const),
            pl.BlockSpec((MLA_KV_RANK, wq), const, **once),
        ] + [pl.BlockSpec((tm, V7X_LANES), pos)] * 5,
        out_specs=[pl.BlockSpec((tm, w), row) for w in out_w],
        out_shape=[jax.ShapeDtypeStruct((t, w), BF16) for w in out_w],
        compiler_params=pltpu.CompilerParams(
            dimension_semantics=("parallel",),
            vmem_limit_bytes=_vmem_limit(nbytes)),
        name="mix_proj",
    )(x, g, w_in_t, qg, wuq, kvg, wukv, *tables)


def _causal_mask(n_q, n_k):
    r = lax.broadcasted_iota(jnp.int32, (n_q, n_k), 0) // CHUNK
    c = lax.broadcasted_iota(jnp.int32, (n_q, n_k), 1) // CHUNK
    return r >= c


def _flash_scratch(n_maps, v_dim):
    stat = pltpu.VMEM((ATT_TQ, V7X_LANES), F32)
    return [stat, stat, pltpu.VMEM((ATT_TQ, v_dim), F32)] * n_maps


def _flash_init(m_ref, l_ref, acc_ref):
    m_ref[...] = jnp.full(m_ref.shape, -jnp.inf, F32)
    l_ref[...] = jnp.zeros(l_ref.shape, F32)
    acc_ref[...] = jnp.zeros(acc_ref.shape, F32)


def _flash_step(t, v, m_ref, l_ref, acc_ref):
    lanes = m_ref.shape[1]
    m_old = m_ref[...]
    m_new = jnp.maximum(m_old, jnp.max(t, axis=-1, keepdims=True))
    alpha = jnp.exp2(m_old - m_new)
    p = jnp.exp2(t - jnp.tile(m_new, (1, t.shape[1] // lanes)))
    l_ref[...] = alpha * l_ref[...] + jnp.sum(p, axis=-1, keepdims=True)
    pv = jnp.dot(p.astype(BF16), v, preferred_element_type=F32)
    acc_ref[...] = jnp.tile(alpha, (1, acc_ref.shape[1] // lanes)) * acc_ref[...] + pv
    m_ref[...] = m_new


def _attn_tile(qi, n_maps, scores, values, state):
    half = ATT_TQ // 2
    assert ATT_TQ == ATT_TK and half % CHUNK == 0

    def update(q_rows, k_rows, mask):
        for n in range(n_maps):
            t = scores(n, q_rows, k_rows)
            if mask is not None:
                t = jnp.where(mask, t, -jnp.inf)
            _flash_step(t, values(n, k_rows), *[r.at[q_rows] for r in state[3 * n:3 * n + 3]])

    for n in range(n_maps):
        _flash_init(*state[3 * n:3 * n + 3])
    base = pl.multiple_of(qi * ATT_TQ, ATT_TQ)
    update(pl.ds(0, ATT_TQ), pl.ds(base, half), _causal_mask(ATT_TQ, half))
    update(pl.ds(half, half), pl.ds(pl.multiple_of(base + half, half), half),
           _causal_mask(half, half))

    grp = ATT_KGROUP

    def body(j, carry):
        update(pl.ds(0, ATT_TQ),
               pl.ds(pl.multiple_of(j * grp * ATT_TK, grp * ATT_TK), grp * ATT_TK), None)
        return carry

    lax.fori_loop(0, qi // grp, body, 0)
    for rem in range(1, grp):
        @pl.when(qi % grp == rem)
        def _():
            start = pl.multiple_of((qi - rem) * ATT_TK, ATT_TK)
            update(pl.ds(0, ATT_TQ), pl.ds(start, rem * ATT_TK), None)


def _qk(q, k):
    return lax.dot_general(q, k, (((1,), (1,)), ((), ())), preferred_element_type=F32)


def _mla_attn_kernel(q_ref, k_ref, v_ref, o_ref, *state):
    wq, wv = MLA_HEAD_PAD, MLA_V_DIM

    def scores(g, q_rows, k_rows):
        return _qk(q_ref[q_rows, g * wq:(g + 1) * wq], k_ref[k_rows, g * wq:(g + 1) * wq])

    def values(g, k_rows):
        return v_ref[k_rows, g * wv:(g + 1) * wv]

    _attn_tile(pl.program_id(2), MLA_GROUP, scores, values, state)
    for g in range(MLA_GROUP):
        _, l_ref, acc_ref = state[3 * g:3 * g + 3]
        o_ref[:, g * wv:(g + 1) * wv] = (acc_ref[...] / l_ref[...]).astype(o_ref.dtype)


def _mla_attn(q, k, v, batch, seq):
    nq = seq // ATT_TQ
    wq, wv = MLA_GROUP * MLA_HEAD_PAD, MLA_GROUP * MLA_V_DIM
    return pl.pallas_call(
        _mla_attn_kernel,
        grid=(batch, MLA_HEADS // MLA_GROUP, nq),
        in_specs=[
            pl.BlockSpec((ATT_TQ, wq), lambda b, h, i: (b * nq + i, h)),
            pl.BlockSpec((seq, wq), lambda b, h, i: (b, h)),
            pl.BlockSpec((seq, wv), lambda b, h, i: (b, h)),
        ],
        out_specs=pl.BlockSpec((ATT_TQ, wv), lambda b, h, i: (b * nq + i, h)),
        out_shape=jax.ShapeDtypeStruct((batch * seq, MLA_OUT), BF16),
        scratch_shapes=_flash_scratch(MLA_GROUP, MLA_V_DIM),
        compiler_params=pltpu.CompilerParams(
            dimension_semantics=("parallel", "parallel", "parallel"),
            vmem_limit_bytes=_vmem_limit(2 * (seq + ATT_TQ) * (wq + wv) * 2
                                         + 8 * MLA_GROUP * ATT_TQ * ATT_TK * 4)),
        name="mla_attn",
    )(q, k, v)


def _diff_attn_kernel(q_ref, k_ref, v_ref, lq1_ref, lk1_ref, lq2_ref, lk2_ref, g_ref, o_ref,
                      *state):
    d, wv = DIFF_HEAD_DIM, DIFF_V_DIM

    def scores(n, q_rows, k_rows):
        return _qk(q_ref[q_rows, n * d:(n + 1) * d], k_ref[k_rows, n * d:(n + 1) * d])

    def values(n, k_rows):
        return v_ref[k_rows, (n // 2) * wv:(n // 2 + 1) * wv]

    _attn_tile(pl.program_id(2), 2 * DIFF_GROUP, scores, values, state)
    lam = (jnp.exp(jnp.sum(lq1_ref[...] * lk1_ref[...], axis=-1, keepdims=True))
           - jnp.exp(jnp.sum(lq2_ref[...] * lk2_ref[...], axis=-1, keepdims=True))
           + LAMBDA_INIT)
    rep = wv // V7X_LANES
    for h in range(DIFF_GROUP):
        _, l1_ref, a1_ref, _, l2_ref, a2_ref = state[6 * h:6 * h + 6]
        o = (a1_ref[...] / jnp.tile(l1_ref[...], (1, rep))
             - lam * (a2_ref[...] / jnp.tile(l2_ref[...], (1, rep))))
        o_ref[:, h * wv:(h + 1) * wv] = (
            _rms(o, g_ref[...]) * (1.0 - LAMBDA_INIT)).astype(o_ref.dtype)


def _diff_attn(q, k, v, lq1, lk1, lq2, lk2, g, batch, seq):
    nq = seq // ATT_TQ
    w = DIFF_GROUP * DIFF_V_DIM
    vec = pl.BlockSpec((1, DIFF_HEAD_DIM), lambda b, h, i: (0, 0))
    return pl.pallas_call(
        _diff_attn_kernel,
        grid=(batch, DIFF_HEADS // DIFF_GROUP, nq),
        in_specs=[
            pl.BlockSpec((ATT_TQ, w), lambda b, h, i: (b * nq + i, h)),
            pl.BlockSpec((seq, w), lambda b, h, i: (b, h)),
            pl.BlockSpec((seq, w), lambda b, h, i: (b, h)),
            vec, vec, vec, vec,
            pl.BlockSpec((1, DIFF_V_DIM), lambda b, h, i: (0, 0)),
        ],
        out_specs=pl.BlockSpec((ATT_TQ, w), lambda b, h, i: (b * nq + i, h)),
        out_shape=jax.ShapeDtypeStruct((batch * seq, DIFF_OUT), BF16),
        scratch_shapes=_flash_scratch(2 * DIFF_GROUP, DIFF_V_DIM),
        compiler_params=pltpu.CompilerParams(
            dimension_semantics=("parallel", "parallel", "parallel"),
            vmem_limit_bytes=_vmem_limit(2 * (2 * seq + 2 * ATT_TQ) * w * 2
                                         + 16 * DIFF_GROUP * ATT_TQ * ATT_TK * 4)),
        name="diff_attn",
    )(q, k, v, lq1, lk1, lq2, lk2, g)


def _out_kernel(x_ref, oa_ref, ob_ref, wa_ref, wb_ref, g_ref, o_ref):
    o = _dot_bf16(oa_ref[...], wa_ref[...]) + _dot_bf16(ob_ref[...], wb_ref[...])
    o_ref[...] = x_ref[...] + _rms(o, g_ref[...])


def _out_proj(x, oa, ob, w, g):
    t, d = x.shape
    assert MLA_OUT == DIFF_OUT, "the two row blocks of w share one block shape"
    tm = OUT_TM
    const = lambda i: (0, 0)
    row = lambda i: (i, 0)
    once = dict(pipeline_mode=pl.Buffered(1))
    nbytes = (d * d * (w.dtype.itemsize + 2) + 4 * tm * d * 4 + 2 * tm * d * 2 + 2 * tm * d * 4)
    return pl.pallas_call(
        _out_kernel,
        grid=(t // tm,),
        in_specs=[
            pl.BlockSpec((tm, d), row),
            pl.BlockSpec((tm, MLA_OUT), row),
            pl.BlockSpec((tm, DIFF_OUT), row),
            pl.BlockSpec((MLA_OUT, d), const, **once),
            pl.BlockSpec((DIFF_OUT, d), lambda i: (1, 0), **once),
            pl.BlockSpec((1, d), const),
        ],
        out_specs=pl.BlockSpec((tm, d), row),
        out_shape=jax.ShapeDtypeStruct((t, d), F32),
        compiler_params=pltpu.CompilerParams(
            dimension_semantics=("parallel",),
            vmem_limit_bytes=_vmem_limit(nbytes)),
        name="out_proj",
    )(x, oa, ob, w, w, g)


def _rope_tables(seq):
    pos = np.arange(seq, dtype=np.float64)[:, None]

    def angles(d):
        inv_freq = ROPE_THETA ** (-np.arange(0, d, 2, dtype=np.float64) / d)
        return pos * inv_freq[None, :]

    ang = angles(MLA_ROPE_DIM)
    zeros = np.zeros((seq, V7X_LANES - MLA_ROPE_DIM))
    half0 = np.zeros_like(ang)
    cos_m = np.concatenate([np.cos(ang), np.cos(ang), zeros], axis=1)
    sin_up = np.concatenate([half0, np.sin(ang), zeros], axis=1)
    sin_dn = np.concatenate([-np.sin(ang), half0, zeros], axis=1)
    ang = angles(DIFF_HEAD_DIM)
    cos_d = np.concatenate([np.cos(ang), np.cos(ang)], axis=1)
    sin_d = np.concatenate([-np.sin(ang), np.sin(ang)], axis=1)
    return tuple(jnp.asarray(t, F32) for t in (cos_m, sin_up, sin_dn, cos_d, sin_d))


def kernel(x, ffn1_pre_g, ffn1_w_gate, ffn1_w_up, ffn1_w_down, ffn1_post_g, mix_pre_g, w_in, mla_q_norm_g, mla_w_uq, mla_kv_norm_g, mla_w_ukv, diff_lambda_q1, diff_lambda_k1, diff_lambda_q2, diff_lambda_k2, diff_subln_g, w_out, mix_post_g, ffn2_pre_g, ffn2_w_gate, ffn2_w_up, ffn2_w_down, ffn2_post_g):
    batch, seq, d = x.shape
    depth = ffn1_pre_g.shape[0]
    assert depth == 1 and d == D_MODEL
    assert seq % ATT_TQ == 0 and seq % PROJ_TM == 0 and (batch * seq) % FFN_TM == 0
    tables = _rope_tables(seq)
    xt = x.reshape(batch * seq, d)
    for l in range(depth):
        xt, w_in_bf = _ffn(xt, ffn1_pre_g[l][None], ffn1_w_gate[l], ffn1_w_up[l], ffn1_w_down[l],
                           ffn1_post_g[l][None], cast_src=w_in[l].T)
        q, k, v, dq, dk, dv = _mix_proj(
            xt, mix_pre_g[l][None], w_in_bf, mla_q_norm_g[l][None],
            mla_w_uq[l], mla_kv_norm_g[l][None], mla_w_ukv[l],
            tables, seq)
        o_mla = _mla_attn(q, k, v, batch, seq)
        o_diff = _diff_attn(dq, dk, dv, diff_lambda_q1[l][None], diff_lambda_k1[l][None],
                            diff_lambda_q2[l][None], diff_lambda_k2[l][None],
                            diff_subln_g[l][None], batch, seq)
        xt = _out_proj(xt, o_mla, o_diff, w_out[l], mix_post_g[l][None])
        xt = _ffn(xt, ffn2_pre_g[l][None], ffn2_w_gate[l], ffn2_w_up[l], ffn2_w_down[l],
                  ffn2_post_g[l][None])
    return xt.reshape(batch, seq, d)
```

```python
import functools
import math

import jax
import jax.numpy as jnp
import numpy as np
from jax import lax
from jax.experimental import pallas as pl
from jax.experimental.pallas import tpu as pltpu

D_MODEL = 2048
CHUNK = 64
ROPE_THETA = 10000.0
NORM_EPS = 1e-6
MLA_HEADS = 8
MLA_Q_RANK = 512
MLA_KV_RANK = 512
MLA_NOPE_DIM = 128
MLA_ROPE_DIM = 64
MLA_V_DIM = 128
MLA_QK_DIM = MLA_NOPE_DIM + MLA_ROPE_DIM
DIFF_HEADS = 4
DIFF_HEAD_DIM = 128
DIFF_V_DIM = 2 * DIFF_HEAD_DIM
MLA_OUT = MLA_HEADS * MLA_V_DIM
DIFF_OUT = DIFF_HEADS * DIFF_V_DIM
DIFF_QK_W = DIFF_HEADS * 2 * DIFF_HEAD_DIM
D_FF = 5632
LAMBDA_INIT = 0.8 - 0.6 * math.exp(-0.3 * 0)
LOG2_E = math.log2(math.e)
MLA_SCORE_SCALE = MLA_QK_DIM ** -0.5 * LOG2_E
DIFF_SCORE_SCALE = DIFF_HEAD_DIM ** -0.5 * LOG2_E

V7X_LANES = 128
V7X_VMEM_BYTES = 64 * 1024 * 1024
V7X_DEFAULT_SCOPED_VMEM = 32 * 1024 * 1024
V7X_VMEM_RESERVE = 4 * 1024 * 1024
MLA_HEAD_PAD = 2 * V7X_LANES

FFN_TM = 1024
FFN_TF = 512
FFN_ROWS = 256
FFN_CAST_ROWS = 64
PROJ_TM = 512
ATT_TQ = 512
ATT_TK = 512
ATT_KGROUP = 2
MLA_GROUP = 8
DIFF_GROUP = 4
OUT_TM = 512

BF16 = jnp.bfloat16
F32 = jnp.float32


def _vmem_limit(nbytes):
    return int(min(V7X_VMEM_BYTES - V7X_VMEM_RESERVE, max(V7X_DEFAULT_SCOPED_VMEM, nbytes * 3 // 2)))


def _dot_bf16(a, w):
    return lax.dot_general(a, w, (((1,), (0,)), ((), ())), preferred_element_type=F32)


def _dot_nt(a, w_t):
    return lax.dot_general(a, w_t, (((1,), (1,)), ((), ())), preferred_element_type=F32)


def _rms(v, g):
    ms = jnp.mean(v * v, axis=-1, keepdims=True)
    return v * lax.rsqrt(ms + NORM_EPS) * g


def _swiglu_hidden(h, wg, wu):
    g = _dot_bf16(h, wg)
    u = _dot_bf16(h, wu)
    return (g * (1.0 / (1.0 + jnp.exp(-g))) * u).astype(BF16)


def _ffn_kernel(*refs, n_cast_slabs):
    if n_cast_slabs:
        (x_ref, pre_g_ref, wg_ref, wu_ref, wd_ref, post_g_ref, src_ref,
         o_hbm, dst_ref, acc_ref, h_ref, sem) = refs
    else:
        x_ref, pre_g_ref, wg_ref, wu_ref, wd_ref, post_g_ref, o_hbm, acc_ref, h_ref, sem = refs
    i = pl.program_id(0)
    f = pl.program_id(1)
    last = pl.num_programs(1) - 1
    if n_cast_slabs:
        @pl.when(i * pl.num_programs(1) + f < n_cast_slabs)
        def _():
            dst_ref[...] = src_ref[...].astype(BF16)
    n_chunks = FFN_TM // FFN_ROWS
    chunks = [pl.ds(c * FFN_ROWS, FFN_ROWS) for c in range(n_chunks)]

    def out_copy(tile, c):
        dst = o_hbm.at[pl.ds(pl.multiple_of(tile * FFN_TM + c * FFN_ROWS, FFN_ROWS), FFN_ROWS)]
        return pltpu.make_async_copy(acc_ref.at[chunks[c]], dst, sem.at[c])

    @pl.when(jnp.logical_and(f == 0, i > 0))
    def _():
        for c in range(n_chunks):
            out_copy(i - 1, c).wait()

    @pl.when(f == 0)
    def _():
        for rows in chunks:
            h = _rms(x_ref[rows, :], pre_g_ref[...]).astype(BF16)
            h_ref[rows, :] = h
            acc_ref[rows, :] = _dot_bf16(_swiglu_hidden(h, wg_ref[...], wu_ref[...]), wd_ref[...])

    @pl.when(jnp.logical_and(f > 0, f < last))
    def _():
        a = _swiglu_hidden(h_ref[...], wg_ref[...], wu_ref[...])
        acc_ref[...] += _dot_bf16(a, wd_ref[...])

    @pl.when(f == last)
    def _():
        for c, rows in enumerate(chunks):
            a = _swiglu_hidden(h_ref[rows, :], wg_ref[...], wu_ref[...])
            acc = acc_ref[rows, :] + _dot_bf16(a, wd_ref[...])
            acc_ref[rows, :] = x_ref[rows, :] + 0.5 * _rms(acc, post_g_ref[...])
            out_copy(i, c).start()

        @pl.when(i == pl.num_programs(0) - 1)
        def _():
            for c in range(n_chunks):
                out_copy(i, c).wait()


def _ffn(x, pre_g, wg, wu, wd, post_g, cast_src=None):
    t, d = x.shape
    n_f = D_FF // FFN_TF
    assert n_f >= 2, "first and last hidden steps must be distinct"
    nbytes = (3 * FFN_TM * d * 4 + FFN_TM * d * 2 + 2 * 3 * d * FFN_TF * wg.dtype.itemsize
              + 3 * FFN_TM * FFN_TF * 4)
    in_specs = [
        pl.BlockSpec((FFN_TM, d), lambda i, f: (i, 0)),
        pl.BlockSpec((1, d), lambda i, f: (0, 0)),
        pl.BlockSpec((d, FFN_TF), lambda i, f: (0, f)),
        pl.BlockSpec((d, FFN_TF), lambda i, f: (0, f)),
        pl.BlockSpec((FFN_TF, d), lambda i, f: (f, 0)),
        pl.BlockSpec((1, d), lambda i, f: (0, 0)),
    ]
    out_specs = [pl.BlockSpec(memory_space=pl.ANY)]
    out_shape = [jax.ShapeDtypeStruct((t, d), F32)]
    operands = [x, pre_g, wg, wu, wd, post_g]
    n_slabs = 0
    if cast_src is not None:
        rows, cols = cast_src.shape
        n_slabs = rows // FFN_CAST_ROWS
        assert rows % FFN_CAST_ROWS == 0 and n_slabs <= (t // FFN_TM) * n_f
        slab = pl.BlockSpec((FFN_CAST_ROWS, cols),
                            lambda i, f: (jnp.minimum(i * n_f + f, n_slabs - 1), 0))
        in_specs.append(slab)
        out_specs.append(slab)
        out_shape.append(jax.ShapeDtypeStruct((rows, cols), BF16))
        operands.append(cast_src)
        nbytes += 2 * FFN_CAST_ROWS * cols * (4 + 2)
    outs = pl.pallas_call(
        functools.partial(_ffn_kernel, n_cast_slabs=n_slabs),
        grid=(t // FFN_TM, n_f),
        in_specs=in_specs,
        out_specs=out_specs,
        out_shape=out_shape,
        scratch_shapes=[pltpu.VMEM((FFN_TM, d), F32), pltpu.VMEM((FFN_TM, d), BF16),
                        pltpu.SemaphoreType.DMA((FFN_TM // FFN_ROWS,))],
        compiler_params=pltpu.CompilerParams(
            dimension_semantics=("arbitrary", "arbitrary"),
            vmem_limit_bytes=_vmem_limit(nbytes)),
        name="ffn",
    )(*operands)
    return outs[0] if cast_src is None else tuple(outs)


_IN_LATENT = MLA_Q_RANK + MLA_KV_RANK
_IN_W = _IN_LATENT + MLA_ROPE_DIM + 2 * DIFF_QK_W + DIFF_OUT


def _rope_half(v, cos, sin_signed):
    return v * cos + pltpu.roll(v, DIFF_HEAD_DIM // 2, 1) * sin_signed


def _rope_mla(v, cos, sin_up, sin_dn):
    half = MLA_ROPE_DIM // 2
    return (v * cos + pltpu.roll(v, half, 1) * sin_up
            + pltpu.roll(v, V7X_LANES - half, 1) * sin_dn)


def _proj_kernel(x_ref, g_ref, win_ref, qg_ref, wuq_ref, kvg_ref, wukv_ref,
                 cm_ref, sup_ref, sdn_ref, cd_ref, sd_ref,
                 q_ref, k_ref, v_ref, dq_ref, dk_ref, dv_ref):
    h = _rms(x_ref[...], g_ref[...]).astype(BF16)

    pa = _dot_nt(h, win_ref[0:_IN_LATENT, :])
    cq = _rms(pa[:, 0:MLA_Q_RANK], qg_ref[...]).astype(BF16)
    ckv = _rms(pa[:, MLA_Q_RANK:_IN_LATENT], kvg_ref[...]).astype(BF16)
    rest = _dot_nt(h, win_ref[_IN_LATENT:_IN_W, :])
    cm, sup, sdn = cm_ref[...], sup_ref[...], sdn_ref[...]
    kpe = _rope_mla(rest[:, 0:V7X_LANES], cm, sup, sdn).astype(BF16)

    q = _dot_bf16(cq, wuq_ref[...]) * MLA_SCORE_SCALE
    kv = _dot_bf16(ckv, wukv_ref[...])
    for hh in range(MLA_HEADS):
        a = hh * MLA_HEAD_PAD
        b = a + V7X_LANES
        c = b + V7X_LANES
        qa = hh * MLA_QK_DIM
        q_ref[:, a:b] = q[:, qa:qa + MLA_NOPE_DIM].astype(BF16)
        if (qa + MLA_NOPE_DIM) % V7X_LANES == 0:
            pe = q[:, qa + MLA_NOPE_DIM:qa + MLA_NOPE_DIM + V7X_LANES]
        else:
            pe = pltpu.roll(q[:, qa + MLA_ROPE_DIM:qa + MLA_QK_DIM], MLA_ROPE_DIM, 1)
        q_ref[:, b:c] = _rope_mla(pe, cm, sup, sdn).astype(BF16)
        k_ref[:, a:b] = kv[:, a:b].astype(BF16)
        k_ref[:, b:c] = kpe
        v_ref[:, a:b] = kv[:, b:c].astype(BF16)
        v_ref[:, b:c] = jnp.ones((x_ref.shape[0], V7X_LANES), BF16)

    cd, sd = cd_ref[...], sd_ref[...]
    off = MLA_ROPE_DIM
    pdq = rest[:, off:off + DIFF_QK_W] * DIFF_SCORE_SCALE
    for c in range(DIFF_QK_W // DIFF_HEAD_DIM):
        cols = slice(c * DIFF_HEAD_DIM, (c + 1) * DIFF_HEAD_DIM)
        dq_ref[:, cols] = _rope_half(pdq[:, cols], cd, sd).astype(BF16)
    pdk = rest[:, off + DIFF_QK_W:off + 2 * DIFF_QK_W]
    for c in range(DIFF_QK_W // DIFF_HEAD_DIM):
        cols = slice(c * DIFF_HEAD_DIM, (c + 1) * DIFF_HEAD_DIM)
        dk_ref[:, cols] = _rope_half(pdk[:, cols], cd, sd).astype(BF16)
    dv_ref[...] = rest[:, off + 2 * DIFF_QK_W:].astype(BF16)


def _mix_proj(x, g, w_in_t, qg, wuq, kvg, wukv, tables, seq):
    t, d = x.shape
    tm = PROJ_TM
    pos_blocks = seq // tm
    const = lambda i: (0, 0)
    row = lambda i: (i, 0)
    pos = lambda i: (i % pos_blocks, 0)
    wq = MLA_HEADS * MLA_HEAD_PAD
    assert MLA_HEAD_PAD == 2 * MLA_V_DIM
    out_w = (wq, wq, 2 * MLA_OUT, DIFF_QK_W, DIFF_QK_W, DIFF_OUT)
    once = dict(pipeline_mode=pl.Buffered(1))
    nbytes = (d * _IN_W * 2 + MLA_Q_RANK * wq * (4 + 4) + 2 * tm * d * 4
              + 2 * tm * sum(out_w) * 2 + tm * _IN_W * 4 + 2 * tm * wq * 4)
    return pl.pallas_call(
        _proj_kernel,
        grid=(t // tm,),
        in_specs=[
            pl.BlockSpec((tm, d), row),
            pl.BlockSpec((1, d), const),
            pl.BlockSpec((_IN_W, d), const, **once),
            pl.BlockSpec((1, MLA_Q_RANK), const),
            pl.BlockSpec((MLA_Q_RANK, MLA_HEADS * MLA_QK_DIM), const, **once),
            pl.BlockSpec((1, MLA_KV_RANK), const),
            pl.BlockSpec((MLA_KV_RANK, wq), const, **once),
        ] + [pl.BlockSpec((tm, V7X_LANES), pos)] * 5,
        out_specs=[pl.BlockSpec((tm, w), row) for w in out_w],
        out_shape=[jax.ShapeDtypeStruct((t, w), BF16) for w in out_w],
        compiler_params=pltpu.CompilerParams(
            dimension_semantics=("parallel",),
            vmem_limit_bytes=_vmem_limit(nbytes)),
        name="mix_proj",
    )(x, g, w_in_t, qg, wuq, kvg, wukv, *tables)


def _causal_mask(n_q, n_k):
    r = lax.broadcasted_iota(jnp.int32, (n_q, n_k), 0) // CHUNK
    c = lax.broadcasted_iota(jnp.int32, (n_q, n_k), 1) // CHUNK
    return r >= c


def _flash_scratch(n_maps, acc_width, with_sum):
    stat = pltpu.VMEM((ATT_TQ, V7X_LANES), F32)
    per_map = [stat] * (2 if with_sum else 1) + [pltpu.VMEM((ATT_TQ, acc_width), F32)]
    return per_map * n_maps


def _flash_maps(state, n_maps, with_sum):
    k = len(state) // n_maps
    return [(state[k * n], state[k * n + 1] if with_sum else None, state[k * n + k - 1])
            for n in range(n_maps)]


def _flash_step(t, v, m_ref, l_ref, acc_ref):
    lanes = m_ref.shape[1]
    m_old = m_ref[...]
    m_new = jnp.maximum(m_old, jnp.max(t, axis=-1, keepdims=True))
    alpha = jnp.exp2(m_old - m_new)
    p = jnp.exp2(t - jnp.tile(m_new, (1, t.shape[1] // lanes)))
    if l_ref is not None:
        l_ref[...] = alpha * l_ref[...] + jnp.sum(p, axis=-1, keepdims=True)
    pv = jnp.dot(p.astype(BF16), v, preferred_element_type=F32)
    acc_ref[...] = jnp.tile(alpha, (1, acc_ref.shape[1] // lanes)) * acc_ref[...] + pv
    m_ref[...] = m_new


def _attn_tile(qi, maps, scores, values):
    half = ATT_TQ // 2
    assert ATT_TQ == ATT_TK and half % CHUNK == 0

    def update(q_rows, k_rows, mask):
        for n, refs in enumerate(maps):
            t = scores(n, q_rows, k_rows)
            if mask is not None:
                t = jnp.where(mask, t, -jnp.inf)
            _flash_step(t, values(n, k_rows), *[r if r is None else r.at[q_rows] for r in refs])

    for m_ref, l_ref, acc_ref in maps:
        m_ref[...] = jnp.full(m_ref.shape, -jnp.inf, F32)
        if l_ref is not None:
            l_ref[...] = jnp.zeros(l_ref.shape, F32)
        acc_ref[...] = jnp.zeros(acc_ref.shape, F32)
    base = pl.multiple_of(qi * ATT_TQ, ATT_TQ)
    update(pl.ds(0, ATT_TQ), pl.ds(base, half), _causal_mask(ATT_TQ, half))
    update(pl.ds(half, half), pl.ds(pl.multiple_of(base + half, half), half),
           _causal_mask(half, half))

    grp = ATT_KGROUP

    def body(j, carry):
        update(pl.ds(0, ATT_TQ),
               pl.ds(pl.multiple_of(j * grp * ATT_TK, grp * ATT_TK), grp * ATT_TK), None)
        return carry

    lax.fori_loop(0, qi // grp, body, 0)
    for rem in range(1, grp):
        @pl.when(qi % grp == rem)
        def _():
            start = pl.multiple_of((qi - rem) * ATT_TK, ATT_TK)
            update(pl.ds(0, ATT_TQ), pl.ds(start, rem * ATT_TK), None)


def _qk(q, k):
    return lax.dot_general(q, k, (((1,), (1,)), ((), ())), preferred_element_type=F32)


def _mla_attn_kernel(q_ref, k_ref, v_ref, o_ref, *state):
    wq, wv = MLA_HEAD_PAD, MLA_V_DIM

    def scores(g, q_rows, k_rows):
        return _qk(q_ref[q_rows, g * wq:(g + 1) * wq], k_ref[k_rows, g * wq:(g + 1) * wq])

    def values(g, k_rows):
        return v_ref[k_rows, g * 2 * wv:(g + 1) * 2 * wv]

    maps = _flash_maps(state, MLA_GROUP, with_sum=False)
    _attn_tile(pl.program_id(2), maps, scores, values)
    for g, (_, _, acc_ref) in enumerate(maps):
        o_ref[:, g * wv:(g + 1) * wv] = (acc_ref[:, 0:wv] / acc_ref[:, wv:2 * wv]).astype(o_ref.dtype)


def _mla_attn(q, k, v, batch, seq):
    nq = seq // ATT_TQ
    wq, wv = MLA_GROUP * MLA_HEAD_PAD, MLA_GROUP * MLA_V_DIM
    return pl.pallas_call(
        _mla_attn_kernel,
        grid=(batch, MLA_HEADS // MLA_GROUP, nq),
        in_specs=[
            pl.BlockSpec((ATT_TQ, wq), lambda b, h, i: (b * nq + i, h)),
            pl.BlockSpec((seq, wq), lambda b, h, i: (b, h)),
            pl.BlockSpec((seq, 2 * wv), lambda b, h, i: (b, h)),
        ],
        out_specs=pl.BlockSpec((ATT_TQ, wv), lambda b, h, i: (b * nq + i, h)),
        out_shape=jax.ShapeDtypeStruct((batch * seq, MLA_OUT), BF16),
        scratch_shapes=_flash_scratch(MLA_GROUP, 2 * MLA_V_DIM, with_sum=False),
        compiler_params=pltpu.CompilerParams(
            dimension_semantics=("parallel", "parallel", "parallel"),
            vmem_limit_bytes=_vmem_limit(2 * (seq + ATT_TQ) * (wq + 2 * wv) * 2
                                         + 8 * MLA_GROUP * ATT_TQ * ATT_TK * 4)),
        name="mla_attn",
    )(q, k, v)


def _diff_attn_kernel(q_ref, k_ref, v_ref, lq1_ref, lk1_ref, lq2_ref, lk2_ref, g_ref, o_ref,
                      *state):
    d, wv = DIFF_HEAD_DIM, DIFF_V_DIM

    def scores(n, q_rows, k_rows):
        return _qk(q_ref[q_rows, n * d:(n + 1) * d], k_ref[k_rows, n * d:(n + 1) * d])

    def values(n, k_rows):
        return v_ref[k_rows, (n // 2) * wv:(n // 2 + 1) * wv]

    maps = _flash_maps(state, 2 * DIFF_GROUP, with_sum=True)
    _attn_tile(pl.program_id(2), maps, scores, values)
    lam = (jnp.exp(jnp.sum(lq1_ref[...] * lk1_ref[...], axis=-1, keepdims=True))
           - jnp.exp(jnp.sum(lq2_ref[...] * lk2_ref[...], axis=-1, keepdims=True))
           + LAMBDA_INIT)
    rep = wv // V7X_LANES
    for h in range(DIFF_GROUP):
        (_, l1_ref, a1_ref), (_, l2_ref, a2_ref) = maps[2 * h:2 * h + 2]
        o = (a1_ref[...] / jnp.tile(l1_ref[...], (1, rep))
             - lam * (a2_ref[...] / jnp.tile(l2_ref[...], (1, rep))))
        o_ref[:, h * wv:(h + 1) * wv] = (
            _rms(o, g_ref[...]) * (1.0 - LAMBDA_INIT)).astype(o_ref.dtype)


def _diff_attn(q, k, v, lq1, lk1, lq2, lk2, g, batch, seq):
    nq = seq // ATT_TQ
    w = DIFF_GROUP * DIFF_V_DIM
    vec = pl.BlockSpec((1, DIFF_HEAD_DIM), lambda b, h, i: (0, 0))
    return pl.pallas_call(
        _diff_attn_kernel,
        grid=(batch, DIFF_HEADS // DIFF_GROUP, nq),
        in_specs=[
            pl.BlockSpec((ATT_TQ, w), lambda b, h, i: (b * nq + i, h)),
            pl.BlockSpec((seq, w), lambda b, h, i: (b, h)),
            pl.BlockSpec((seq, w), lambda b, h, i: (b, h)),
            vec, vec, vec, vec,
            pl.BlockSpec((1, DIFF_V_DIM), lambda b, h, i: (0, 0)),
        ],
        out_specs=pl.BlockSpec((ATT_TQ, w), lambda b, h, i: (b * nq + i, h)),
        out_shape=jax.ShapeDtypeStruct((batch * seq, DIFF_OUT), BF16),
        scratch_shapes=_flash_scratch(2 * DIFF_GROUP, DIFF_V_DIM, with_sum=True),
        compiler_params=pltpu.CompilerParams(
            dimension_semantics=("parallel", "parallel", "parallel"),
            vmem_limit_bytes=_vmem_limit(2 * (2 * seq + 2 * ATT_TQ) * w * 2
                                         + 16 * DIFF_GROUP * ATT_TQ * ATT_TK * 4)),
        name="diff_attn",
    )(q, k, v, lq1, lk1, lq2, lk2, g)


def _out_kernel(x_ref, oa_ref, ob_ref, wa_ref, wb_ref, g_ref, o_ref):
    o = _dot_bf16(oa_ref[...], wa_ref[...]) + _dot_bf16(ob_ref[...], wb_ref[...])
    o_ref[...] = x_ref[...] + _rms(o, g_ref[...])


def _out_proj(x, oa, ob, w, g):
    t, d = x.shape
    assert MLA_OUT == DIFF_OUT, "the two row blocks of w share one block shape"
    tm = OUT_TM
    const = lambda i: (0, 0)
    row = lambda i: (i, 0)
    once = dict(pipeline_mode=pl.Buffered(1))
    nbytes = (d * d * (w.dtype.itemsize + 2) + 4 * tm * d * 4 + 2 * tm * d * 2 + 2 * tm * d * 4)
    return pl.pallas_call(
        _out_kernel,
        grid=(t // tm,),
        in_specs=[
            pl.BlockSpec((tm, d), row),
            pl.BlockSpec((tm, MLA_OUT), row),
            pl.BlockSpec((tm, DIFF_OUT), row),
            pl.BlockSpec((MLA_OUT, d), const, **once),
            pl.BlockSpec((DIFF_OUT, d), lambda i: (1, 0), **once),
            pl.BlockSpec((1, d), const),
        ],
        out_specs=pl.BlockSpec((tm, d), row),
        out_shape=jax.ShapeDtypeStruct((t, d), F32),
        compiler_params=pltpu.CompilerParams(
            dimension_semantics=("parallel",),
            vmem_limit_bytes=_vmem_limit(nbytes)),
        name="out_proj",
    )(x, oa, ob, w, w, g)


def _rope_tables(seq):
    pos = np.arange(seq, dtype=np.float64)[:, None]

    def angles(d):
        inv_freq = ROPE_THETA ** (-np.arange(0, d, 2, dtype=np.float64) / d)
        return pos * inv_freq[None, :]

    ang = angles(MLA_ROPE_DIM)
    zeros = np.zeros((seq, V7X_LANES - MLA_ROPE_DIM))
    half0 = np.zeros_like(ang)
    cos_m = np.concatenate([np.cos(ang), np.cos(ang), zeros], axis=1)
    sin_up = np.concatenate([half0, np.sin(ang), zeros], axis=1)
    sin_dn = np.concatenate([-np.sin(ang), half0, zeros], axis=1)
    ang = angles(DIFF_HEAD_DIM)
    cos_d = np.concatenate([np.cos(ang), np.cos(ang)], axis=1)
    sin_d = np.concatenate([-np.sin(ang), np.sin(ang)], axis=1)
    return tuple(jnp.asarray(t, F32) for t in (cos_m, sin_up, sin_dn, cos_d, sin_d))


def kernel(x, ffn1_pre_g, ffn1_w_gate, ffn1_w_up, ffn1_w_down, ffn1_post_g, mix_pre_g, w_in, mla_q_norm_g, mla_w_uq, mla_kv_norm_g, mla_w_ukv, diff_lambda_q1, diff_lambda_k1, diff_lambda_q2, diff_lambda_k2, diff_subln_g, w_out, mix_post_g, ffn2_pre_g, ffn2_w_gate, ffn2_w_up, ffn2_w_down, ffn2_post_g):
    batch, seq, d = x.shape
    depth = ffn1_pre_g.shape[0]
    assert depth == 1 and d == D_MODEL
    assert seq % ATT_TQ == 0 and seq % PROJ_TM == 0 and (batch * seq) % FFN_TM == 0
    tables = _rope_tables(seq)
    xt = x.reshape(batch * seq, d)
    for l in range(depth):
        xt, w_in_bf = _ffn(xt, ffn1_pre_g[l][None], ffn1_w_gate[l], ffn1_w_up[l], ffn1_w_down[l],
                           ffn1_post_g[l][None], cast_src=w_in[l].T)
        q, k, v, dq, dk, dv = _mix_proj(
            xt, mix_pre_g[l][None], w_in_bf, mla_q_norm_g[l][None],
            mla_w_uq[l], mla_kv_norm_g[l][None], mla_w_ukv[l],
            tables, seq)
        o_mla = _mla_attn(q, k, v, batch, seq)
        o_diff = _diff_attn(dq, dk, dv, diff_lambda_q1[l][None], diff_lambda_k1[l][None],
                            diff_lambda_q2[l][None], diff_lambda_k2[l][None],
                            diff_subln_g[l][None], batch, seq)
        xt = _out_proj(xt, o_mla, o_diff, w_out[l], mix_post_g[l][None])
        xt = _ffn(xt, ffn2_pre_g[l][None], ffn2_w_gate[l], ffn2_w_up[l], ffn2_w_down[l],
                  ffn2_post_g[l][None])
    return xt.reshape(batch, seq, d)
```

```python
import functools
import math

import jax
import jax.numpy as jnp
import numpy as np
from jax import lax
from jax.experimental import pallas as pl
from jax.experimental.pallas import tpu as pltpu

D_MODEL = 2048
CHUNK = 64
ROPE_THETA = 10000.0
NORM_EPS = 1e-6
MLA_HEADS = 8
MLA_Q_RANK = 512
MLA_KV_RANK = 512
MLA_NOPE_DIM = 128
MLA_ROPE_DIM = 64
MLA_V_DIM = 128
MLA_QK_DIM = MLA_NOPE_DIM + MLA_ROPE_DIM
DIFF_HEADS = 4
DIFF_HEAD_DIM = 128
DIFF_V_DIM = 2 * DIFF_HEAD_DIM
MLA_OUT = MLA_HEADS * MLA_V_DIM
DIFF_OUT = DIFF_HEADS * DIFF_V_DIM
DIFF_QK_W = DIFF_HEADS * 2 * DIFF_HEAD_DIM
D_FF = 5632
LAMBDA_INIT = 0.8 - 0.6 * math.exp(-0.3 * 0)
LOG2_E = math.log2(math.e)
MLA_SCORE_SCALE = MLA_QK_DIM ** -0.5 * LOG2_E
DIFF_SCORE_SCALE = DIFF_HEAD_DIM ** -0.5 * LOG2_E

V7X_LANES = 128
V7X_VMEM_BYTES = 64 * 1024 * 1024
V7X_DEFAULT_SCOPED_VMEM = 32 * 1024 * 1024
V7X_VMEM_RESERVE = 4 * 1024 * 1024
MLA_HEAD_PAD = 2 * V7X_LANES

FFN_TM = 1024
FFN_TF = 512
FFN_ROWS = 256
FFN_CAST_ROWS = 64
PROJ_TM = 512
ATT_TQ = 512
ATT_TK = 512
ATT_KGROUP = 2
MLA_GROUP = 8
DIFF_GROUP = 4
OUT_TM = 512

BF16 = jnp.bfloat16
F32 = jnp.float32


def _vmem_limit(nbytes):
    return int(min(V7X_VMEM_BYTES - V7X_VMEM_RESERVE, max(V7X_DEFAULT_SCOPED_VMEM, nbytes * 3 // 2)))


def _dot_bf16(a, w):
    return lax.dot_general(a, w, (((1,), (0,)), ((), ())), preferred_element_type=F32)


def _dot_nt(a, w_t):
    return lax.dot_general(a, w_t, (((1,), (1,)), ((), ())), preferred_element_type=F32)


def _rms(v, g):
    ms = jnp.mean(v * v, axis=-1, keepdims=True)
    return v * lax.rsqrt(ms + NORM_EPS) * g


def _swiglu_hidden(h, wg, wu):
    g = _dot_bf16(h, wg)
    u = _dot_bf16(h, wu)
    return (g * (1.0 / (1.0 + jnp.exp(-g))) * u).astype(BF16)


def _ffn_kernel(*refs, n_cast_slabs):
    if n_cast_slabs:
        (x_ref, pre_g_ref, wg_ref, wu_ref, wd_ref, post_g_ref, src_ref,
         o_hbm, dst_ref, acc_ref, h_ref, sem) = refs
    else:
        x_ref, pre_g_ref, wg_ref, wu_ref, wd_ref, post_g_ref, o_hbm, acc_ref, h_ref, sem = refs
    i = pl.program_id(0)
    f = pl.program_id(1)
    last = pl.num_programs(1) - 1
    if n_cast_slabs:
        @pl.when(i * pl.num_programs(1) + f < n_cast_slabs)
        def _():
            dst_ref[...] = src_ref[...].astype(BF16)
    n_chunks = FFN_TM // FFN_ROWS
    chunks = [pl.ds(c * FFN_ROWS, FFN_ROWS) for c in range(n_chunks)]

    def out_copy(tile, c):
        dst = o_hbm.at[pl.ds(pl.multiple_of(tile * FFN_TM + c * FFN_ROWS, FFN_ROWS), FFN_ROWS)]
        return pltpu.make_async_copy(acc_ref.at[chunks[c]], dst, sem.at[c])

    @pl.when(jnp.logical_and(f == 0, i > 0))
    def _():
        for c in range(n_chunks):
            out_copy(i - 1, c).wait()

    @pl.when(f == 0)
    def _():
        for rows in chunks:
            h = _rms(x_ref[rows, :], pre_g_ref[...]).astype(BF16)
            h_ref[rows, :] = h
            acc_ref[rows, :] = _dot_bf16(_swiglu_hidden(h, wg_ref[...], wu_ref[...]), wd_ref[...])

    @pl.when(jnp.logical_and(f > 0, f < last))
    def _():
        a = _swiglu_hidden(h_ref[...], wg_ref[...], wu_ref[...])
        acc_ref[...] += _dot_bf16(a, wd_ref[...])

    @pl.when(f == last)
    def _():
        for c, rows in enumerate(chunks):
            a = _swiglu_hidden(h_ref[rows, :], wg_ref[...], wu_ref[...])
            acc = acc_ref[rows, :] + _dot_bf16(a, wd_ref[...])
            acc_ref[rows, :] = x_ref[rows, :] + 0.5 * _rms(acc, post_g_ref[...])
            out_copy(i, c).start()

        @pl.when(i == pl.num_programs(0) - 1)
        def _():
            for c in range(n_chunks):
                out_copy(i, c).wait()


def _ffn(x, pre_g, wg, wu, wd, post_g, cast_src=None):
    t, d = x.shape
    n_f = D_FF // FFN_TF
    assert n_f >= 2, "first and last hidden steps must be distinct"
    nbytes = (3 * FFN_TM * d * 4 + FFN_TM * d * 2 + 2 * 3 * d * FFN_TF * wg.dtype.itemsize
              + 3 * FFN_TM * FFN_TF * 4)
    in_specs = [
        pl.BlockSpec((FFN_TM, d), lambda i, f: (i, 0)),
        pl.BlockSpec((1, d), lambda i, f: (0, 0)),
        pl.BlockSpec((d, FFN_TF), lambda i, f: (0, f)),
        pl.BlockSpec((d, FFN_TF), lambda i, f: (0, f)),
        pl.BlockSpec((FFN_TF, d), lambda i, f: (f, 0)),
        pl.BlockSpec((1, d), lambda i, f: (0, 0)),
    ]
    out_specs = [pl.BlockSpec(memory_space=pl.ANY)]
    out_shape = [jax.ShapeDtypeStruct((t, d), F32)]
    operands = [x, pre_g, wg, wu, wd, post_g]
    n_slabs = 0
    if cast_src is not None:
        rows, cols = cast_src.shape
        n_slabs = rows // FFN_CAST_ROWS
        assert rows % FFN_CAST_ROWS == 0 and n_slabs <= (t // FFN_TM) * n_f
        slab = pl.BlockSpec((FFN_CAST_ROWS, cols),
                            lambda i, f: (jnp.minimum(i * n_f + f, n_slabs - 1), 0))
        in_specs.append(slab)
        out_specs.append(slab)
        out_shape.append(jax.ShapeDtypeStruct((rows, cols), BF16))
        operands.append(cast_src)
        nbytes += 2 * FFN_CAST_ROWS * cols * (4 + 2)
    outs = pl.pallas_call(
        functools.partial(_ffn_kernel, n_cast_slabs=n_slabs),
        grid=(t // FFN_TM, n_f),
        in_specs=in_specs,
        out_specs=out_specs,
        out_shape=out_shape,
        scratch_shapes=[pltpu.VMEM((FFN_TM, d), F32), pltpu.VMEM((FFN_TM, d), BF16),
                        pltpu.SemaphoreType.DMA((FFN_TM // FFN_ROWS,))],
        compiler_params=pltpu.CompilerParams(
            dimension_semantics=("arbitrary", "arbitrary"),
            vmem_limit_bytes=_vmem_limit(nbytes)),
        name="ffn",
    )(*operands)
    return outs[0] if cast_src is None else tuple(outs)


_IN_LATENT = MLA_Q_RANK + MLA_KV_RANK
_IN_W = _IN_LATENT + MLA_ROPE_DIM + 2 * DIFF_QK_W + DIFF_OUT


def _rope_half(v, cos, sin_signed):
    return v * cos + pltpu.roll(v, DIFF_HEAD_DIM // 2, 1) * sin_signed


def _rope_mla(v, cos, sin_up, sin_dn):
    half = MLA_ROPE_DIM // 2
    return (v * cos + pltpu.roll(v, half, 1) * sin_up
            + pltpu.roll(v, V7X_LANES - half, 1) * sin_dn)


def _proj_kernel(x_ref, g_ref, win_ref, qg_ref, wuq_ref, kvg_ref, wukv_ref,
                 cm_ref, sup_ref, sdn_ref, cd_ref, sd_ref,
                 q_ref, k_ref, v_ref, dq_ref, dk_ref, dv_ref):
    h = _rms(x_ref[...], g_ref[...]).astype(BF16)

    pa = _dot_nt(h, win_ref[0:_IN_LATENT, :])
    cq = _rms(pa[:, 0:MLA_Q_RANK], qg_ref[...]).astype(BF16)
    ckv = _rms(pa[:, MLA_Q_RANK:_IN_LATENT], kvg_ref[...]).astype(BF16)
    rest = _dot_nt(h, win_ref[_IN_LATENT:_IN_W, :])
    cm, sup, sdn = cm_ref[...], sup_ref[...], sdn_ref[...]
    kpe = _rope_mla(rest[:, 0:V7X_LANES], cm, sup, sdn).astype(BF16)

    q = _dot_bf16(cq, wuq_ref[...]) * MLA_SCORE_SCALE
    kv = _dot_bf16(ckv, wukv_ref[...])
    for hh in range(MLA_HEADS):
        a = hh * MLA_HEAD_PAD
        b = a + V7X_LANES
        c = b + V7X_LANES
        qa = hh * MLA_QK_DIM
        q_ref[:, a:b] = q[:, qa:qa + MLA_NOPE_DIM].astype(BF16)
        if (qa + MLA_NOPE_DIM) % V7X_LANES == 0:
            pe = q[:, qa + MLA_NOPE_DIM:qa + MLA_NOPE_DIM + V7X_LANES]
        else:
            pe = pltpu.roll(q[:, qa + MLA_ROPE_DIM:qa + MLA_QK_DIM], MLA_ROPE_DIM, 1)
        q_ref[:, b:c] = _rope_mla(pe, cm, sup, sdn).astype(BF16)
        k_ref[:, a:b] = kv[:, a:b].astype(BF16)
        k_ref[:, b:c] = kpe
        v_ref[:, hh * MLA_V_DIM:(hh + 1) * MLA_V_DIM] = kv[:, b:c].astype(BF16)

    cd, sd = cd_ref[...], sd_ref[...]
    off = MLA_ROPE_DIM
    pdq = rest[:, off:off + DIFF_QK_W] * DIFF_SCORE_SCALE
    for c in range(DIFF_QK_W // DIFF_HEAD_DIM):
        cols = slice(c * DIFF_HEAD_DIM, (c + 1) * DIFF_HEAD_DIM)
        dq_ref[:, cols] = _rope_half(pdq[:, cols], cd, sd).astype(BF16)
    pdk = rest[:, off + DIFF_QK_W:off + 2 * DIFF_QK_W]
    for c in range(DIFF_QK_W // DIFF_HEAD_DIM):
        cols = slice(c * DIFF_HEAD_DIM, (c + 1) * DIFF_HEAD_DIM)
        dk_ref[:, cols] = _rope_half(pdk[:, cols], cd, sd).astype(BF16)
    dv_ref[...] = rest[:, off + 2 * DIFF_QK_W:].astype(BF16)


def _mix_proj(x, g, w_in_t, qg, wuq, kvg, wukv, tables, seq):
    t, d = x.shape
    tm = PROJ_TM
    pos_blocks = seq // tm
    const = lambda i: (0, 0)
    row = lambda i: (i, 0)
    pos = lambda i: (i % pos_blocks, 0)
    wq = MLA_HEADS * MLA_HEAD_PAD
    out_w = (wq, wq, MLA_OUT, DIFF_QK_W, DIFF_QK_W, DIFF_OUT)
    once = dict(pipeline_mode=pl.Buffered(1))
    nbytes = (d * _IN_W * 2 + MLA_Q_RANK * wq * (4 + 4) + 2 * tm * d * 4
              + 2 * tm * sum(out_w) * 2 + tm * _IN_W * 4 + 2 * tm * wq * 4)
    return pl.pallas_call(
        _proj_kernel,
        grid=(t // tm,),
        in_specs=[
            pl.BlockSpec((tm, d), row),
            pl.BlockSpec((1, d), const),
            pl.BlockSpec((_IN_W, d), const, **once),
            pl.BlockSpec((1, MLA_Q_RANK), const),
            pl.BlockSpec((MLA_Q_RANK, MLA_HEADS * MLA_QK_DIM), const, **once),
            pl.BlockSpec((1, MLA_KV_RANK), const),
            pl.BlockSpec((MLA_KV_RANK, wq), const, **once),
        ] + [pl.BlockSpec((tm, V7X_LANES), pos)] * 5,
        out_specs=[pl.BlockSpec((tm, w), row) for w in out_w],
        out_shape=[jax.ShapeDtypeStruct((t, w), BF16) for w in out_w],
        compiler_params=pltpu.CompilerParams(
            dimension_semantics=("parallel",),
            vmem_limit_bytes=_vmem_limit(nbytes)),
        name="mix_proj",
    )(x, g, w_in_t, qg, wuq, kvg, wukv, *tables)


def _causal_mask(n_q, n_k):
    r = lax.broadcasted_iota(jnp.int32, (n_q, n_k), 0) // CHUNK
    c = lax.broadcasted_iota(jnp.int32, (n_q, n_k), 1) // CHUNK
    return r >= c


def _flash_scratch(n_maps, acc_width, with_sum):
    stat = pltpu.VMEM((ATT_TQ, V7X_LANES), F32)
    per_map = [stat] * (2 if with_sum else 1) + [pltpu.VMEM((ATT_TQ, acc_width), F32)]
    return per_map * n_maps


def _flash_maps(state, n_maps, with_sum):
    k = len(state) // n_maps
    return [(state[k * n], state[k * n + 1] if with_sum else None, state[k * n + k - 1])
            for n in range(n_maps)]


def _flash_step(t, v, m_ref, l_ref, acc_ref):
    lanes = m_ref.shape[1]
    m_old = m_ref[...]
    m_new = jnp.maximum(m_old, jnp.max(t, axis=-1, keepdims=True))
    alpha = jnp.exp2(m_old - m_new)
    p = jnp.exp2(t - jnp.tile(m_new, (1, t.shape[1] // lanes)))
    if l_ref is not None:
        l_ref[...] = alpha * l_ref[...] + jnp.sum(p, axis=-1, keepdims=True)
    pv = jnp.dot(p.astype(BF16), v, preferred_element_type=F32)
    acc_ref[...] = jnp.tile(alpha, (1, acc_ref.shape[1] // lanes)) * acc_ref[...] + pv
    m_ref[...] = m_new


def _attn_tile(qi, maps, scores, values):
    half = ATT_TQ // 2
    assert ATT_TQ == ATT_TK and half % CHUNK == 0

    def update(q_rows, k_rows, mask):
        for n, refs in enumerate(maps):
            t = scores(n, q_rows, k_rows)
            if mask is not None:
                t = jnp.where(mask, t, -jnp.inf)
            _flash_step(t, values(n, k_rows), *[r if r is None else r.at[q_rows] for r in refs])

    for m_ref, l_ref, acc_ref in maps:
        m_ref[...] = jnp.full(m_ref.shape, -jnp.inf, F32)
        if l_ref is not None:
            l_ref[...] = jnp.zeros(l_ref.shape, F32)
        acc_ref[...] = jnp.zeros(acc_ref.shape, F32)
    base = pl.multiple_of(qi * ATT_TQ, ATT_TQ)
    update(pl.ds(0, ATT_TQ), pl.ds(base, half), _causal_mask(ATT_TQ, half))
    update(pl.ds(half, half), pl.ds(pl.multiple_of(base + half, half), half),
           _causal_mask(half, half))

    grp = ATT_KGROUP

    def body(j, carry):
        update(pl.ds(0, ATT_TQ),
               pl.ds(pl.multiple_of(j * grp * ATT_TK, grp * ATT_TK), grp * ATT_TK), None)
        return carry

    lax.fori_loop(0, qi // grp, body, 0)
    for rem in range(1, grp):
        @pl.when(qi % grp == rem)
        def _():
            start = pl.multiple_of((qi - rem) * ATT_TK, ATT_TK)
            update(pl.ds(0, ATT_TQ), pl.ds(start, rem * ATT_TK), None)


def _qk(q, k):
    return lax.dot_general(q, k, (((1,), (1,)), ((), ())), preferred_element_type=F32)


def _mla_attn_kernel(q_ref, k_ref, v_ref, o_ref, *state):
    wq, wv = MLA_HEAD_PAD, MLA_V_DIM

    def scores(g, q_rows, k_rows):
        return _qk(q_ref[q_rows, g * wq:(g + 1) * wq], k_ref[k_rows, g * wq:(g + 1) * wq])

    def values(g, k_rows):
        v = v_ref[k_rows, g * wv:(g + 1) * wv]
        return jnp.concatenate([v, jnp.ones(v.shape, v.dtype)], axis=1)

    maps = _flash_maps(state, MLA_GROUP, with_sum=False)
    _attn_tile(pl.program_id(2), maps, scores, values)
    for g, (_, _, acc_ref) in enumerate(maps):
        o_ref[:, g * wv:(g + 1) * wv] = (acc_ref[:, 0:wv] / acc_ref[:, wv:2 * wv]).astype(o_ref.dtype)


def _mla_attn(q, k, v, batch, seq):
    nq = seq // ATT_TQ
    wq, wv = MLA_GROUP * MLA_HEAD_PAD, MLA_GROUP * MLA_V_DIM
    return pl.pallas_call(
        _mla_attn_kernel,
        grid=(batch, MLA_HEADS // MLA_GROUP, nq),
        in_specs=[
            pl.BlockSpec((ATT_TQ, wq), lambda b, h, i: (b * nq + i, h)),
            pl.BlockSpec((seq, wq), lambda b, h, i: (b, h)),
            pl.BlockSpec((seq, wv), lambda b, h, i: (b, h)),
        ],
        out_specs=pl.BlockSpec((ATT_TQ, wv), lambda b, h, i: (b * nq + i, h)),
        out_shape=jax.ShapeDtypeStruct((batch * seq, MLA_OUT), BF16),
        scratch_shapes=_flash_scratch(MLA_GROUP, 2 * MLA_V_DIM, with_sum=False),
        compiler_params=pltpu.CompilerParams(
            dimension_semantics=("parallel", "parallel", "parallel"),
            vmem_limit_bytes=_vmem_limit(2 * (seq + ATT_TQ) * (wq + wv) * 2
                                         + 8 * MLA_GROUP * ATT_TQ * ATT_TK * 4)),
        name="mla_attn",
    )(q, k, v)


def _diff_attn_kernel(q_ref, k_ref, v_ref, lq1_ref, lk1_ref, lq2_ref, lk2_ref, g_ref, o_ref,
                      *state):
    d, wv = DIFF_HEAD_DIM, DIFF_V_DIM

    def scores(n, q_rows, k_rows):
        return _qk(q_ref[q_rows, n * d:(n + 1) * d], k_ref[k_rows, n * d:(n + 1) * d])

    def values(n, k_rows):
        return v_ref[k_rows, (n // 2) * wv:(n // 2 + 1) * wv]

    maps = _flash_maps(state, 2 * DIFF_GROUP, with_sum=True)
    _attn_tile(pl.program_id(2), maps, scores, values)
    lam = (jnp.exp(jnp.sum(lq1_ref[...] * lk1_ref[...], axis=-1, keepdims=True))
           - jnp.exp(jnp.sum(lq2_ref[...] * lk2_ref[...], axis=-1, keepdims=True))
           + LAMBDA_INIT)
    rep = wv // V7X_LANES
    for h in range(DIFF_GROUP):
        (_, l1_ref, a1_ref), (_, l2_ref, a2_ref) = maps[2 * h:2 * h + 2]
        o = (a1_ref[...] / jnp.tile(l1_ref[...], (1, rep))
             - lam * (a2_ref[...] / jnp.tile(l2_ref[...], (1, rep))))
        o_ref[:, h * wv:(h + 1) * wv] = (
            _rms(o, g_ref[...]) * (1.0 - LAMBDA_INIT)).astype(o_ref.dtype)


def _diff_attn(q, k, v, lq1, lk1, lq2, lk2, g, batch, seq):
    nq = seq // ATT_TQ
    w = DIFF_GROUP * DIFF_V_DIM
    vec = pl.BlockSpec((1, DIFF_HEAD_DIM), lambda b, h, i: (0, 0))
    return pl.pallas_call(
        _diff_attn_kernel,
        grid=(batch, DIFF_HEADS // DIFF_GROUP, nq),
        in_specs=[
            pl.BlockSpec((ATT_TQ, w), lambda b, h, i: (b * nq + i, h)),
            pl.BlockSpec((seq, w), lambda b, h, i: (b, h)),
            pl.BlockSpec((seq, w), lambda b, h, i: (b, h)),
            vec, vec, vec, vec,
            pl.BlockSpec((1, DIFF_V_DIM), lambda b, h, i: (0, 0)),
        ],
        out_specs=pl.BlockSpec((ATT_TQ, w), lambda b, h, i: (b * nq + i, h)),
        out_shape=jax.ShapeDtypeStruct((batch * seq, DIFF_OUT), BF16),
        scratch_shapes=_flash_scratch(2 * DIFF_GROUP, DIFF_V_DIM, with_sum=True),
        compiler_params=pltpu.CompilerParams(
            dimension_semantics=("parallel", "parallel", "parallel"),
            vmem_limit_bytes=_vmem_limit(2 * (2 * seq + 2 * ATT_TQ) * w * 2
                                         + 16 * DIFF_GROUP * ATT_TQ * ATT_TK * 4)),
        name="diff_attn",
    )(q, k, v, lq1, lk1, lq2, lk2, g)


def _out_kernel(x_ref, oa_ref, ob_ref, wa_ref, wb_ref, g_ref, o_ref):
    o = _dot_bf16(oa_ref[...], wa_ref[...]) + _dot_bf16(ob_ref[...], wb_ref[...])
    o_ref[...] = x_ref[...] + _rms(o, g_ref[...])


def _out_proj(x, oa, ob, w, g):
    t, d = x.shape
    assert MLA_OUT == DIFF_OUT, "the two row blocks of w share one block shape"
    tm = OUT_TM
    const = lambda i: (0, 0)
    row = lambda i: (i, 0)
    once = dict(pipeline_mode=pl.Buffered(1))
    nbytes = (d * d * (w.dtype.itemsize + 2) + 4 * tm * d * 4 + 2 * tm * d * 2 + 2 * tm * d * 4)
    return pl.pallas_call(
        _out_kernel,
        grid=(t // tm,),
        in_specs=[
            pl.BlockSpec((tm, d), row),
            pl.BlockSpec((tm, MLA_OUT), row),
            pl.BlockSpec((tm, DIFF_OUT), row),
            pl.BlockSpec((MLA_OUT, d), const, **once),
            pl.BlockSpec((DIFF_OUT, d), lambda i: (1, 0), **once),
            pl.BlockSpec((1, d), const),
        ],
        out_specs=pl.BlockSpec((tm, d), row),
        out_shape=jax.ShapeDtypeStruct((t, d), F32),
        compiler_params=pltpu.CompilerParams(
            dimension_semantics=("parallel",),
            vmem_limit_bytes=_vmem_limit(nbytes)),
        name="out_proj",
    )(x, oa, ob, w, w, g)


def _rope_tables(seq):
    pos = np.arange(seq, dtype=np.float64)[:, None]

    def angles(d):
        inv_freq = ROPE_THETA ** (-np.arange(0, d, 2, dtype=np.float64) / d)
        return pos * inv_freq[None, :]

    ang = angles(MLA_ROPE_DIM)
    zeros = np.zeros((seq, V7X_LANES - MLA_ROPE_DIM))
    half0 = np.zeros_like(ang)
    cos_m = np.concatenate([np.cos(ang), np.cos(ang), zeros], axis=1)
    sin_up = np.concatenate([half0, np.sin(ang), zeros], axis=1)
    sin_dn = np.concatenate([-np.sin(ang), half0, zeros], axis=1)
    ang = angles(DIFF_HEAD_DIM)
    cos_d = np.concatenate([np.cos(ang), np.cos(ang)], axis=1)
    sin_d = np.concatenate([-np.sin(ang), np.sin(ang)], axis=1)
    return tuple(jnp.asarray(t, F32) for t in (cos_m, sin_up, sin_dn, cos_d, sin_d))


def kernel(x, ffn1_pre_g, ffn1_w_gate, ffn1_w_up, ffn1_w_down, ffn1_post_g, mix_pre_g, w_in, mla_q_norm_g, mla_w_uq, mla_kv_norm_g, mla_w_ukv, diff_lambda_q1, diff_lambda_k1, diff_lambda_q2, diff_lambda_k2, diff_subln_g, w_out, mix_post_g, ffn2_pre_g, ffn2_w_gate, ffn2_w_up, ffn2_w_down, ffn2_post_g):
    batch, seq, d = x.shape
    depth = ffn1_pre_g.shape[0]
    assert depth == 1 and d == D_MODEL
    assert seq % ATT_TQ == 0 and seq % PROJ_TM == 0 and (batch * seq) % FFN_TM == 0
    tables = _rope_tables(seq)
    xt = x.reshape(batch * seq, d)
    for l in range(depth):
        xt, w_in_bf = _ffn(xt, ffn1_pre_g[l][None], ffn1_w_gate[l], ffn1_w_up[l], ffn1_w_down[l],
                           ffn1_post_g[l][None], cast_src=w_in[l].T)
        q, k, v, dq, dk, dv = _mix_proj(
            xt, mix_pre_g[l][None], w_in_bf, mla_q_norm_g[l][None],
            mla_w_uq[l], mla_kv_norm_g[l][None], mla_w_ukv[l],
            tables, seq)
        o_mla = _mla_attn(q, k, v, batch, seq)
        o_diff = _diff_attn(dq, dk, dv, diff_lambda_q1[l][None], diff_lambda_k1[l][None],
                            diff_lambda_q2[l][None], diff_lambda_k2[l][None],
                            diff_subln_g[l][None], batch, seq)
        xt = _out_proj(xt, o_mla, o_diff, w_out[l], mix_post_g[l][None])
        xt = _ffn(xt, ffn2_pre_g[l][None], ffn2_w_gate[l], ffn2_w_up[l], ffn2_w_down[l],
                  ffn2_post_g[l][None])
    return xt.reshape(batch, seq, d)
```

```python
import functools
import math

import jax
import jax.numpy as jnp
import numpy as np
from jax import lax
from jax.experimental import pallas as pl
from jax.experimental.pallas import tpu as pltpu

D_MODEL = 2048
CHUNK = 64
ROPE_THETA = 10000.0
NORM_EPS = 1e-6
MLA_HEADS = 8
MLA_Q_RANK = 512
MLA_KV_RANK = 512
MLA_NOPE_DIM = 128
MLA_ROPE_DIM = 64
MLA_V_DIM = 128
MLA_QK_DIM = MLA_NOPE_DIM + MLA_ROPE_DIM
DIFF_HEADS = 4
DIFF_HEAD_DIM = 128
DIFF_V_DIM = 2 * DIFF_HEAD_DIM
MLA_OUT = MLA_HEADS * MLA_V_DIM
DIFF_OUT = DIFF_HEADS * DIFF_V_DIM
DIFF_QK_W = DIFF_HEADS * 2 * DIFF_HEAD_DIM
D_FF = 5632
LAMBDA_INIT = 0.8 - 0.6 * math.exp(-0.3 * 0)
LOG2_E = math.log2(math.e)
MLA_SCORE_SCALE = MLA_QK_DIM ** -0.5 * LOG2_E
DIFF_SCORE_SCALE = DIFF_HEAD_DIM ** -0.5 * LOG2_E

V7X_LANES = 128
V7X_VMEM_BYTES = 64 * 1024 * 1024
V7X_DEFAULT_SCOPED_VMEM = 32 * 1024 * 1024
V7X_VMEM_RESERVE = 4 * 1024 * 1024
MLA_HEAD_PAD = 2 * V7X_LANES

FFN_TM = 1024
FFN_TF = 512
FFN_ROWS = 256
FFN_CAST_ROWS = 64
PROJ_TM = 512
ATT_TQ = 512
ATT_TK = 512
ATT_KGROUP = 2
ATT_MLA_HEADS = 4
ATT_DIFF_HEADS = 2
OUT_TM = 512

BF16 = jnp.bfloat16
F32 = jnp.float32


def _vmem_limit(nbytes):
    return int(min(V7X_VMEM_BYTES - V7X_VMEM_RESERVE, max(V7X_DEFAULT_SCOPED_VMEM, nbytes * 3 // 2)))


def _dot_bf16(a, w):
    return lax.dot_general(a, w, (((1,), (0,)), ((), ())), preferred_element_type=F32)


def _dot_nt(a, w_t):
    return lax.dot_general(a, w_t, (((1,), (1,)), ((), ())), preferred_element_type=F32)


def _rms(v, g):
    ms = jnp.mean(v * v, axis=-1, keepdims=True)
    return v * lax.rsqrt(ms + NORM_EPS) * g


def _swiglu_hidden(h, wg, wu):
    g = _dot_bf16(h, wg)
    u = _dot_bf16(h, wu)
    return (g * (1.0 / (1.0 + jnp.exp(-g))) * u).astype(BF16)


def _ffn_kernel(*refs, n_cast_slabs):
    if n_cast_slabs:
        (x_ref, pre_g_ref, wg_ref, wu_ref, wd_ref, post_g_ref, src_ref,
         o_hbm, dst_ref, acc_ref, h_ref, sem) = refs
    else:
        x_ref, pre_g_ref, wg_ref, wu_ref, wd_ref, post_g_ref, o_hbm, acc_ref, h_ref, sem = refs
    i = pl.program_id(0)
    f = pl.program_id(1)
    last = pl.num_programs(1) - 1
    if n_cast_slabs:
        @pl.when(i * pl.num_programs(1) + f < n_cast_slabs)
        def _():
            dst_ref[...] = src_ref[...].astype(BF16)
    n_chunks = FFN_TM // FFN_ROWS
    chunks = [pl.ds(c * FFN_ROWS, FFN_ROWS) for c in range(n_chunks)]

    def out_copy(tile, c):
        dst = o_hbm.at[pl.ds(pl.multiple_of(tile * FFN_TM + c * FFN_ROWS, FFN_ROWS), FFN_ROWS)]
        return pltpu.make_async_copy(acc_ref.at[chunks[c]], dst, sem.at[c])

    @pl.when(jnp.logical_and(f == 0, i > 0))
    def _():
        for c in range(n_chunks):
            out_copy(i - 1, c).wait()

    @pl.when(f == 0)
    def _():
        for rows in chunks:
            h = _rms(x_ref[rows, :], pre_g_ref[...]).astype(BF16)
            h_ref[rows, :] = h
            acc_ref[rows, :] = _dot_bf16(_swiglu_hidden(h, wg_ref[...], wu_ref[...]), wd_ref[...])

    @pl.when(jnp.logical_and(f > 0, f < last))
    def _():
        a = _swiglu_hidden(h_ref[...], wg_ref[...], wu_ref[...])
        acc_ref[...] += _dot_bf16(a, wd_ref[...])

    @pl.when(f == last)
    def _():
        for c, rows in enumerate(chunks):
            a = _swiglu_hidden(h_ref[rows, :], wg_ref[...], wu_ref[...])
            acc = acc_ref[rows, :] + _dot_bf16(a, wd_ref[...])
            acc_ref[rows, :] = x_ref[rows, :] + 0.5 * _rms(acc, post_g_ref[...])
            out_copy(i, c).start()

        @pl.when(i == pl.num_programs(0) - 1)
        def _():
            for c in range(n_chunks):
                out_copy(i, c).wait()


def _ffn(x, pre_g, wg, wu, wd, post_g, cast_src=None):
    t, d = x.shape
    n_f = D_FF // FFN_TF
    assert n_f >= 2, "first and last hidden steps must be distinct"
    nbytes = (3 * FFN_TM * d * 4 + FFN_TM * d * 2 + 2 * 3 * d * FFN_TF * wg.dtype.itemsize
              + 3 * FFN_TM * FFN_TF * 4)
    in_specs = [
        pl.BlockSpec((FFN_TM, d), lambda i, f: (i, 0)),
        pl.BlockSpec((1, d), lambda i, f: (0, 0)),
        pl.BlockSpec((d, FFN_TF), lambda i, f: (0, f)),
        pl.BlockSpec((d, FFN_TF), lambda i, f: (0, f)),
        pl.BlockSpec((FFN_TF, d), lambda i, f: (f, 0)),
        pl.BlockSpec((1, d), lambda i, f: (0, 0)),
    ]
    out_specs = [pl.BlockSpec(memory_space=pl.ANY)]
    out_shape = [jax.ShapeDtypeStruct((t, d), F32)]
    operands = [x, pre_g, wg, wu, wd, post_g]
    n_slabs = 0
    if cast_src is not None:
        rows, cols = cast_src.shape
        n_slabs = rows // FFN_CAST_ROWS
        assert rows % FFN_CAST_ROWS == 0 and n_slabs <= (t // FFN_TM) * n_f
        slab = pl.BlockSpec((FFN_CAST_ROWS, cols),
                            lambda i, f: (jnp.minimum(i * n_f + f, n_slabs - 1), 0))
        in_specs.append(slab)
        out_specs.append(slab)
        out_shape.append(jax.ShapeDtypeStruct((rows, cols), BF16))
        operands.append(cast_src)
        nbytes += 2 * FFN_CAST_ROWS * cols * (4 + 2)
    outs = pl.pallas_call(
        functools.partial(_ffn_kernel, n_cast_slabs=n_slabs),
        grid=(t // FFN_TM, n_f),
        in_specs=in_specs,
        out_specs=out_specs,
        out_shape=out_shape,
        scratch_shapes=[pltpu.VMEM((FFN_TM, d), F32), pltpu.VMEM((FFN_TM, d), BF16),
                        pltpu.SemaphoreType.DMA((FFN_TM // FFN_ROWS,))],
        compiler_params=pltpu.CompilerParams(
            dimension_semantics=("arbitrary", "arbitrary"),
            vmem_limit_bytes=_vmem_limit(nbytes)),
        name="ffn",
    )(*operands)
    return outs[0] if cast_src is None else tuple(outs)


_IN_LATENT = MLA_Q_RANK + MLA_KV_RANK
_IN_W = _IN_LATENT + MLA_ROPE_DIM + 2 * DIFF_QK_W + DIFF_OUT


def _rope_half(v, cos, sin_signed):
    return v * cos + pltpu.roll(v, DIFF_HEAD_DIM // 2, 1) * sin_signed


def _rope_mla(v, cos, sin_up, sin_dn):
    half = MLA_ROPE_DIM // 2
    return (v * cos + pltpu.roll(v, half, 1) * sin_up
            + pltpu.roll(v, V7X_LANES - half, 1) * sin_dn)


def _proj_kernel(x_ref, g_ref, win_ref, qg_ref, wuq_ref, kvg_ref, wukv_ref,
                 cm_ref, sup_ref, sdn_ref, cd_ref, sd_ref,
                 q_ref, k_ref, v_ref, dq_ref, dk_ref, dv_ref):
    h = _rms(x_ref[...], g_ref[...]).astype(BF16)

    pa = _dot_nt(h, win_ref[0:_IN_LATENT, :])
    cq = _rms(pa[:, 0:MLA_Q_RANK], qg_ref[...]).astype(BF16)
    ckv = _rms(pa[:, MLA_Q_RANK:_IN_LATENT], kvg_ref[...]).astype(BF16)
    rest = _dot_nt(h, win_ref[_IN_LATENT:_IN_W, :])
    cm, sup, sdn = cm_ref[...], sup_ref[...], sdn_ref[...]
    kpe = _rope_mla(rest[:, 0:V7X_LANES], cm, sup, sdn).astype(BF16)

    q = _dot_bf16(cq, wuq_ref[...]) * MLA_SCORE_SCALE
    kv = _dot_bf16(ckv, wukv_ref[...])
    for hh in range(MLA_HEADS):
        a = hh * MLA_HEAD_PAD
        b = a + V7X_LANES
        c = b + V7X_LANES
        qa = hh * MLA_QK_DIM
        q_ref[:, a:b] = q[:, qa:qa + MLA_NOPE_DIM].astype(BF16)
        if (qa + MLA_NOPE_DIM) % V7X_LANES == 0:
            pe = q[:, qa + MLA_NOPE_DIM:qa + MLA_NOPE_DIM + V7X_LANES]
        else:
            pe = pltpu.roll(q[:, qa + MLA_ROPE_DIM:qa + MLA_QK_DIM], MLA_ROPE_DIM, 1)
        q_ref[:, b:c] = _rope_mla(pe, cm, sup, sdn).astype(BF16)
        k_ref[:, a:b] = kv[:, a:b].astype(BF16)
        k_ref[:, b:c] = kpe
        v_ref[:, a:b] = kv[:, b:c].astype(BF16)
        v_ref[:, b:c] = jnp.ones((x_ref.shape[0], V7X_LANES), BF16)

    cd, sd = cd_ref[...], sd_ref[...]
    off = MLA_ROPE_DIM
    pdq = rest[:, off:off + DIFF_QK_W] * DIFF_SCORE_SCALE
    for c in range(DIFF_QK_W // DIFF_HEAD_DIM):
        cols = slice(c * DIFF_HEAD_DIM, (c + 1) * DIFF_HEAD_DIM)
        dq_ref[:, cols] = _rope_half(pdq[:, cols], cd, sd).astype(BF16)
    pdk = rest[:, off + DIFF_QK_W:off + 2 * DIFF_QK_W]
    for c in range(DIFF_QK_W // DIFF_HEAD_DIM):
        cols = slice(c * DIFF_HEAD_DIM, (c + 1) * DIFF_HEAD_DIM)
        dk_ref[:, cols] = _rope_half(pdk[:, cols], cd, sd).astype(BF16)
    dv_ref[...] = rest[:, off + 2 * DIFF_QK_W:].astype(BF16)


def _mix_proj(x, g, w_in_t, qg, wuq, kvg, wukv, tables, seq):
    t, d = x.shape
    tm = PROJ_TM
    pos_blocks = seq // tm
    const = lambda i: (0, 0)
    row = lambda i: (i, 0)
    pos = lambda i: (i % pos_blocks, 0)
    wq = MLA_HEADS * MLA_HEAD_PAD
    assert MLA_HEAD_PAD == 2 * MLA_V_DIM
    out_w = (wq, wq, 2 * MLA_OUT, DIFF_QK_W, DIFF_QK_W, DIFF_OUT)
    once = dict(pipeline_mode=pl.Buffered(1))
    nbytes = (d * _IN_W * 2 + MLA_Q_RANK * wq * (4 + 4) + 2 * tm * d * 4
              + 2 * tm * sum(out_w) * 2 + tm * _IN_W * 4 + 2 * tm * wq * 4)
    return pl.pallas_call(
        _proj_kernel,
        grid=(t // tm,),
        in_specs=[
            pl.BlockSpec((tm, d), row),
            pl.BlockSpec((1, d), const),
            pl.BlockSpec((_IN_W, d), const, **once),
            pl.BlockSpec((1, MLA_Q_RANK), const),
            pl.BlockSpec((MLA_Q_RANK, MLA_HEADS * MLA_QK_DIM), const, **once),
            pl.BlockSpec((1, MLA_KV_RANK), const),
            pl.BlockSpec((MLA_KV_RANK, wq), const, **once),
        ] + [pl.BlockSpec((tm, V7X_LANES), pos)] * 5,
        out_specs=[pl.BlockSpec((tm, w), row) for w in out_w],
        out_shape=[jax.ShapeDtypeStruct((t, w), BF16) for w in out_w],
        compiler_params=pltpu.CompilerParams(
            dimension_semantics=("parallel",),
            vmem_limit_bytes=_vmem_limit(nbytes)),
        name="mix_proj",
    )(x, g, w_in_t, qg, wuq, kvg, wukv, *tables)


def _causal_mask(n_q, n_k):
    r = lax.broadcasted_iota(jnp.int32, (n_q, n_k), 0) // CHUNK
    c = lax.broadcasted_iota(jnp.int32, (n_q, n_k), 1) // CHUNK
    return r >= c


def _flash_scratch(n_maps, acc_width, with_sum):
    stat = pltpu.VMEM((ATT_TQ, V7X_LANES), F32)
    per_map = [stat] * (2 if with_sum else 1) + [pltpu.VMEM((ATT_TQ, acc_width), F32)]
    return per_map * n_maps


def _flash_maps(state, n_maps, with_sum):
    k = len(state) // n_maps
    return [(state[k * n], state[k * n + 1] if with_sum else None, state[k * n + k - 1])
            for n in range(n_maps)]


def _flash_weights(t, m_ref, l_ref):
    lanes = m_ref.shape[1]
    m_old = m_ref[...]
    m_new = jnp.maximum(m_old, jnp.max(t, axis=-1, keepdims=True))
    alpha = jnp.exp2(m_old - m_new)
    p = jnp.exp2(t - jnp.tile(m_new, (1, t.shape[1] // lanes)))
    if l_ref is not None:
        l_ref[...] = alpha * l_ref[...] + jnp.sum(p, axis=-1, keepdims=True)
    m_ref[...] = m_new
    return alpha, p.astype(BF16)


def _flash_acc(alpha, pv, acc_ref):
    lanes = alpha.shape[1]
    acc_ref[...] = jnp.tile(alpha, (1, acc_ref.shape[1] // lanes)) * acc_ref[...] + pv


def _attn_tile(qi, maps, scores, values, share=1):
    half = ATT_TQ // 2
    assert ATT_TQ == ATT_TK and half % CHUNK == 0 and len(maps) % share == 0

    def update(q_rows, k_rows, mask):
        n_q = q_rows.size
        for n0 in range(0, len(maps), share):
            views, alphas, weights = [], [], []
            for n in range(n0, n0 + share):
                t = scores(n, q_rows, k_rows)
                if mask is not None:
                    t = jnp.where(mask, t, -jnp.inf)
                refs = [r if r is None else r.at[q_rows] for r in maps[n]]
                alpha, p = _flash_weights(t, refs[0], refs[1])
                views.append(refs[2])
                alphas.append(alpha)
                weights.append(p)
            p_all = weights[0] if share == 1 else jnp.concatenate(weights, axis=0)
            pv = jnp.dot(p_all, values(n0, k_rows), preferred_element_type=F32)
            for j in range(share):
                _flash_acc(alphas[j], pv[j * n_q:(j + 1) * n_q, :], views[j])

    for m_ref, l_ref, acc_ref in maps:
        m_ref[...] = jnp.full(m_ref.shape, -jnp.inf, F32)
        if l_ref is not None:
            l_ref[...] = jnp.zeros(l_ref.shape, F32)
        acc_ref[...] = jnp.zeros(acc_ref.shape, F32)
    base = pl.multiple_of(qi * ATT_TQ, ATT_TQ)
    update(pl.ds(0, ATT_TQ), pl.ds(base, half), _causal_mask(ATT_TQ, half))
    update(pl.ds(half, half), pl.ds(pl.multiple_of(base + half, half), half),
           _causal_mask(half, half))

    grp = ATT_KGROUP

    def body(j, carry):
        update(pl.ds(0, ATT_TQ),
               pl.ds(pl.multiple_of(j * grp * ATT_TK, grp * ATT_TK), grp * ATT_TK), None)
        return carry

    lax.fori_loop(0, qi // grp, body, 0)
    for rem in range(1, grp):
        @pl.when(qi % grp == rem)
        def _():
            start = pl.multiple_of((qi - rem) * ATT_TK, ATT_TK)
            update(pl.ds(0, ATT_TQ), pl.ds(start, rem * ATT_TK), None)


def _qk(q, k):
    return lax.dot_general(q, k, (((1,), (1,)), ((), ())), preferred_element_type=F32)


def _attn_kernel(q_ref, k_ref, v_ref, dq_ref, dk_ref, dv_ref, lq1_ref, lk1_ref, lq2_ref, lk2_ref,
                 g_ref, oa_ref, ob_ref, *state):
    wq, wv = MLA_HEAD_PAD, MLA_V_DIM
    d, dwv = DIFF_HEAD_DIM, DIFF_V_DIM
    n_m, n_d = ATT_MLA_HEADS, 2 * ATT_DIFF_HEADS
    m_maps = _flash_maps(state[:2 * n_m], n_m, with_sum=False)
    d_maps = _flash_maps(state[2 * n_m:], n_d, with_sum=True)
    kinds, maps = [], []
    for j in range(max(n_m, n_d)):
        if j < n_m:
            kinds.append(("mla", j))
            maps.append(m_maps[j])
        if j < n_d:
            kinds.append(("diff", j))
            maps.append(d_maps[j])

    def scores(n, q_rows, k_rows):
        kind, j = kinds[n]
        if kind == "mla":
            return _qk(q_ref[q_rows, j * wq:(j + 1) * wq], k_ref[k_rows, j * wq:(j + 1) * wq])
        return _qk(dq_ref[q_rows, j * d:(j + 1) * d], dk_ref[k_rows, j * d:(j + 1) * d])

    def values(n, k_rows):
        kind, j = kinds[n]
        if kind == "mla":
            return v_ref[k_rows, j * 2 * wv:(j + 1) * 2 * wv]
        return dv_ref[k_rows, (j // 2) * dwv:(j // 2 + 1) * dwv]

    _attn_tile(pl.program_id(2), maps, scores, values)

    for g, (_, _, acc_ref) in enumerate(m_maps):
        oa_ref[:, g * wv:(g + 1) * wv] = (acc_ref[:, 0:wv] / acc_ref[:, wv:2 * wv]).astype(oa_ref.dtype)
    lam = (jnp.exp(jnp.sum(lq1_ref[...] * lk1_ref[...], axis=-1, keepdims=True))
           - jnp.exp(jnp.sum(lq2_ref[...] * lk2_ref[...], axis=-1, keepdims=True))
           + LAMBDA_INIT)
    rep = dwv // V7X_LANES
    for h in range(ATT_DIFF_HEADS):
        (_, l1_ref, a1_ref), (_, l2_ref, a2_ref) = d_maps[2 * h:2 * h + 2]
        o = (a1_ref[...] / jnp.tile(l1_ref[...], (1, rep))
             - lam * (a2_ref[...] / jnp.tile(l2_ref[...], (1, rep))))
        ob_ref[:, h * dwv:(h + 1) * dwv] = (
            _rms(o, g_ref[...]) * (1.0 - LAMBDA_INIT)).astype(ob_ref.dtype)


def _attention(q, k, v, dq, dk, dv, lq1, lk1, lq2, lk2, g, batch, seq):
    nq = seq // ATT_TQ
    n_blk = MLA_HEADS // ATT_MLA_HEADS
    assert n_blk == DIFF_HEADS // ATT_DIFF_HEADS
    wq, wv = ATT_MLA_HEADS * MLA_HEAD_PAD, ATT_MLA_HEADS * MLA_V_DIM
    wd = ATT_DIFF_HEADS * DIFF_V_DIM
    tile = lambda b, h, i: (b * nq + i, h)
    full = lambda b, h, i: (b, h)
    vec = pl.BlockSpec((1, DIFF_HEAD_DIM), lambda b, h, i: (0, 0))
    nbytes = (2 * (seq + ATT_TQ) * (2 * wq + 3 * wd) * 2
              + 8 * (ATT_MLA_HEADS + 2 * ATT_DIFF_HEADS) * ATT_TQ * ATT_TK * 4)
    return pl.pallas_call(
        _attn_kernel,
        grid=(batch, n_blk, nq),
        in_specs=[
            pl.BlockSpec((ATT_TQ, wq), tile),
            pl.BlockSpec((seq, wq), full),
            pl.BlockSpec((seq, 2 * wv), full),
            pl.BlockSpec((ATT_TQ, wd), tile),
            pl.BlockSpec((seq, wd), full),
            pl.BlockSpec((seq, wd), full),
            vec, vec, vec, vec,
            pl.BlockSpec((1, DIFF_V_DIM), lambda b, h, i: (0, 0)),
        ],
        out_specs=[pl.BlockSpec((ATT_TQ, wv), tile), pl.BlockSpec((ATT_TQ, wd), tile)],
        out_shape=[jax.ShapeDtypeStruct((batch * seq, MLA_OUT), BF16),
                   jax.ShapeDtypeStruct((batch * seq, DIFF_OUT), BF16)],
        scratch_shapes=(_flash_scratch(ATT_MLA_HEADS, 2 * MLA_V_DIM, with_sum=False)
                        + _flash_scratch(2 * ATT_DIFF_HEADS, DIFF_V_DIM, with_sum=True)),
        compiler_params=pltpu.CompilerParams(
            dimension_semantics=("parallel", "parallel", "parallel"),
            vmem_limit_bytes=_vmem_limit(nbytes)),
        name="attention",
    )(q, k, v, dq, dk, dv, lq1, lk1, lq2, lk2, g)


def _out_kernel(x_ref, oa_ref, ob_ref, wa_ref, wb_ref, g_ref, o_ref):
    o = _dot_bf16(oa_ref[...], wa_ref[...]) + _dot_bf16(ob_ref[...], wb_ref[...])
    o_ref[...] = x_ref[...] + _rms(o, g_ref[...])


def _out_proj(x, oa, ob, w, g):
    t, d = x.shape
    assert MLA_OUT == DIFF_OUT, "the two row blocks of w share one block shape"
    tm = OUT_TM
    const = lambda i: (0, 0)
    row = lambda i: (i, 0)
    once = dict(pipeline_mode=pl.Buffered(1))
    nbytes = (d * d * (w.dtype.itemsize + 2) + 4 * tm * d * 4 + 2 * tm * d * 2 + 2 * tm * d * 4)
    return pl.pallas_call(
        _out_kernel,
        grid=(t // tm,),
        in_specs=[
            pl.BlockSpec((tm, d), row),
            pl.BlockSpec((tm, MLA_OUT), row),
            pl.BlockSpec((tm, DIFF_OUT), row),
            pl.BlockSpec((MLA_OUT, d), const, **once),
            pl.BlockSpec((DIFF_OUT, d), lambda i: (1, 0), **once),
            pl.BlockSpec((1, d), const),
        ],
        out_specs=pl.BlockSpec((tm, d), row),
        out_shape=jax.ShapeDtypeStruct((t, d), F32),
        compiler_params=pltpu.CompilerParams(
            dimension_semantics=("parallel",),
            vmem_limit_bytes=_vmem_limit(nbytes)),
        name="out_proj",
    )(x, oa, ob, w, w, g)


def _rope_tables(seq):
    pos = np.arange(seq, dtype=np.float64)[:, None]

    def angles(d):
        inv_freq = ROPE_THETA ** (-np.arange(0, d, 2, dtype=np.float64) / d)
        return pos * inv_freq[None, :]

    ang = angles(MLA_ROPE_DIM)
    zeros = np.zeros((seq, V7X_LANES - MLA_ROPE_DIM))
    half0 = np.zeros_like(ang)
    cos_m = np.concatenate([np.cos(ang), np.cos(ang), zeros], axis=1)
    sin_up = np.concatenate([half0, np.sin(ang), zeros], axis=1)
    sin_dn = np.concatenate([-np.sin(ang), half0, zeros], axis=1)
    ang = angles(DIFF_HEAD_DIM)
    cos_d = np.concatenate([np.cos(ang), np.cos(ang)], axis=1)
    sin_d = np.concatenate([-np.sin(ang), np.sin(ang)], axis=1)
    return tuple(jnp.asarray(t, F32) for t in (cos_m, sin_up, sin_dn, cos_d, sin_d))


def kernel(x, ffn1_pre_g, ffn1_w_gate, ffn1_w_up, ffn1_w_down, ffn1_post_g, mix_pre_g, w_in, mla_q_norm_g, mla_w_uq, mla_kv_norm_g, mla_w_ukv, diff_lambda_q1, diff_lambda_k1, diff_lambda_q2, diff_lambda_k2, diff_subln_g, w_out, mix_post_g, ffn2_pre_g, ffn2_w_gate, ffn2_w_up, ffn2_w_down, ffn2_post_g):
    batch, seq, d = x.shape
    depth = ffn1_pre_g.shape[0]
    assert depth == 1 and d == D_MODEL
    assert seq % ATT_TQ == 0 and seq % PROJ_TM == 0 and (batch * seq) % FFN_TM == 0
    tables = _rope_tables(seq)
    xt = x.reshape(batch * seq, d)
    for l in range(depth):
        xt, w_in_bf = _ffn(xt, ffn1_pre_g[l][None], ffn1_w_gate[l], ffn1_w_up[l], ffn1_w_down[l],
                           ffn1_post_g[l][None], cast_src=w_in[l].T)
        q, k, v, dq, dk, dv = _mix_proj(
            xt, mix_pre_g[l][None], w_in_bf, mla_q_norm_g[l][None],
            mla_w_uq[l], mla_kv_norm_g[l][None], mla_w_ukv[l],
            tables, seq)
        o_mla, o_diff = _attention(
            q, k, v, dq, dk, dv, diff_lambda_q1[l][None], diff_lambda_k1[l][None],
            diff_lambda_q2[l][None], diff_lambda_k2[l][None], diff_subln_g[l][None], batch, seq)
        xt = _out_proj(xt, o_mla, o_diff, w_out[l], mix_post_g[l][None])
        xt = _ffn(xt, ffn2_pre_g[l][None], ffn2_w_gate[l], ffn2_w_up[l], ffn2_w_down[l],
                  ffn2_post_g[l][None])
    return xt.reshape(batch, seq, d)
```

```python
import functools
import math

import jax
import jax.numpy as jnp
import numpy as np
from jax import lax
from jax.experimental import pallas as pl
from jax.experimental.pallas import tpu as pltpu

D_MODEL = 2048
CHUNK = 64
ROPE_THETA = 10000.0
NORM_EPS = 1e-6
MLA_HEADS = 8
MLA_Q_RANK = 512
MLA_KV_RANK = 512
MLA_NOPE_DIM = 128
MLA_ROPE_DIM = 64
MLA_V_DIM = 128
MLA_QK_DIM = MLA_NOPE_DIM + MLA_ROPE_DIM
DIFF_HEADS = 4
DIFF_HEAD_DIM = 128
DIFF_V_DIM = 2 * DIFF_HEAD_DIM
MLA_OUT = MLA_HEADS * MLA_V_DIM
DIFF_OUT = DIFF_HEADS * DIFF_V_DIM
DIFF_QK_W = DIFF_HEADS * 2 * DIFF_HEAD_DIM
D_FF = 5632
LAMBDA_INIT = 0.8 - 0.6 * math.exp(-0.3 * 0)
LOG2_E = math.log2(math.e)
MLA_SCORE_SCALE = MLA_QK_DIM ** -0.5 * LOG2_E
DIFF_SCORE_SCALE = DIFF_HEAD_DIM ** -0.5 * LOG2_E

V7X_LANES = 128
V7X_VMEM_BYTES = 64 * 1024 * 1024
V7X_DEFAULT_SCOPED_VMEM = 32 * 1024 * 1024
V7X_VMEM_RESERVE = 4 * 1024 * 1024
MLA_HEAD_PAD = 2 * V7X_LANES

FFN_TM = 1024
FFN_TF = 512
FFN_ROWS = 256
FFN_CAST_ROWS = 64
PROJ_TM = 512
ATT_TQ = 512
ATT_TK = 512
ATT_KGROUP = 2
ATT_MLA_HEADS = 4
ATT_DIFF_HEADS = 2
OUT_TM = 512

BF16 = jnp.bfloat16
F32 = jnp.float32


def _vmem_limit(nbytes):
    return int(min(V7X_VMEM_BYTES - V7X_VMEM_RESERVE, max(V7X_DEFAULT_SCOPED_VMEM, nbytes * 3 // 2)))


def _dot_bf16(a, w):
    return lax.dot_general(a, w, (((1,), (0,)), ((), ())), preferred_element_type=F32)


def _dot_nt(a, w_t):
    return lax.dot_general(a, w_t, (((1,), (1,)), ((), ())), preferred_element_type=F32)


def _rms(v, g):
    ms = jnp.mean(v * v, axis=-1, keepdims=True)
    return v * lax.rsqrt(ms + NORM_EPS) * g


def _swiglu_hidden(h, wg, wu):
    g = _dot_bf16(h, wg)
    u = _dot_bf16(h, wu)
    return (g * (1.0 / (1.0 + jnp.exp(-g))) * u).astype(BF16)


def _ffn_kernel(*refs, n_cast_slabs):
    if n_cast_slabs:
        (x_ref, pre_g_ref, wg_ref, wu_ref, wd_ref, post_g_ref, src_ref,
         o_hbm, dst_ref, acc_ref, h_ref, sem) = refs
    else:
        x_ref, pre_g_ref, wg_ref, wu_ref, wd_ref, post_g_ref, o_hbm, acc_ref, h_ref, sem = refs
    i = pl.program_id(0)
    f = pl.program_id(1)
    last = pl.num_programs(1) - 1
    if n_cast_slabs:
        @pl.when(i * pl.num_programs(1) + f < n_cast_slabs)
        def _():
            dst_ref[...] = src_ref[...].astype(BF16)
    n_chunks = FFN_TM // FFN_ROWS
    chunks = [pl.ds(c * FFN_ROWS, FFN_ROWS) for c in range(n_chunks)]

    def out_copy(tile, c):
        dst = o_hbm.at[pl.ds(pl.multiple_of(tile * FFN_TM + c * FFN_ROWS, FFN_ROWS), FFN_ROWS)]
        return pltpu.make_async_copy(acc_ref.at[chunks[c]], dst, sem.at[c])

    @pl.when(jnp.logical_and(f == 0, i > 0))
    def _():
        for c in range(n_chunks):
            out_copy(i - 1, c).wait()

    @pl.when(f == 0)
    def _():
        for rows in chunks:
            h = _rms(x_ref[rows, :], pre_g_ref[...]).astype(BF16)
            h_ref[rows, :] = h
            acc_ref[rows, :] = _dot_bf16(_swiglu_hidden(h, wg_ref[...], wu_ref[...]), wd_ref[...])

    @pl.when(jnp.logical_and(f > 0, f < last))
    def _():
        a = _swiglu_hidden(h_ref[...], wg_ref[...], wu_ref[...])
        acc_ref[...] += _dot_bf16(a, wd_ref[...])

    @pl.when(f == last)
    def _():
        for c, rows in enumerate(chunks):
            a = _swiglu_hidden(h_ref[rows, :], wg_ref[...], wu_ref[...])
            acc = acc_ref[rows, :] + _dot_bf16(a, wd_ref[...])
            acc_ref[rows, :] = x_ref[rows, :] + 0.5 * _rms(acc, post_g_ref[...])
            out_copy(i, c).start()

        @pl.when(i == pl.num_programs(0) - 1)
        def _():
            for c in range(n_chunks):
                out_copy(i, c).wait()


def _ffn(x, pre_g, wg, wu, wd, post_g, cast_src=None):
    t, d = x.shape
    n_f = D_FF // FFN_TF
    assert n_f >= 2, "first and last hidden steps must be distinct"
    nbytes = (3 * FFN_TM * d * 4 + FFN_TM * d * 2 + 2 * 3 * d * FFN_TF * wg.dtype.itemsize
              + 3 * FFN_TM * FFN_TF * 4)
    in_specs = [
        pl.BlockSpec((FFN_TM, d), lambda i, f: (i, 0)),
        pl.BlockSpec((1, d), lambda i, f: (0, 0)),
        pl.BlockSpec((d, FFN_TF), lambda i, f: (0, f)),
        pl.BlockSpec((d, FFN_TF), lambda i, f: (0, f)),
        pl.BlockSpec((FFN_TF, d), lambda i, f: (f, 0)),
        pl.BlockSpec((1, d), lambda i, f: (0, 0)),
    ]
    out_specs = [pl.BlockSpec(memory_space=pl.ANY)]
    out_shape = [jax.ShapeDtypeStruct((t, d), F32)]
    operands = [x, pre_g, wg, wu, wd, post_g]
    n_slabs = 0
    if cast_src is not None:
        rows, cols = cast_src.shape
        n_slabs = rows // FFN_CAST_ROWS
        assert rows % FFN_CAST_ROWS == 0 and n_slabs <= (t // FFN_TM) * n_f
        slab = pl.BlockSpec((FFN_CAST_ROWS, cols),
                            lambda i, f: (jnp.minimum(i * n_f + f, n_slabs - 1), 0))
        in_specs.append(slab)
        out_specs.append(slab)
        out_shape.append(jax.ShapeDtypeStruct((rows, cols), BF16))
        operands.append(cast_src)
        nbytes += 2 * FFN_CAST_ROWS * cols * (4 + 2)
    outs = pl.pallas_call(
        functools.partial(_ffn_kernel, n_cast_slabs=n_slabs),
        grid=(t // FFN_TM, n_f),
        in_specs=in_specs,
        out_specs=out_specs,
        out_shape=out_shape,
        scratch_shapes=[pltpu.VMEM((FFN_TM, d), F32), pltpu.VMEM((FFN_TM, d), BF16),
                        pltpu.SemaphoreType.DMA((FFN_TM // FFN_ROWS,))],
        compiler_params=pltpu.CompilerParams(
            dimension_semantics=("arbitrary", "arbitrary"),
            vmem_limit_bytes=_vmem_limit(nbytes)),
        name="ffn",
    )(*operands)
    return outs[0] if cast_src is None else tuple(outs)


_IN_LATENT = MLA_Q_RANK + MLA_KV_RANK
_IN_W = _IN_LATENT + MLA_ROPE_DIM + 2 * DIFF_QK_W + DIFF_OUT


def _rope_half(v, cos, sin_signed):
    return v * cos + pltpu.roll(v, DIFF_HEAD_DIM // 2, 1) * sin_signed


def _rope_mla(v, cos, sin_up, sin_dn):
    half = MLA_ROPE_DIM // 2
    return (v * cos + pltpu.roll(v, half, 1) * sin_up
            + pltpu.roll(v, V7X_LANES - half, 1) * sin_dn)


def _proj_kernel(x_ref, g_ref, win_ref, qg_ref, wuq_ref, kvg_ref, wukv_ref,
                 cm_ref, sup_ref, sdn_ref, cd_ref, sd_ref,
                 q_ref, k_ref, v_ref, dq_ref, dk_ref, dv_ref):
    h = _rms(x_ref[...], g_ref[...]).astype(BF16)

    pa = _dot_nt(h, win_ref[0:_IN_LATENT, :])
    cq = _rms(pa[:, 0:MLA_Q_RANK], qg_ref[...]).astype(BF16)
    ckv = _rms(pa[:, MLA_Q_RANK:_IN_LATENT], kvg_ref[...]).astype(BF16)
    rest = _dot_nt(h, win_ref[_IN_LATENT:_IN_W, :])
    cm, sup, sdn = cm_ref[...], sup_ref[...], sdn_ref[...]
    kpe = _rope_mla(rest[:, 0:V7X_LANES], cm, sup, sdn).astype(BF16)

    q = _dot_bf16(cq, wuq_ref[...]) * MLA_SCORE_SCALE
    kv = _dot_bf16(ckv, wukv_ref[...])
    for hh in range(MLA_HEADS):
        a = hh * MLA_HEAD_PAD
        b = a + V7X_LANES
        c = b + V7X_LANES
        qa = hh * MLA_QK_DIM
        q_ref[:, a:b] = q[:, qa:qa + MLA_NOPE_DIM].astype(BF16)
        if (qa + MLA_NOPE_DIM) % V7X_LANES == 0:
            pe = q[:, qa + MLA_NOPE_DIM:qa + MLA_NOPE_DIM + V7X_LANES]
        else:
            pe = pltpu.roll(q[:, qa + MLA_ROPE_DIM:qa + MLA_QK_DIM], MLA_ROPE_DIM, 1)
        q_ref[:, b:c] = _rope_mla(pe, cm, sup, sdn).astype(BF16)
        k_ref[:, a:b] = kv[:, a:b].astype(BF16)
        k_ref[:, b:c] = kpe
        v_ref[:, hh * MLA_V_DIM:(hh + 1) * MLA_V_DIM] = kv[:, b:c].astype(BF16)

    cd, sd = cd_ref[...], sd_ref[...]
    off = MLA_ROPE_DIM
    pdq = rest[:, off:off + DIFF_QK_W] * DIFF_SCORE_SCALE
    for c in range(DIFF_QK_W // DIFF_HEAD_DIM):
        cols = slice(c * DIFF_HEAD_DIM, (c + 1) * DIFF_HEAD_DIM)
        dq_ref[:, cols] = _rope_half(pdq[:, cols], cd, sd).astype(BF16)
    pdk = rest[:, off + DIFF_QK_W:off + 2 * DIFF_QK_W]
    for c in range(DIFF_QK_W // DIFF_HEAD_DIM):
        cols = slice(c * DIFF_HEAD_DIM, (c + 1) * DIFF_HEAD_DIM)
        dk_ref[:, cols] = _rope_half(pdk[:, cols], cd, sd).astype(BF16)
    dv_ref[...] = rest[:, off + 2 * DIFF_QK_W:].astype(BF16)


def _mix_proj(x, g, w_in_t, qg, wuq, kvg, wukv, tables, seq):
    t, d = x.shape
    tm = PROJ_TM
    pos_blocks = seq // tm
    const = lambda i: (0, 0)
    row = lambda i: (i, 0)
    pos = lambda i: (i % pos_blocks, 0)
    wq = MLA_HEADS * MLA_HEAD_PAD
    out_w = (wq, wq, MLA_OUT, DIFF_QK_W, DIFF_QK_W, DIFF_OUT)
    once = dict(pipeline_mode=pl.Buffered(1))
    nbytes = (d * _IN_W * 2 + MLA_Q_RANK * wq * (4 + 4) + 2 * tm * d * 4
              + 2 * tm * sum(out_w) * 2 + tm * _IN_W * 4 + 2 * tm * wq * 4)
    return pl.pallas_call(
        _proj_kernel,
        grid=(t // tm,),
        in_specs=[
            pl.BlockSpec((tm, d), row),
            pl.BlockSpec((1, d), const),
            pl.BlockSpec((_IN_W, d), const, **once),
            pl.BlockSpec((1, MLA_Q_RANK), const),
            pl.BlockSpec((MLA_Q_RANK, MLA_HEADS * MLA_QK_DIM), const, **once),
            pl.BlockSpec((1, MLA_KV_RANK), const),
            pl.BlockSpec((MLA_KV_RANK, wq), const, **once),
        ] + [pl.BlockSpec((tm, V7X_LANES), pos)] * 5,
        out_specs=[pl.BlockSpec((tm, w), row) for w in out_w],
        out_shape=[jax.ShapeDtypeStruct((t, w), BF16) for w in out_w],
        compiler_params=pltpu.CompilerParams(
            dimension_semantics=("parallel",),
            vmem_limit_bytes=_vmem_limit(nbytes)),
        name="mix_proj",
    )(x, g, w_in_t, qg, wuq, kvg, wukv, *tables)


def _causal_mask(n_q, n_k):
    r = lax.broadcasted_iota(jnp.int32, (n_q, n_k), 0) // CHUNK
    c = lax.broadcasted_iota(jnp.int32, (n_q, n_k), 1) // CHUNK
    return r >= c


def _flash_scratch(n_maps, acc_width, with_sum):
    stat = pltpu.VMEM((ATT_TQ, V7X_LANES), F32)
    per_map = [stat] * (2 if with_sum else 1) + [pltpu.VMEM((ATT_TQ, acc_width), F32)]
    return per_map * n_maps


def _flash_maps(state, n_maps, with_sum):
    k = len(state) // n_maps
    return [(state[k * n], state[k * n + 1] if with_sum else None, state[k * n + k - 1])
            for n in range(n_maps)]


def _flash_step(t, v, m_ref, l_ref, acc_ref):
    lanes = m_ref.shape[1]
    m_old = m_ref[...]
    m_new = jnp.maximum(m_old, jnp.max(t, axis=-1, keepdims=True))
    alpha = jnp.exp2(m_old - m_new)
    p = jnp.exp2(t - jnp.tile(m_new, (1, t.shape[1] // lanes)))
    if l_ref is not None:
        l_ref[...] = alpha * l_ref[...] + jnp.sum(p, axis=-1, keepdims=True)
    pv = jnp.dot(p.astype(BF16), v, preferred_element_type=F32)
    acc_ref[...] = jnp.tile(alpha, (1, acc_ref.shape[1] // lanes)) * acc_ref[...] + pv
    m_ref[...] = m_new


def _attn_tile(qi, maps, scores, values):
    half = ATT_TQ // 2
    assert ATT_TQ == ATT_TK and half % CHUNK == 0

    def update(q_rows, k_rows, mask):
        for n, refs in enumerate(maps):
            t = scores(n, q_rows, k_rows)
            if mask is not None:
                t = jnp.where(mask, t, -jnp.inf)
            _flash_step(t, values(n, k_rows), *[r if r is None else r.at[q_rows] for r in refs])

    for m_ref, l_ref, acc_ref in maps:
        m_ref[...] = jnp.full(m_ref.shape, -jnp.inf, F32)
        if l_ref is not None:
            l_ref[...] = jnp.zeros(l_ref.shape, F32)
        acc_ref[...] = jnp.zeros(acc_ref.shape, F32)
    base = pl.multiple_of(qi * ATT_TQ, ATT_TQ)
    update(pl.ds(0, ATT_TQ), pl.ds(base, half), _causal_mask(ATT_TQ, half))
    update(pl.ds(half, half), pl.ds(pl.multiple_of(base + half, half), half),
           _causal_mask(half, half))

    grp = ATT_KGROUP

    def body(j, carry):
        update(pl.ds(0, ATT_TQ),
               pl.ds(pl.multiple_of(j * grp * ATT_TK, grp * ATT_TK), grp * ATT_TK), None)
        return carry

    lax.fori_loop(0, qi // grp, body, 0)
    for rem in range(1, grp):
        @pl.when(qi % grp == rem)
        def _():
            start = pl.multiple_of((qi - rem) * ATT_TK, ATT_TK)
            update(pl.ds(0, ATT_TQ), pl.ds(start, rem * ATT_TK), None)


def _qk(q, k):
    return lax.dot_general(q, k, (((1,), (1,)), ((), ())), preferred_element_type=F32)


def _attn_kernel(q_ref, k_ref, v_ref, dq_ref, dk_ref, dv_ref, lq1_ref, lk1_ref, lq2_ref, lk2_ref,
                 g_ref, oa_ref, ob_ref, *state):
    wq, wv = MLA_HEAD_PAD, MLA_V_DIM
    d, dwv = DIFF_HEAD_DIM, DIFF_V_DIM
    n_m, n_d = ATT_MLA_HEADS, 2 * ATT_DIFF_HEADS
    m_maps = _flash_maps(state[:2 * n_m], n_m, with_sum=False)
    d_maps = _flash_maps(state[2 * n_m:], n_d, with_sum=True)
    kinds, maps = [], []
    for j in range(max(n_m, n_d)):
        if j < n_m:
            kinds.append(("mla", j))
            maps.append(m_maps[j])
        if j < n_d:
            kinds.append(("diff", j))
            maps.append(d_maps[j])

    def scores(n, q_rows, k_rows):
        kind, j = kinds[n]
        if kind == "mla":
            return _qk(q_ref[q_rows, j * wq:(j + 1) * wq], k_ref[k_rows, j * wq:(j + 1) * wq])
        return _qk(dq_ref[q_rows, j * d:(j + 1) * d], dk_ref[k_rows, j * d:(j + 1) * d])

    def values(n, k_rows):
        kind, j = kinds[n]
        if kind == "mla":
            v = v_ref[k_rows, j * wv:(j + 1) * wv]
            return jnp.concatenate([v, jnp.ones(v.shape, v.dtype)], axis=1)
        return dv_ref[k_rows, (j // 2) * dwv:(j // 2 + 1) * dwv]

    _attn_tile(pl.program_id(2), maps, scores, values)

    for g, (_, _, acc_ref) in enumerate(m_maps):
        oa_ref[:, g * wv:(g + 1) * wv] = (acc_ref[:, 0:wv] / acc_ref[:, wv:2 * wv]).astype(oa_ref.dtype)
    lam = (jnp.exp(jnp.sum(lq1_ref[...] * lk1_ref[...], axis=-1, keepdims=True))
           - jnp.exp(jnp.sum(lq2_ref[...] * lk2_ref[...], axis=-1, keepdims=True))
           + LAMBDA_INIT)
    rep = dwv // V7X_LANES
    for h in range(ATT_DIFF_HEADS):
        (_, l1_ref, a1_ref), (_, l2_ref, a2_ref) = d_maps[2 * h:2 * h + 2]
        o = (a1_ref[...] / jnp.tile(l1_ref[...], (1, rep))
             - lam * (a2_ref[...] / jnp.tile(l2_ref[...], (1, rep))))
        ob_ref[:, h * dwv:(h + 1) * dwv] = (
            _rms(o, g_ref[...]) * (1.0 - LAMBDA_INIT)).astype(ob_ref.dtype)


def _attention(q, k, v, dq, dk, dv, lq1, lk1, lq2, lk2, g, batch, seq):
    nq = seq // ATT_TQ
    n_blk = MLA_HEADS // ATT_MLA_HEADS
    assert n_blk == DIFF_HEADS // ATT_DIFF_HEADS
    wq, wv = ATT_MLA_HEADS * MLA_HEAD_PAD, ATT_MLA_HEADS * MLA_V_DIM
    wd = ATT_DIFF_HEADS * DIFF_V_DIM
    tile = lambda b, h, i: (b * nq + i, h)
    full = lambda b, h, i: (b, h)
    vec = pl.BlockSpec((1, DIFF_HEAD_DIM), lambda b, h, i: (0, 0))
    nbytes = (2 * (seq + ATT_TQ) * (2 * wq + 3 * wd) * 2
              + 8 * (ATT_MLA_HEADS + 2 * ATT_DIFF_HEADS) * ATT_TQ * ATT_TK * 4)
    return pl.pallas_call(
        _attn_kernel,
        grid=(batch, n_blk, nq),
        in_specs=[
            pl.BlockSpec((ATT_TQ, wq), tile),
            pl.BlockSpec((seq, wq), full),
            pl.BlockSpec((seq, wv), full),
            pl.BlockSpec((ATT_TQ, wd), tile),
            pl.BlockSpec((seq, wd), full),
            pl.BlockSpec((seq, wd), full),
            vec, vec, vec, vec,
            pl.BlockSpec((1, DIFF_V_DIM), lambda b, h, i: (0, 0)),
        ],
        out_specs=[pl.BlockSpec((ATT_TQ, wv), tile), pl.BlockSpec((ATT_TQ, wd), tile)],
        out_shape=[jax.ShapeDtypeStruct((batch * seq, MLA_OUT), BF16),
                   jax.ShapeDtypeStruct((batch * seq, DIFF_OUT), BF16)],
        scratch_shapes=(_flash_scratch(ATT_MLA_HEADS, 2 * MLA_V_DIM, with_sum=False)
                        + _flash_scratch(2 * ATT_DIFF_HEADS, DIFF_V_DIM, with_sum=True)),
        compiler_params=pltpu.CompilerParams(
            dimension_semantics=("parallel", "parallel", "parallel"),
            vmem_limit_bytes=_vmem_limit(nbytes)),
        name="attention",
    )(q, k, v, dq, dk, dv, lq1, lk1, lq2, lk2, g)


def _out_kernel(x_ref, oa_ref, ob_ref, wa_ref, wb_ref, g_ref, o_ref):
    o = _dot_bf16(oa_ref[...], wa_ref[...]) + _dot_bf16(ob_ref[...], wb_ref[...])
    o_ref[...] = x_ref[...] + _rms(o, g_ref[...])


def _out_proj(x, oa, ob, w, g):
    t, d = x.shape
    assert MLA_OUT == DIFF_OUT, "the two row blocks of w share one block shape"
    tm = OUT_TM
    const = lambda i: (0, 0)
    row = lambda i: (i, 0)
    once = dict(pipeline_mode=pl.Buffered(1))
    nbytes = (d * d * (w.dtype.itemsize + 2) + 4 * tm * d * 4 + 2 * tm * d * 2 + 2 * tm * d * 4)
    return pl.pallas_call(
        _out_kernel,
        grid=(t // tm,),
        in_specs=[
            pl.BlockSpec((tm, d), row),
            pl.BlockSpec((tm, MLA_OUT), row),
            pl.BlockSpec((tm, DIFF_OUT), row),
            pl.BlockSpec((MLA_OUT, d), const, **once),
            pl.BlockSpec((DIFF_OUT, d), lambda i: (1, 0), **once),
            pl.BlockSpec((1, d), const),
        ],
        out_specs=pl.BlockSpec((tm, d), row),
        out_shape=jax.ShapeDtypeStruct((t, d), F32),
        compiler_params=pltpu.CompilerParams(
            dimension_semantics=("parallel",),
            vmem_limit_bytes=_vmem_limit(nbytes)),
        name="out_proj",
    )(x, oa, ob, w, w, g)


def _rope_tables(seq):
    pos = np.arange(seq, dtype=np.float64)[:, None]

    def angles(d):
        inv_freq = ROPE_THETA ** (-np.arange(0, d, 2, dtype=np.float64) / d)
        return pos * inv_freq[None, :]

    ang = angles(MLA_ROPE_DIM)
    zeros = np.zeros((seq, V7X_LANES - MLA_ROPE_DIM))
    half0 = np.zeros_like(ang)
    cos_m = np.concatenate([np.cos(ang), np.cos(ang), zeros], axis=1)
    sin_up = np.concatenate([half0, np.sin(ang), zeros], axis=1)
    sin_dn = np.concatenate([-np.sin(ang), half0, zeros], axis=1)
    ang = angles(DIFF_HEAD_DIM)
    cos_d = np.concatenate([np.cos(ang), np.cos(ang)], axis=1)
    sin_d = np.concatenate([-np.sin(ang), np.sin(ang)], axis=1)
    return tuple(jnp.asarray(t, F32) for t in (cos_m, sin_up, sin_dn, cos_d, sin_d))


def kernel(x, ffn1_pre_g, ffn1_w_gate, ffn1_w_up, ffn1_w_down, ffn1_post_g, mix_pre_g, w_in, mla_q_norm_g, mla_w_uq, mla_kv_norm_g, mla_w_ukv, diff_lambda_q1, diff_lambda_k1, diff_lambda_q2, diff_lambda_k2, diff_subln_g, w_out, mix_post_g, ffn2_pre_g, ffn2_w_gate, ffn2_w_up, ffn2_w_down, ffn2_post_g):
    batch, seq, d = x.shape
    depth = ffn1_pre_g.shape[0]
    assert depth == 1 and d == D_MODEL
    assert seq % ATT_TQ == 0 and seq % PROJ_TM == 0 and (batch * seq) % FFN_TM == 0
    tables = _rope_tables(seq)
    xt = x.reshape(batch * seq, d)
    for l in range(depth):
        xt, w_in_bf = _ffn(xt, ffn1_pre_g[l][None], ffn1_w_gate[l], ffn1_w_up[l], ffn1_w_down[l],
                           ffn1_post_g[l][None], cast_src=w_in[l].T)
        q, k, v, dq, dk, dv = _mix_proj(
            xt, mix_pre_g[l][None], w_in_bf, mla_q_norm_g[l][None],
            mla_w_uq[l], mla_kv_norm_g[l][None], mla_w_ukv[l],
            tables, seq)
        o_mla, o_diff = _attention(
            q, k, v, dq, dk, dv, diff_lambda_q1[l][None], diff_lambda_k1[l][None],
            diff_lambda_q2[l][None], diff_lambda_k2[l][None], diff_subln_g[l][None], batch, seq)
        xt = _out_proj(xt, o_mla, o_diff, w_out[l], mix_post_g[l][None])
        xt = _ffn(xt, ffn2_pre_g[l][None], ffn2_w_gate[l], ffn2_w_up[l], ffn2_w_down[l],
                  ffn2_post_g[l][None])
    return xt.reshape(batch, seq, d)
```
